```python
import jax
import jax.numpy as jnp
from jax import lax
import numpy as np

D_MODEL = 2048
BATCH = 4
SEQ = 4096
DEPTH = 4

CHUNK = 64
HEAD_DIM = 128
N_MIX_HEADS = D_MODEL // HEAD_DIM
FOX_HEADS = N_MIX_HEADS // 2
RET_HEADS = N_MIX_HEADS // 4
GMLP_GROUPS = N_MIX_HEADS - FOX_HEADS - RET_HEADS
FOX_W = FOX_HEADS * HEAD_DIM
RET_W = RET_HEADS * HEAD_DIM
GMLP_W = GMLP_GROUPS * HEAD_DIM
MIX_W = FOX_W + RET_W + GMLP_W
IN_SPLIT_SIZES = (FOX_W, FOX_W, FOX_W, FOX_HEADS, RET_W, RET_W, RET_W, RET_W, GMLP_W, GMLP_W)
IN_COLS = sum(IN_SPLIT_SIZES)
FOX_Q_BLOCK = 128
GMLP_CHUNK = 128
D_FF = ((8 * D_MODEL // 3 + 255) // 256) * 256
ROPE_BASE = 10000.0
RET_GAMMA_BASE = 5.0
EPS = 1e-6

kernel_name = "hybrid_fox_retention_gmlp_macaron"


def rmsnorm(x, g):
    x32 = x.astype(jnp.float32)
    y = x32 * lax.rsqrt(jnp.mean(x32 * x32, axis=-1, keepdims=True) + EPS)
    return (y * g.astype(jnp.float32)).astype(x.dtype)


def swiglu(h, w_gate, w_up, w_down):
    return (jax.nn.silu(h @ w_gate) * (h @ w_up)) @ w_down


def rope(x, pos):
    half = x.shape[-1] // 2
    inv_freq = ROPE_BASE ** (-jnp.arange(half, dtype=jnp.float32) / half)
    ang = pos[:, None] * inv_freq[None, :]
    cos = jnp.cos(ang)[None, :, None, :]
    sin = jnp.sin(ang)[None, :, None, :]
    x1, x2 = x[..., :half], x[..., half:]
    return jnp.concatenate([x1 * cos - x2 * sin, x1 * sin + x2 * cos], axis=-1)


def forgetting_attention(q, k, v, log_f):
    S = q.shape[1]
    scale = q.shape[-1] ** -0.5
    c = jnp.transpose(jnp.cumsum(log_f, axis=1), (0, 2, 1))
    outs = []
    for i in range(S // FOX_Q_BLOCK):
        q0, q1 = i * FOX_Q_BLOCK, (i + 1) * FOX_Q_BLOCK
        s = jnp.einsum("bqhd,bkhd->bhqk", q[:, q0:q1], k[:, :q1]) * scale
        s = s + c[:, :, q0:q1, None] - c[:, :, None, :q1]
        causal = jnp.arange(q0, q1)[:, None] >= jnp.arange(q1)[None, :]
        s = jnp.where(causal[None, None], s, -jnp.inf)
        p = jax.nn.softmax(s, axis=-1)
        outs.append(jnp.einsum("bhqk,bkhd->bqhd", p, v[:, :q1]))
    return jnp.concatenate(outs, axis=1)


def retention(q, k, v):
    B, S, H, D = q.shape
    N = S // CHUNK
    log_g = jnp.log1p(-jnp.exp2(-(RET_GAMMA_BASE + jnp.arange(H, dtype=jnp.float32))))
    k = k * (D ** -0.5)
    qc = q.reshape(B, N, CHUNK, H, D)
    kc = k.reshape(B, N, CHUNK, H, D)
    vc = v.reshape(B, N, CHUNK, H, D)
    idx = jnp.arange(CHUNK, dtype=jnp.float32)
    diff = idx[:, None] - idx[None, :]
    decay_mask = jnp.where(diff[None] >= 0, jnp.exp(jnp.maximum(diff, 0.0)[None] * log_g[:, None, None]), 0.0)
    scores = jnp.einsum("bnchd,bnshd->bnhcs", qc, kc) * decay_mask[None, None]
    o_inner = jnp.einsum("bnhcs,bnshe->bnche", scores, vc)
    zeta = jnp.exp((CHUNK - 1 - idx)[:, None] * log_g[None, :])
    kv = jnp.einsum("bnshd,bnshe->nbhde", kc * zeta[None, None, :, :, None], vc)
    chunk_decay = jnp.exp(CHUNK * log_g)[None, :, None, None]

    def step(state, kv_n):
        return state * chunk_decay + kv_n, state

    _, state_prev = lax.scan(step, jnp.zeros((B, H, D, D), jnp.float32), kv)
    xi = jnp.exp((idx + 1.0)[:, None] * log_g[None, :])
    o_cross = jnp.einsum("bnchd,nbhde->bnche", qc, state_prev) * xi[None, None, :, :, None]
    return (o_inner + o_cross).reshape(B, S, H, D)


def chunk_gmlp(u, v, ln_g, ln_b, w_s, b_s):
    B, S, G, Dg = v.shape
    mu = jnp.mean(v, axis=-1, keepdims=True)
    var = jnp.mean(jnp.square(v - mu), axis=-1, keepdims=True)
    v = (v - mu) * lax.rsqrt(var + EPS) * ln_g.reshape(G, Dg) + ln_b.reshape(G, Dg)
    v = v.reshape(B, S // GMLP_CHUNK, GMLP_CHUNK, G, Dg)
    tril = jnp.tril(jnp.ones((GMLP_CHUNK, GMLP_CHUNK), jnp.float32))
    v_mix = jnp.einsum("gts,bnsgc->bntgc", w_s * tril[None], v) + jnp.transpose(b_s)[None, None, :, :, None]
    return u * v_mix.reshape(B, S, G, Dg)


def hybrid_mixer(h, w_in, fox_b_f, gmlp_ln_g, gmlp_ln_b, gmlp_w_s, gmlp_b_s, out_norm, w_out):
    B, S, _ = h.shape
    f32 = jnp.float32
    proj = (h @ w_in).astype(f32)
    splits = np.cumsum(IN_SPLIT_SIZES)[:-1].tolist()
    fq, fk, fv, fz, rq, rk, rv, rg, gu, gv = jnp.split(proj, splits, axis=-1)

    def heads(t):
        return t.reshape(B, S, -1, HEAD_DIM)

    log_f = jax.nn.log_sigmoid(fz + fox_b_f.astype(f32))
    y_a = forgetting_attention(heads(fq), heads(fk), heads(fv), log_f)
    pos = jnp.arange(S, dtype=f32)
    y_b = retention(rope(heads(rq), pos), rope(heads(rk), pos), heads(rv))
    y_c = chunk_gmlp(heads(jax.nn.gelu(gu)), heads(jax.nn.gelu(gv)), gmlp_ln_g.astype(f32),
                     gmlp_ln_b.astype(f32), gmlp_w_s.astype(f32), gmlp_b_s.astype(f32))
    y = jnp.concatenate([y_a, y_b, y_c], axis=2)
    y = y * lax.rsqrt(jnp.mean(y * y, axis=-1, keepdims=True) + EPS) * out_norm.astype(f32).reshape(N_MIX_HEADS, HEAD_DIM)
    y_a, y_b, y_c = jnp.split(y, [FOX_HEADS, FOX_HEADS + RET_HEADS], axis=2)
    y = jnp.concatenate([y_a, y_b * jax.nn.silu(heads(rg)), y_c], axis=2).reshape(B, S, MIX_W)
    return y.astype(h.dtype) @ w_out


def setup_inputs(seed: int = 0) -> dict:
    key = jax.random.key(seed)
    ks = jax.random.split(key, 20)
    f32 = jnp.float32

    def nrm(k, shape, fan_in):
        return jax.random.normal(k, shape, f32) * (fan_in ** -0.5)

    def gain(k, shape):
        return 1.0 + 0.02 * jax.random.normal(k, shape, f32)

    return {
        "x": jax.random.normal(ks[0], (BATCH, SEQ, D_MODEL), f32),
        "ffn1_norm": gain(ks[1], (DEPTH, D_MODEL)),
        "ffn1_w_gate": nrm(ks[2], (DEPTH, D_MODEL, D_FF), D_MODEL),
        "ffn1_w_up": nrm(ks[3], (DEPTH, D_MODEL, D_FF), D_MODEL),
        "ffn1_w_down": nrm(ks[4], (DEPTH, D_FF, D_MODEL), D_FF),
        "mix_norm": gain(ks[5], (DEPTH, D_MODEL)),
        "w_in": nrm(ks[6], (DEPTH, D_MODEL, IN_COLS), D_MODEL),
        "fox_b_f": jax.random.uniform(ks[7], (DEPTH, FOX_HEADS), f32, 1.0, 5.0),
        "gmlp_ln_g": gain(ks[8], (DEPTH, GMLP_W)),
        "gmlp_ln_b": 0.02 * jax.random.normal(ks[9], (DEPTH, GMLP_W), f32),
        "gmlp_w_s": nrm(ks[10], (DEPTH, GMLP_GROUPS, GMLP_CHUNK, GMLP_CHUNK), GMLP_CHUNK),
        "gmlp_b_s": 1.0 + 0.1 * jax.random.normal(ks[11], (DEPTH, GMLP_GROUPS, GMLP_CHUNK), f32),
        "out_norm": gain(ks[12], (DEPTH, MIX_W)),
        "w_out": nrm(ks[13], (DEPTH, MIX_W, D_MODEL), MIX_W),
        "ffn2_norm": gain(ks[14], (DEPTH, D_MODEL)),
        "ffn2_w_gate": nrm(ks[15], (DEPTH, D_MODEL, D_FF), D_MODEL),
        "ffn2_w_up": nrm(ks[16], (DEPTH, D_MODEL, D_FF), D_MODEL),
        "ffn2_w_down": nrm(ks[17], (DEPTH, D_FF, D_MODEL), D_FF),
        "final_norm": gain(ks[18], (D_MODEL,)),
    }


def reference(x, ffn1_norm, ffn1_w_gate, ffn1_w_up, ffn1_w_down, mix_norm, w_in, fox_b_f,
              gmlp_ln_g, gmlp_ln_b, gmlp_w_s, gmlp_b_s, out_norm, w_out, ffn2_norm,
              ffn2_w_gate, ffn2_w_up, ffn2_w_down, final_norm):
    for l in range(DEPTH):
        x = x + 0.5 * swiglu(rmsnorm(x, ffn1_norm[l]), ffn1_w_gate[l], ffn1_w_up[l], ffn1_w_down[l])
        x = x + hybrid_mixer(rmsnorm(x, mix_norm[l]), w_in[l], fox_b_f[l], gmlp_ln_g[l], gmlp_ln_b[l],
                             gmlp_w_s[l], gmlp_b_s[l], out_norm[l], w_out[l])
        x = x + 0.5 * swiglu(rmsnorm(x, ffn2_norm[l]), ffn2_w_gate[l], ffn2_w_up[l], ffn2_w_down[l])
    return rmsnorm(x, final_norm)
```

```python
import functools

import jax
import jax.numpy as jnp
import numpy as np
from jax import lax
from jax.experimental import pallas as pl
from jax.experimental.pallas import tpu as pltpu

F32 = jnp.float32
BF16 = jnp.bfloat16

HEAD_DIM = 128
CHUNK_GMLP = 128
ROPE_BASE = 10000.0
RET_GAMMA_BASE = 5.0
EPS = 1e-6

V7X_VMEM_BYTES = 64 * 1024 * 1024
VMEM_LIMIT = V7X_VMEM_BYTES - 8 * 1024 * 1024

FFN_ROWS = 512
FFN_COLS = 512
MM_ROWS = 1024
MM_COLS = 512
FOX_BLOCK = 512
RET_CHUNK = 512
GMLP_ROWS = 512


def _params(*sem):
    return pltpu.CompilerParams(dimension_semantics=sem, vmem_limit_bytes=VMEM_LIMIT)


def _rms(x, gain):
    return x * lax.rsqrt(jnp.mean(x * x, axis=-1, keepdims=True) + EPS) * gain


def _dot(a, b):
    return jnp.dot(a, b, preferred_element_type=F32)


def _dot_nt(a, b):
    return lax.dot_general(a, b, (((1,), (1,)), ((), ())), preferred_element_type=F32)


def _ffn_kernel(x_ref, g_ref, wg_ref, wu_ref, wd_ref, g2_ref, *rest, emit_x, emit_n, n_dtype):
    outs = rest[: int(emit_x) + int(emit_n)]
    xn_sc, acc_sc = rest[int(emit_x) + int(emit_n):]
    j = pl.program_id(1)

    @pl.when(j == 0)
    def _():
        xn_sc[...] = _rms(x_ref[...], g_ref[...]).astype(BF16)
        acc_sc[...] = jnp.zeros_like(acc_sc)

    xn = xn_sc[...]
    a = _dot(xn, wg_ref[...])
    b = _dot(xn, wu_ref[...])
    hmid = (a * jax.nn.sigmoid(a) * b).astype(BF16)
    acc_sc[...] += _dot(hmid, wd_ref[...])

    @pl.when(j == pl.num_programs(1) - 1)
    def _():
        y = x_ref[...] + 0.5 * acc_sc[...]
        k = 0
        if emit_x:
            outs[k][...] = y
            k += 1
        if emit_n:
            outs[k][...] = _rms(y, g2_ref[...]).astype(n_dtype)


def _ffn(x, gain, wg, wu, wd, gain2, layer, *, emit_x, emit_n, n_dtype):
    t, d = x.shape
    f = wg.shape[-1]
    bm, tf = min(FFN_ROWS, t), FFN_COLS
    assert t % bm == 0 and f % tf == 0
    out_shape, out_specs = [], []
    if emit_x:
        out_shape.append(jax.ShapeDtypeStruct((t, d), F32))
        out_specs.append(pl.BlockSpec((bm, d), lambda i, j: (i, 0)))
    if emit_n:
        out_shape.append(jax.ShapeDtypeStruct((t, d), n_dtype))
        out_specs.append(pl.BlockSpec((bm, d), lambda i, j: (i, 0)))
    return pl.pallas_call(
        functools.partial(_ffn_kernel, emit_x=emit_x, emit_n=emit_n, n_dtype=n_dtype),
        grid=(t // bm, f // tf),
        in_specs=[
            pl.BlockSpec((bm, d), lambda i, j: (i, 0)),
            pl.BlockSpec((1, d), lambda i, j: (0, 0)),
            pl.BlockSpec((None, d, tf), lambda i, j: (layer, 0, j)),
            pl.BlockSpec((None, d, tf), lambda i, j: (layer, 0, j)),
            pl.BlockSpec((None, tf, d), lambda i, j: (layer, j, 0)),
            pl.BlockSpec((1, d), lambda i, j: (0, 0)),
        ],
        out_specs=out_specs,
        out_shape=out_shape,
        scratch_shapes=[pltpu.VMEM((bm, d), BF16), pltpu.VMEM((bm, d), F32)],
        compiler_params=_params("parallel", "arbitrary"),
        name="ffn",
    )(x, gain, wg, wu, wd, gain2)


def _proj_kernel(h_ref, w_ref, o_ref, *, scale, scaled_cols, bn):
    y = _dot(h_ref[...], w_ref[...])
    if scaled_cols:
        j = pl.program_id(1)
        y = y * jnp.where(j * bn < scaled_cols, scale, 1.0).astype(F32)
    o_ref[...] = y.astype(o_ref.dtype)


def _proj(h, w, layer, out_dtype, *, scale=1.0, scaled_cols=0):
    t, d = h.shape
    n = w.shape[-1]
    bm, bn = min(MM_ROWS, t), min(MM_COLS, n)
    assert t % bm == 0 and n % bn == 0 and scaled_cols % bn == 0
    return pl.pallas_call(
        functools.partial(_proj_kernel, scale=scale, scaled_cols=scaled_cols, bn=bn),
        grid=(t // bm, n // bn),
        in_specs=[
            pl.BlockSpec((bm, d), lambda i, j: (i, 0)),
            pl.BlockSpec((None, d, bn), lambda i, j: (layer, 0, j)),
        ],
        out_specs=pl.BlockSpec((bm, bn), lambda i, j: (i, j)),
        out_shape=jax.ShapeDtypeStruct((t, n), out_dtype),
        compiler_params=_params("parallel", "arbitrary"),
        name="proj",
    )(h, w)


def _cumsum_kernel(z_ref, b_ref, o_ref):
    s = z_ref.shape[0]
    blk = 128
    r = lax.broadcasted_iota(jnp.int32, (blk, blk), 0)
    c = lax.broadcasted_iota(jnp.int32, (blk, blk), 1)
    tril = jnp.where(r >= c, 1.0, 0.0).astype(F32)
    carry = jnp.zeros((1, z_ref.shape[1]), F32)
    for i in range(s // blk):
        z = z_ref[i * blk:(i + 1) * blk, :] + b_ref[...]
        log_f = jnp.minimum(z, 0.0) - jnp.log1p(jnp.exp(-jnp.abs(z)))
        cs = jnp.dot(tril, log_f, precision=lax.Precision.HIGHEST,
                     preferred_element_type=F32) + carry
        o_ref[i * blk:(i + 1) * blk, :] = cs
        carry = cs[blk - 1:blk, :]


def _forget_cumsum(z, bias):
    b, s, w = z.shape
    return pl.pallas_call(
        _cumsum_kernel,
        grid=(b,),
        in_specs=[
            pl.BlockSpec((None, s, w), lambda i: (i, 0, 0)),
            pl.BlockSpec((1, w), lambda i: (0, 0)),
        ],
        out_specs=pl.BlockSpec((None, s, w), lambda i: (i, 0, 0)),
        out_shape=jax.ShapeDtypeStruct((b, s, w), F32),
        compiler_params=_params("parallel"),
        name="forget_cumsum",
    )(z, bias)


def _fox_kernel(q_ref, k_ref, v_ref, cq_ref, ck_ref, g_ref, o_ref, m_sc, l_sc, acc_sc, *, blk):
    qi = pl.program_id(2)
    q = q_ref[...]
    cq = cq_ref[...]
    m_sc[...] = jnp.full_like(m_sc, -jnp.inf)
    l_sc[...] = jnp.zeros_like(l_sc)
    acc_sc[...] = jnp.zeros_like(acc_sc)

    def step(ki, diagonal):
        k0 = pl.multiple_of(ki * blk, blk)
        k = k_ref[pl.ds(k0, blk), :]
        v = v_ref[pl.ds(k0, blk), :]
        ck = ck_ref[:, pl.ds(k0, blk)]
        s = _dot_nt(q, k) + (cq - ck)
        if diagonal:
            r = lax.broadcasted_iota(jnp.int32, (blk, blk), 0)
            c = lax.broadcasted_iota(jnp.int32, (blk, blk), 1)
            s = jnp.where(r >= c, s, -jnp.inf)
        m_prev = m_sc[...]
        m_new = jnp.maximum(m_prev, jnp.max(s, axis=1, keepdims=True))
        alpha = jnp.exp(m_prev - m_new)
        p = jnp.exp(s - m_new)
        l_sc[...] = alpha * l_sc[...] + jnp.sum(p, axis=1, keepdims=True)
        acc_sc[...] = alpha * acc_sc[...] + _dot(p.astype(BF16), v)
        m_sc[...] = m_new

    def body(ki, carry):
        step(ki, False)
        return carry

    lax.fori_loop(0, qi, body, 0)
    step(qi, True)
    y = acc_sc[...] / l_sc[...]
    o_ref[...] = _rms(y, g_ref[...]).astype(o_ref.dtype)


def _fox(qkv, c_q, c_k, gain, heads):
    b, s, _ = qkv.shape
    blk = min(FOX_BLOCK, s)
    assert s % blk == 0
    return pl.pallas_call(
        functools.partial(_fox_kernel, blk=blk),
        grid=(b, heads, s // blk),
        in_specs=[
            pl.BlockSpec((None, blk, HEAD_DIM), lambda bi, h, qi: (bi, qi, h)),
            pl.BlockSpec((None, s, HEAD_DIM), lambda bi, h, qi: (bi, 0, heads + h)),
            pl.BlockSpec((None, s, HEAD_DIM), lambda bi, h, qi: (bi, 0, 2 * heads + h)),
            pl.BlockSpec((None, None, blk, 1), lambda bi, h, qi: (bi, h, qi, 0)),
            pl.BlockSpec((None, None, 1, s), lambda bi, h, qi: (bi, h, 0, 0)),
            pl.BlockSpec((1, HEAD_DIM), lambda bi, h, qi: (0, h)),
        ],
        out_specs=pl.BlockSpec((None, blk, HEAD_DIM), lambda bi, h, qi: (bi, qi, h)),
        out_shape=jax.ShapeDtypeStruct((b, s, heads * HEAD_DIM), BF16),
        scratch_shapes=[
            pltpu.VMEM((blk, 1), F32),
            pltpu.VMEM((blk, 1), F32),
            pltpu.VMEM((blk, HEAD_DIM), F32),
        ],
        compiler_params=_params("parallel", "parallel", "arbitrary"),
        name="fox",
    )(qkv, qkv, qkv, c_q, c_k, gain)


def _ret_kernel(lg_ref, q_ref, k_ref, v_ref, gate_ref, cos_ref, sin_ref, gain_ref, o_ref,
                state_sc, *, chunk):
    h = pl.program_id(1)
    n = pl.program_id(2)
    lg = lg_ref[h]

    @pl.when(n == 0)
    def _():
        state_sc[...] = jnp.zeros_like(state_sc)

    cos = cos_ref[...]
    sin = sin_ref[...]

    def rope(x):
        return x * cos + pltpu.roll(x, HEAD_DIM // 2, 1) * sin

    q = rope(q_ref[...])
    k = rope(k_ref[...]) * (HEAD_DIM ** -0.5)
    v = v_ref[...].astype(BF16)
    t = lax.broadcasted_iota(jnp.int32, (chunk, 1), 0).astype(F32)
    xi = jnp.exp((t + 1.0) * lg)
    zeta = jnp.exp((chunk - 1.0 - t) * lg)
    r = lax.broadcasted_iota(jnp.int32, (chunk, chunk), 0)
    c = lax.broadcasted_iota(jnp.int32, (chunk, chunk), 1)
    diff = (r - c).astype(F32)
    decay = jnp.where(diff >= 0.0, jnp.exp(jnp.maximum(diff, 0.0) * lg), 0.0)
    qb = q.astype(BF16)
    scores = _dot_nt(qb, k.astype(BF16)) * decay
    state = state_sc[...]
    o = _dot(scores.astype(BF16), v) + _dot(qb, state.astype(BF16)) * xi
    kz_t = jnp.transpose(k * zeta).astype(BF16)
    state_sc[...] = state * jnp.exp(chunk * lg) + _dot(kz_t, v)
    gate = gate_ref[...]
    y = _rms(o, gain_ref[...]) * (gate * jax.nn.sigmoid(gate))
    o_ref[...] = y.astype(o_ref.dtype)


def _retention(log_g, qkvg, cos2, sin2, gain, heads):
    b, s, _ = qkvg.shape
    chunk = min(RET_CHUNK, s)
    assert s % chunk == 0

    def col(off):
        return pl.BlockSpec((None, chunk, HEAD_DIM), lambda bi, h, n, lg: (bi, n, off + h))

    grid_spec = pltpu.PrefetchScalarGridSpec(
        num_scalar_prefetch=1,
        grid=(b, heads, s // chunk),
        in_specs=[
            col(0), col(heads), col(2 * heads), col(3 * heads),
            pl.BlockSpec((chunk, HEAD_DIM), lambda bi, h, n, lg: (n, 0)),
            pl.BlockSpec((chunk, HEAD_DIM), lambda bi, h, n, lg: (n, 0)),
            pl.BlockSpec((1, HEAD_DIM), lambda bi, h, n, lg: (0, h)),
        ],
        out_specs=pl.BlockSpec((None, chunk, HEAD_DIM), lambda bi, h, n, lg: (bi, n, h)),
        scratch_shapes=[pltpu.VMEM((HEAD_DIM, HEAD_DIM), F32)],
    )
    return pl.pallas_call(
        functools.partial(_ret_kernel, chunk=chunk),
        grid_spec=grid_spec,
        out_shape=jax.ShapeDtypeStruct((b, s, heads * HEAD_DIM), BF16),
        compiler_params=_params("parallel", "parallel", "arbitrary"),
        name="retention",
    )(log_g, qkvg, qkvg, qkvg, qkvg, cos2, sin2, gain)


def _gelu(x):
    return 0.5 * x * (1.0 + jnp.tanh(np.sqrt(2.0 / np.pi).astype(np.float32) * (x + 0.044715 * (x * x * x))))


def _gmlp_kernel(uv_ref, lng_ref, lnb_ref, ws_ref, bs_ref, gain_ref, o_ref, *, groups, rows):
    w = groups * HEAD_DIM
    blk = CHUNK_GMLP
    r = lax.broadcasted_iota(jnp.int32, (blk, blk), 0)
    c = lax.broadcasted_iota(jnp.int32, (blk, blk), 1)
    for g in range(groups):
        lo, hi = g * HEAD_DIM, (g + 1) * HEAD_DIM
        wm = jnp.where(r >= c, ws_ref[g], 0.0).astype(BF16)
        bias = bs_ref[g]
        for ci in range(rows // blk):
            rs = slice(ci * blk, (ci + 1) * blk)
            u = _gelu(uv_ref[rs, lo:hi])
            v = _gelu(uv_ref[rs, w + lo:w + hi])
            mu = jnp.mean(v, axis=-1, keepdims=True)
            var = jnp.mean(jnp.square(v - mu), axis=-1, keepdims=True)
            v = (v - mu) * lax.rsqrt(var + EPS) * lng_ref[:, lo:hi] + lnb_ref[:, lo:hi]
            y = u * (_dot(wm, v.astype(BF16)) + bias)
            o_ref[rs, lo:hi] = _rms(y, gain_ref[:, lo:hi]).astype(o_ref.dtype)


def _gmlp(uv, ln_g, ln_b, w_s, b_s, gain, groups):
    t, _ = uv.shape
    w = groups * HEAD_DIM
    rows = min(GMLP_ROWS, t)
    assert t % rows == 0 and rows % CHUNK_GMLP == 0
    return pl.pallas_call(
        functools.partial(_gmlp_kernel, groups=groups, rows=rows),
        grid=(t // rows,),
        in_specs=[
            pl.BlockSpec((rows, 2 * w), lambda i: (i, 0)),
            pl.BlockSpec((1, w), lambda i: (0, 0)),
            pl.BlockSpec((1, w), lambda i: (0, 0)),
            pl.BlockSpec((groups, CHUNK_GMLP, CHUNK_GMLP), lambda i: (0, 0, 0)),
            pl.BlockSpec((groups, CHUNK_GMLP, 1), lambda i: (0, 0, 0)),
            pl.BlockSpec((1, w), lambda i: (0, 0)),
        ],
        out_specs=pl.BlockSpec((rows, w), lambda i: (i, 0)),
        out_shape=jax.ShapeDtypeStruct((t, w), BF16),
        compiler_params=_params("parallel"),
        name="gmlp",
    )(uv, ln_g, ln_b, w_s, b_s, gain)


def _outproj_kernel(x_ref, ya_ref, yb_ref, yc_ref, w_ref, o_ref):
    wa, wb = ya_ref.shape[1], yb_ref.shape[1]
    acc = _dot(ya_ref[...], w_ref[0:wa, :])
    acc += _dot(yb_ref[...], w_ref[wa:wa + wb, :])
    acc += _dot(yc_ref[...], w_ref[wa + wb:, :])
    o_ref[...] = x_ref[...] + acc


def _outproj(x, ya, yb, yc, w, layer):
    t, d = x.shape
    k = w.shape[-2]
    bm, bn = min(MM_ROWS, t), min(MM_COLS, d)
    assert t % bm == 0 and d % bn == 0
    return pl.pallas_call(
        _outproj_kernel,
        grid=(t // bm, d // bn),
        in_specs=[
            pl.BlockSpec((bm, bn), lambda i, j: (i, j)),
            pl.BlockSpec((bm, ya.shape[1]), lambda i, j: (i, 0)),
            pl.BlockSpec((bm, yb.shape[1]), lambda i, j: (i, 0)),
            pl.BlockSpec((bm, yc.shape[1]), lambda i, j: (i, 0)),
            pl.BlockSpec((None, k, bn), lambda i, j: (layer, 0, j)),
        ],
        out_specs=pl.BlockSpec((bm, bn), lambda i, j: (i, j)),
        out_shape=jax.ShapeDtypeStruct((t, d), F32),
        compiler_params=_params("parallel", "arbitrary"),
        name="outproj",
    )(x, ya, yb, yc, w)


def kernel(x, ffn1_norm, ffn1_w_gate, ffn1_w_up, ffn1_w_down, mix_norm, w_in, fox_b_f,
           gmlp_ln_g, gmlp_ln_b, gmlp_w_s, gmlp_b_s, out_norm, w_out, ffn2_norm,
           ffn2_w_gate, ffn2_w_up, ffn2_w_down, final_norm):
    b, s, d = x.shape
    depth = w_in.shape[0]
    t = b * s
    n_heads = d // HEAD_DIM
    fox_h, ret_h = n_heads // 2, n_heads // 4
    gm_g = n_heads - fox_h - ret_h
    fox_w, ret_w, gm_w = fox_h * HEAD_DIM, ret_h * HEAD_DIM, gm_g * HEAD_DIM

    o_fz = 3 * fox_w
    o_ret = o_fz + fox_h
    o_gm = o_ret + 4 * ret_w
    w_fox = w_in[:, :, :o_fz].astype(BF16)
    w_fz = jnp.pad(w_in[:, :, o_fz:o_ret], ((0, 0), (0, 0), (0, HEAD_DIM - fox_h))).astype(BF16)
    w_ret = w_in[:, :, o_ret:o_gm].astype(BF16)
    w_gm = w_in[:, :, o_gm:].astype(BF16)
    w_o = w_out.astype(BF16)
    f1 = [w.astype(BF16) for w in (ffn1_w_gate, ffn1_w_up, ffn1_w_down)]
    f2 = [w.astype(BF16) for w in (ffn2_w_gate, ffn2_w_up, ffn2_w_down)]
    fz_bias = jnp.pad(fox_b_f, ((0, 0), (0, HEAD_DIM - fox_h)))

    half = HEAD_DIM // 2
    pos = jnp.arange(s, dtype=F32)
    inv_freq = ROPE_BASE ** (-jnp.arange(half, dtype=F32) / half)
    ang = pos[:, None] * inv_freq[None, :]
    cos2 = jnp.concatenate([jnp.cos(ang), jnp.cos(ang)], axis=-1)
    sin2 = jnp.concatenate([-jnp.sin(ang), jnp.sin(ang)], axis=-1)
    log_g = jnp.log1p(-jnp.exp2(-(RET_GAMMA_BASE + jnp.arange(ret_h, dtype=F32))))

    xf = x.reshape(t, d)
    for l in range(depth):
        x1, h = _ffn(xf, ffn1_norm[l][None], *f1, mix_norm[l][None], l,
                     emit_x=True, emit_n=True, n_dtype=BF16)
        qkv = _proj(h, w_fox, l, BF16, scale=HEAD_DIM ** -0.5, scaled_cols=fox_w)
        fz = _proj(h, w_fz, l, F32)
        rp = _proj(h, w_ret, l, F32)
        gp = _proj(h, w_gm, l, F32)

        c = _forget_cumsum(fz.reshape(b, s, HEAD_DIM), fz_bias[l][None])
        c = jnp.transpose(c[:, :, :fox_h], (0, 2, 1))
        gains = out_norm[l][None]
        ya = _fox(qkv.reshape(b, s, 3 * fox_w), c[..., None], c[:, :, None, :],
                  gains[:, :fox_w], fox_h)
        yb = _retention(log_g, rp.reshape(b, s, 4 * ret_w), cos2, sin2,
                        gains[:, fox_w:fox_w + ret_w], ret_h)
        yc = _gmlp(gp, gmlp_ln_g[l][None], gmlp_ln_b[l][None], gmlp_w_s[l],
                   gmlp_b_s[l][..., None], gains[:, fox_w + ret_w:], gm_g)
        x2 = _outproj(x1, ya.reshape(t, fox_w), yb.reshape(t, ret_w), yc, w_o, l)

        last = l == depth - 1
        (xf,) = _ffn(x2, ffn2_norm[l][None], *f2, final_norm[None], l,
                     emit_x=not last, emit_n=last, n_dtype=F32)
    return xf.reshape(b, s, d)
```

```python
import functools

import jax
import jax.numpy as jnp
import numpy as np
from jax import lax
from jax.experimental import pallas as pl
from jax.experimental.pallas import tpu as pltpu

F32 = jnp.float32
BF16 = jnp.bfloat16

HEAD_DIM = 128
CHUNK_GMLP = 128
ROPE_BASE = 10000.0
RET_GAMMA_BASE = 5.0
EPS = 1e-6

V7X_VMEM_BYTES = 64 * 1024 * 1024
VMEM_LIMIT = V7X_VMEM_BYTES - 8 * 1024 * 1024

FFN_ROWS = 512
FFN_COLS = 512
MM_ROWS = 1024
MM_COLS = 512
FOX_BLOCK = 512
RET_CHUNK = 512
GMLP_ROWS = 512


def _params(*sem):
    return pltpu.CompilerParams(dimension_semantics=sem, vmem_limit_bytes=VMEM_LIMIT)


def _rms(x, gain):
    return x * lax.rsqrt(jnp.mean(x * x, axis=-1, keepdims=True) + EPS) * gain


def _dot(a, b):
    return jnp.dot(a, b, preferred_element_type=F32)


def _dot_nt(a, b):
    return lax.dot_general(a, b, (((1,), (1,)), ((), ())), preferred_element_type=F32)


def _ffn_kernel(x_ref, g_ref, wg_ref, wu_ref, wd_ref, g2_ref, *rest, emit_x, emit_n, n_dtype):
    outs = rest[: int(emit_x) + int(emit_n)]
    xn_sc, acc_sc = rest[int(emit_x) + int(emit_n):]
    j = pl.program_id(1)

    @pl.when(j == 0)
    def _():
        xn_sc[...] = _rms(x_ref[...], g_ref[...]).astype(BF16)
        acc_sc[...] = jnp.zeros_like(acc_sc)

    xn = xn_sc[...]
    a = _dot(xn, wg_ref[...])
    b = _dot(xn, wu_ref[...])
    hmid = (a * jax.nn.sigmoid(a) * b).astype(BF16)
    acc_sc[...] += _dot(hmid, wd_ref[...])

    @pl.when(j == pl.num_programs(1) - 1)
    def _():
        y = x_ref[...] + 0.5 * acc_sc[...]
        k = 0
        if emit_x:
            outs[k][...] = y
            k += 1
        if emit_n:
            outs[k][...] = _rms(y, g2_ref[...]).astype(n_dtype)


def _ffn(x, gain, wg, wu, wd, gain2, layer, *, emit_x, emit_n, n_dtype):
    t, d = x.shape
    f = wg.shape[-1]
    bm, tf = min(FFN_ROWS, t), FFN_COLS
    assert t % bm == 0 and f % tf == 0
    out_shape, out_specs = [], []
    if emit_x:
        out_shape.append(jax.ShapeDtypeStruct((t, d), F32))
        out_specs.append(pl.BlockSpec((bm, d), lambda i, j: (i, 0)))
    if emit_n:
        out_shape.append(jax.ShapeDtypeStruct((t, d), n_dtype))
        out_specs.append(pl.BlockSpec((bm, d), lambda i, j: (i, 0)))
    return pl.pallas_call(
        functools.partial(_ffn_kernel, emit_x=emit_x, emit_n=emit_n, n_dtype=n_dtype),
        grid=(t // bm, f // tf),
        in_specs=[
            pl.BlockSpec((bm, d), lambda i, j: (i, 0)),
            pl.BlockSpec((1, d), lambda i, j: (0, 0)),
            pl.BlockSpec((None, d, tf), lambda i, j: (layer, 0, j)),
            pl.BlockSpec((None, d, tf), lambda i, j: (layer, 0, j)),
            pl.BlockSpec((None, tf, d), lambda i, j: (layer, j, 0)),
            pl.BlockSpec((1, d), lambda i, j: (0, 0)),
        ],
        out_specs=out_specs,
        out_shape=out_shape,
        scratch_shapes=[pltpu.VMEM((bm, d), BF16), pltpu.VMEM((bm, d), F32)],
        compiler_params=_params("parallel", "arbitrary"),
        name="ffn",
    )(x, gain, wg, wu, wd, gain2)


def _proj_kernel(h_ref, w_ref, o_ref):
    o_ref[...] = _dot(h_ref[...], w_ref[...]).astype(o_ref.dtype)


def _proj_t_kernel(wt_ref, h_ref, o_ref, *, scale, scaled_rows, bn):
    y = _dot_nt(wt_ref[...], h_ref[...])
    if scaled_rows:
        j = pl.program_id(1)
        y = y * jnp.where(j * bn < scaled_rows, scale, 1.0).astype(F32)
    o_ref[...] = y.astype(o_ref.dtype)


def _proj_t(h, wt, layer, out_dtype, *, scale=1.0, scaled_rows=0):
    t, d = h.shape
    n = wt.shape[-2]
    bm, bn = min(MM_ROWS, t), min(MM_COLS, n)
    assert t % bm == 0 and n % bn == 0 and scaled_rows % bn == 0
    return pl.pallas_call(
        functools.partial(_proj_t_kernel, scale=scale, scaled_rows=scaled_rows, bn=bn),
        grid=(t // bm, n // bn),
        in_specs=[
            pl.BlockSpec((None, bn, d), lambda i, j: (layer, j, 0)),
            pl.BlockSpec((bm, d), lambda i, j: (i, 0)),
        ],
        out_specs=pl.BlockSpec((bn, bm), lambda i, j: (j, i)),
        out_shape=jax.ShapeDtypeStruct((n, t), out_dtype),
        compiler_params=_params("parallel", "arbitrary"),
        name="proj_t",
    )(wt, h)


def _proj(h, w, layer, out_dtype):
    t, d = h.shape
    n = w.shape[-1]
    bm, bn = min(MM_ROWS, t), min(MM_COLS, n)
    assert t % bm == 0 and n % bn == 0
    return pl.pallas_call(
        _proj_kernel,
        grid=(t // bm, n // bn),
        in_specs=[
            pl.BlockSpec((bm, d), lambda i, j: (i, 0)),
            pl.BlockSpec((None, d, bn), lambda i, j: (layer, 0, j)),
        ],
        out_specs=pl.BlockSpec((bm, bn), lambda i, j: (i, j)),
        out_shape=jax.ShapeDtypeStruct((t, n), out_dtype),
        compiler_params=_params("parallel", "arbitrary"),
        name="proj",
    )(h, w)


def _cumsum_kernel(z_ref, b_ref, o_ref):
    s = z_ref.shape[0]
    blk = 128
    r = lax.broadcasted_iota(jnp.int32, (blk, blk), 0)
    c = lax.broadcasted_iota(jnp.int32, (blk, blk), 1)
    tril = jnp.where(r >= c, 1.0, 0.0).astype(F32)
    carry = jnp.zeros((1, z_ref.shape[1]), F32)
    for i in range(s // blk):
        z = z_ref[i * blk:(i + 1) * blk, :] + b_ref[...]
        log_f = jnp.minimum(z, 0.0) - jnp.log1p(jnp.exp(-jnp.abs(z)))
        cs = jnp.dot(tril, log_f, precision=lax.Precision.HIGHEST,
                     preferred_element_type=F32) + carry
        o_ref[i * blk:(i + 1) * blk, :] = cs
        carry = cs[blk - 1:blk, :]


def _forget_cumsum(z, bias):
    b, s, w = z.shape
    return pl.pallas_call(
        _cumsum_kernel,
        grid=(b,),
        in_specs=[
            pl.BlockSpec((None, s, w), lambda i: (i, 0, 0)),
            pl.BlockSpec((1, w), lambda i: (0, 0)),
        ],
        out_specs=pl.BlockSpec((None, s, w), lambda i: (i, 0, 0)),
        out_shape=jax.ShapeDtypeStruct((b, s, w), F32),
        compiler_params=_params("parallel"),
        name="forget_cumsum",
    )(z, bias)


LOG2E = float(np.log2(np.e))
BIAS_TERMS = 3
AUG_ROWS = 16


def _split_bf16(x):
    hi = x.astype(BF16).astype(F32)
    r = x - hi
    mid = r.astype(BF16).astype(F32)
    lo = (r - mid).astype(BF16).astype(F32)
    return hi, mid, lo


def _bias_rows(c, idx, sign):
    hi, mid, lo = _split_bf16(c)
    first = 0 if sign > 0 else BIAS_TERMS
    ones_first = BIAS_TERMS if sign > 0 else 0
    out = jnp.where((idx >= ones_first) & (idx < ones_first + BIAS_TERMS), 1.0, 0.0)
    out = jnp.where(idx == first, sign * hi, out)
    out = jnp.where(idx == first + 1, sign * mid, out)
    out = jnp.where(idx == first + 2, sign * lo, out)
    return out.astype(BF16)


def _fox_kernel(qt_ref, k_ref, vt_ref, c_ref, g_ref, o_ref, kaug_sc, qaug_sc, m_sc, l_sc, acc_sc,
                *, blk):
    qi = pl.program_id(2)
    s_len = k_ref.shape[0]
    d = HEAD_DIM

    @pl.when(qi == 0)
    def _():
        kaug_sc[:, 0:d] = k_ref[...]
        lane = lax.broadcasted_iota(jnp.int32, (d, d), 1)
        for i in range(s_len // d):
            c_row = c_ref[:, i * d:(i + 1) * d] * LOG2E
            c_col = jnp.transpose(jnp.broadcast_to(c_row, (d, d)))
            kaug_sc[i * d:(i + 1) * d, d:2 * d] = _bias_rows(c_col, lane, -1.0)

    q0 = pl.multiple_of(qi * blk, blk)
    cq = jnp.broadcast_to(c_ref[:, pl.ds(q0, blk)] * LOG2E, (AUG_ROWS, blk))
    sub = lax.broadcasted_iota(jnp.int32, (AUG_ROWS, blk), 0)
    qaug_sc[0:d, :] = qt_ref[...]
    qaug_sc[d:d + AUG_ROWS, :] = _bias_rows(cq, sub, 1.0)
    qaug_sc[d + AUG_ROWS:, :] = jnp.zeros((d - AUG_ROWS, blk), BF16)

    m_sc[...] = jnp.full_like(m_sc, -jnp.inf)
    l_sc[...] = jnp.zeros_like(l_sc)
    acc_sc[...] = jnp.zeros_like(acc_sc)

    def step(ki, diagonal):
        k0 = pl.multiple_of(ki * blk, blk)
        s = _dot(kaug_sc[pl.ds(k0, blk), :], qaug_sc[...])
        if diagonal:
            r = lax.broadcasted_iota(jnp.int32, (blk, blk), 0)
            c = lax.broadcasted_iota(jnp.int32, (blk, blk), 1)
            s = jnp.where(r <= c, s, -jnp.inf)
        m_prev = m_sc[...]
        m_new = jnp.maximum(m_prev, jnp.max(s, axis=0, keepdims=True))
        alpha = jnp.exp2(m_prev - m_new)
        p = jnp.exp2(s - m_new)
        l_sc[...] = alpha * l_sc[...] + jnp.sum(p, axis=0, keepdims=True)
        acc_sc[...] = alpha * acc_sc[...] + _dot(vt_ref[:, pl.ds(k0, blk)], p.astype(BF16))
        m_sc[...] = m_new

    def body(ki, carry):
        step(ki, False)
        return carry

    lax.fori_loop(0, qi, body, 0)
    step(qi, True)
    y = jnp.transpose(acc_sc[...] / l_sc[...])
    o_ref[...] = _rms(y, g_ref[...]).astype(o_ref.dtype)


def _fox(qvt, k, c_row, gain, heads):
    b, s, _ = k.shape
    blk = min(FOX_BLOCK, s)
    nq = s // blk
    assert s % blk == 0
    return pl.pallas_call(
        functools.partial(_fox_kernel, blk=blk),
        grid=(b, heads, nq),
        in_specs=[
            pl.BlockSpec((HEAD_DIM, blk), lambda bi, h, qi: (h, bi * nq + qi)),
            pl.BlockSpec((None, s, HEAD_DIM), lambda bi, h, qi: (bi, 0, h)),
            pl.BlockSpec((HEAD_DIM, s), lambda bi, h, qi: (heads + h, bi)),
            pl.BlockSpec((None, None, 1, s), lambda bi, h, qi: (bi, h, 0, 0)),
            pl.BlockSpec((1, HEAD_DIM), lambda bi, h, qi: (0, h)),
        ],
        out_specs=pl.BlockSpec((None, blk, HEAD_DIM), lambda bi, h, qi: (bi, qi, h)),
        out_shape=jax.ShapeDtypeStruct((b, s, heads * HEAD_DIM), BF16),
        scratch_shapes=[
            pltpu.VMEM((s, 2 * HEAD_DIM), BF16),
            pltpu.VMEM((2 * HEAD_DIM, blk), BF16),
            pltpu.VMEM((1, blk), F32),
            pltpu.VMEM((1, blk), F32),
            pltpu.VMEM((HEAD_DIM, blk), F32),
        ],
        compiler_params=_params("parallel", "parallel", "arbitrary"),
        name="fox",
    )(qvt, k, qvt, c_row, gain)


def _ret_kernel(lg_ref, q_ref, k_ref, v_ref, gate_ref, cos_ref, sin_ref, gain_ref, o_ref,
                state_sc, *, chunk):
    h = pl.program_id(1)
    n = pl.program_id(2)
    lg = lg_ref[h]

    @pl.when(n == 0)
    def _():
        state_sc[...] = jnp.zeros_like(state_sc)

    cos = cos_ref[...]
    sin = sin_ref[...]

    def rope(x):
        return x * cos + pltpu.roll(x, HEAD_DIM // 2, 1) * sin

    q = rope(q_ref[...])
    k = rope(k_ref[...]) * (HEAD_DIM ** -0.5)
    v = v_ref[...].astype(BF16)
    t = lax.broadcasted_iota(jnp.int32, (chunk, 1), 0).astype(F32)
    xi = jnp.exp((t + 1.0) * lg)
    zeta = jnp.exp((chunk - 1.0 - t) * lg)
    r = lax.broadcasted_iota(jnp.int32, (chunk, chunk), 0)
    c = lax.broadcasted_iota(jnp.int32, (chunk, chunk), 1)
    diff = (r - c).astype(F32)
    decay = jnp.where(diff >= 0.0, jnp.exp(jnp.maximum(diff, 0.0) * lg), 0.0)
    qb = q.astype(BF16)
    scores = _dot_nt(qb, k.astype(BF16)) * decay
    state = state_sc[...]
    o = _dot(scores.astype(BF16), v) + _dot(qb, state.astype(BF16)) * xi
    kz_t = jnp.transpose(k * zeta).astype(BF16)
    state_sc[...] = state * jnp.exp(chunk * lg) + _dot(kz_t, v)
    gate = gate_ref[...]
    y = _rms(o, gain_ref[...]) * (gate * jax.nn.sigmoid(gate))
    o_ref[...] = y.astype(o_ref.dtype)


def _retention(log_g, qkvg, cos2, sin2, gain, heads):
    b, s, _ = qkvg.shape
    chunk = min(RET_CHUNK, s)
    assert s % chunk == 0

    def col(off):
        return pl.BlockSpec((None, chunk, HEAD_DIM), lambda bi, h, n, lg: (bi, n, off + h))

    grid_spec = pltpu.PrefetchScalarGridSpec(
        num_scalar_prefetch=1,
        grid=(b, heads, s // chunk),
        in_specs=[
            col(0), col(heads), col(2 * heads), col(3 * heads),
            pl.BlockSpec((chunk, HEAD_DIM), lambda bi, h, n, lg: (n, 0)),
            pl.BlockSpec((chunk, HEAD_DIM), lambda bi, h, n, lg: (n, 0)),
            pl.BlockSpec((1, HEAD_DIM), lambda bi, h, n, lg: (0, h)),
        ],
        out_specs=pl.BlockSpec((None, chunk, HEAD_DIM), lambda bi, h, n, lg: (bi, n, h)),
        scratch_shapes=[pltpu.VMEM((HEAD_DIM, HEAD_DIM), F32)],
    )
    return pl.pallas_call(
        functools.partial(_ret_kernel, chunk=chunk),
        grid_spec=grid_spec,
        out_shape=jax.ShapeDtypeStruct((b, s, heads * HEAD_DIM), BF16),
        compiler_params=_params("parallel", "parallel", "arbitrary"),
        name="retention",
    )(log_g, qkvg, qkvg, qkvg, qkvg, cos2, sin2, gain)


def _gelu(x):
    return 0.5 * x * (1.0 + jnp.tanh(np.sqrt(2.0 / np.pi).astype(np.float32) * (x + 0.044715 * (x * x * x))))


def _gmlp_kernel(uv_ref, lng_ref, lnb_ref, ws_ref, bs_ref, gain_ref, o_ref, *, groups, rows):
    w = groups * HEAD_DIM
    blk = CHUNK_GMLP
    r = lax.broadcasted_iota(jnp.int32, (blk, blk), 0)
    c = lax.broadcasted_iota(jnp.int32, (blk, blk), 1)
    for g in range(groups):
        lo, hi = g * HEAD_DIM, (g + 1) * HEAD_DIM
        wm = jnp.where(r >= c, ws_ref[g], 0.0).astype(BF16)
        bias = bs_ref[g]
        for ci in range(rows // blk):
            rs = slice(ci * blk, (ci + 1) * blk)
            u = _gelu(uv_ref[rs, lo:hi])
            v = _gelu(uv_ref[rs, w + lo:w + hi])
            mu = jnp.mean(v, axis=-1, keepdims=True)
            var = jnp.mean(jnp.square(v - mu), axis=-1, keepdims=True)
            v = (v - mu) * lax.rsqrt(var + EPS) * lng_ref[:, lo:hi] + lnb_ref[:, lo:hi]
            y = u * (_dot(wm, v.astype(BF16)) + bias)
            o_ref[rs, lo:hi] = _rms(y, gain_ref[:, lo:hi]).astype(o_ref.dtype)


def _gmlp(uv, ln_g, ln_b, w_s, b_s, gain, groups):
    t, _ = uv.shape
    w = groups * HEAD_DIM
    rows = min(GMLP_ROWS, t)
    assert t % rows == 0 and rows % CHUNK_GMLP == 0
    return pl.pallas_call(
        functools.partial(_gmlp_kernel, groups=groups, rows=rows),
        grid=(t // rows,),
        in_specs=[
            pl.BlockSpec((rows, 2 * w), lambda i: (i, 0)),
            pl.BlockSpec((1, w), lambda i: (0, 0)),
            pl.BlockSpec((1, w), lambda i: (0, 0)),
            pl.BlockSpec((groups, CHUNK_GMLP, CHUNK_GMLP), lambda i: (0, 0, 0)),
            pl.BlockSpec((groups, CHUNK_GMLP, 1), lambda i: (0, 0, 0)),
            pl.BlockSpec((1, w), lambda i: (0, 0)),
        ],
        out_specs=pl.BlockSpec((rows, w), lambda i: (i, 0)),
        out_shape=jax.ShapeDtypeStruct((t, w), BF16),
        compiler_params=_params("parallel"),
        name="gmlp",
    )(uv, ln_g, ln_b, w_s, b_s, gain)


def _outproj_kernel(x_ref, ya_ref, yb_ref, yc_ref, w_ref, o_ref):
    wa, wb = ya_ref.shape[1], yb_ref.shape[1]
    acc = _dot(ya_ref[...], w_ref[0:wa, :])
    acc += _dot(yb_ref[...], w_ref[wa:wa + wb, :])
    acc += _dot(yc_ref[...], w_ref[wa + wb:, :])
    o_ref[...] = x_ref[...] + acc


def _outproj(x, ya, yb, yc, w, layer):
    t, d = x.shape
    k = w.shape[-2]
    bm, bn = min(MM_ROWS, t), min(MM_COLS, d)
    assert t % bm == 0 and d % bn == 0
    return pl.pallas_call(
        _outproj_kernel,
        grid=(t // bm, d // bn),
        in_specs=[
            pl.BlockSpec((bm, bn), lambda i, j: (i, j)),
            pl.BlockSpec((bm, ya.shape[1]), lambda i, j: (i, 0)),
            pl.BlockSpec((bm, yb.shape[1]), lambda i, j: (i, 0)),
            pl.BlockSpec((bm, yc.shape[1]), lambda i, j: (i, 0)),
            pl.BlockSpec((None, k, bn), lambda i, j: (layer, 0, j)),
        ],
        out_specs=pl.BlockSpec((bm, bn), lambda i, j: (i, j)),
        out_shape=jax.ShapeDtypeStruct((t, d), F32),
        compiler_params=_params("parallel", "arbitrary"),
        name="outproj",
    )(x, ya, yb, yc, w)


def kernel(x, ffn1_norm, ffn1_w_gate, ffn1_w_up, ffn1_w_down, mix_norm, w_in, fox_b_f,
           gmlp_ln_g, gmlp_ln_b, gmlp_w_s, gmlp_b_s, out_norm, w_out, ffn2_norm,
           ffn2_w_gate, ffn2_w_up, ffn2_w_down, final_norm):
    b, s, d = x.shape
    depth = w_in.shape[0]
    t = b * s
    n_heads = d // HEAD_DIM
    fox_h, ret_h = n_heads // 2, n_heads // 4
    gm_g = n_heads - fox_h - ret_h
    fox_w, ret_w, gm_w = fox_h * HEAD_DIM, ret_h * HEAD_DIM, gm_g * HEAD_DIM

    o_fz = 3 * fox_w
    o_ret = o_fz + fox_h
    o_gm = o_ret + 4 * ret_w
    w_qv_t = jnp.transpose(
        jnp.concatenate([w_in[:, :, :fox_w], w_in[:, :, 2 * fox_w:o_fz]], axis=2), (0, 2, 1)
    ).astype(BF16)
    w_k = w_in[:, :, fox_w:2 * fox_w].astype(BF16)
    w_fz = jnp.pad(w_in[:, :, o_fz:o_ret], ((0, 0), (0, 0), (0, HEAD_DIM - fox_h))).astype(BF16)
    w_ret = w_in[:, :, o_ret:o_gm].astype(BF16)
    w_gm = w_in[:, :, o_gm:].astype(BF16)
    w_o = w_out.astype(BF16)
    f1 = [w.astype(BF16) for w in (ffn1_w_gate, ffn1_w_up, ffn1_w_down)]
    f2 = [w.astype(BF16) for w in (ffn2_w_gate, ffn2_w_up, ffn2_w_down)]
    fz_bias = jnp.pad(fox_b_f, ((0, 0), (0, HEAD_DIM - fox_h)))

    half = HEAD_DIM // 2
    pos = jnp.arange(s, dtype=F32)
    inv_freq = ROPE_BASE ** (-jnp.arange(half, dtype=F32) / half)
    ang = pos[:, None] * inv_freq[None, :]
    cos2 = jnp.concatenate([jnp.cos(ang), jnp.cos(ang)], axis=-1)
    sin2 = jnp.concatenate([-jnp.sin(ang), jnp.sin(ang)], axis=-1)
    log_g = jnp.log1p(-jnp.exp2(-(RET_GAMMA_BASE + jnp.arange(ret_h, dtype=F32))))

    xf = x.reshape(t, d)
    for l in range(depth):
        x1, h = _ffn(xf, ffn1_norm[l][None], *f1, mix_norm[l][None], l,
                     emit_x=True, emit_n=True, n_dtype=BF16)
        qvt = _proj_t(h, w_qv_t, l, BF16, scale=HEAD_DIM ** -0.5 * LOG2E, scaled_rows=fox_w)
        kf = _proj(h, w_k, l, BF16)
        fz = _proj(h, w_fz, l, F32)
        rp = _proj(h, w_ret, l, F32)
        gp = _proj(h, w_gm, l, F32)

        c = _forget_cumsum(fz.reshape(b, s, HEAD_DIM), fz_bias[l][None])
        c = jnp.transpose(c[:, :, :fox_h], (0, 2, 1))
        gains = out_norm[l][None]
        ya = _fox(qvt, kf.reshape(b, s, fox_w), c[:, :, None, :], gains[:, :fox_w], fox_h)
        yb = _retention(log_g, rp.reshape(b, s, 4 * ret_w), cos2, sin2,
                        gains[:, fox_w:fox_w + ret_w], ret_h)
        yc = _gmlp(gp, gmlp_ln_g[l][None], gmlp_ln_b[l][None], gmlp_w_s[l],
                   gmlp_b_s[l][..., None], gains[:, fox_w + ret_w:], gm_g)
        x2 = _outproj(x1, ya.reshape(t, fox_w), yb.reshape(t, ret_w), yc, w_o, l)

        last = l == depth - 1
        (xf,) = _ffn(x2, ffn2_norm[l][None], *f2, final_norm[None], l,
                     emit_x=not last, emit_n=last, n_dtype=F32)
    return xf.reshape(b, s, d)
```

```python
import functools

import jax
import jax.numpy as jnp
import numpy as np
from jax import lax
from jax.experimental import pallas as pl
from jax.experimental.pallas import tpu as pltpu

F32 = jnp.float32
BF16 = jnp.bfloat16

HEAD_DIM = 128
CHUNK_GMLP = 128
ROPE_BASE = 10000.0
RET_GAMMA_BASE = 5.0
EPS = 1e-6

V7X_VMEM_BYTES = 64 * 1024 * 1024
VMEM_LIMIT = V7X_VMEM_BYTES - 1 * 1024 * 1024

FFN_ROWS = 1024
FFN_COLS = 512
MM_ROWS = 1024
MM_COLS = 1024
FOX_Q_ROWS = 1024
RET_CHUNK = 512
GMLP_ROWS = 512


def _params(*sem):
    return pltpu.CompilerParams(dimension_semantics=sem, vmem_limit_bytes=VMEM_LIMIT)


def _rms(x, gain):
    return x * lax.rsqrt(jnp.mean(x * x, axis=-1, keepdims=True) + EPS) * gain


def _dot(a, b):
    return jnp.dot(a, b, preferred_element_type=F32)


def _dot_nt(a, b):
    return lax.dot_general(a, b, (((1,), (1,)), ((), ())), preferred_element_type=F32)


def _ffn_kernel(x_ref, g_ref, wg_ref, wu_ref, wd_ref, g2_ref, *rest, emit_h, norm_out):
    acc_ref = rest[0]
    xn_sc = rest[-1]
    j = pl.program_id(1)

    @pl.when(j == 0)
    def _():
        xn_sc[...] = _rms(x_ref[...], g_ref[...]).astype(BF16)
        acc_ref[...] = jnp.zeros_like(acc_ref)

    xn = xn_sc[...]
    a = _dot(xn, wg_ref[...])
    b = _dot(xn, wu_ref[...])
    hmid = (a * jax.nn.sigmoid(a) * b).astype(BF16)
    acc_ref[...] += _dot(hmid, wd_ref[...])

    @pl.when(j == pl.num_programs(1) - 1)
    def _():
        y = x_ref[...] + 0.5 * acc_ref[...]
        acc_ref[...] = _rms(y, g2_ref[...]) if norm_out else y
        if emit_h:
            rest[1][...] = _rms(y, g2_ref[...]).astype(BF16)


def _ffn(x, gain, wg, wu, wd, gain2, layer, *, emit_h=False, norm_out=False):
    t, d = x.shape
    f = wg.shape[-1]
    bm, tf = min(FFN_ROWS, t), FFN_COLS
    assert t % bm == 0 and f % tf == 0
    out_shape = [jax.ShapeDtypeStruct((t, d), F32)]
    if emit_h:
        out_shape.append(jax.ShapeDtypeStruct((t, d), BF16))
    out_specs = [pl.BlockSpec((bm, d), lambda i, j: (i, 0)) for _ in out_shape]
    return pl.pallas_call(
        functools.partial(_ffn_kernel, emit_h=emit_h, norm_out=norm_out),
        grid=(t // bm, f // tf),
        in_specs=[
            pl.BlockSpec((bm, d), lambda i, j: (i, 0), pipeline_mode=pl.Buffered(1)),
            pl.BlockSpec((1, d), lambda i, j: (0, 0)),
            pl.BlockSpec((None, d, tf), lambda i, j: (layer, 0, j)),
            pl.BlockSpec((None, d, tf), lambda i, j: (layer, 0, j)),
            pl.BlockSpec((None, tf, d), lambda i, j: (layer, j, 0)),
            pl.BlockSpec((1, d), lambda i, j: (0, 0)),
        ],
        out_specs=out_specs,
        out_shape=out_shape,
        scratch_shapes=[pltpu.VMEM((bm, d), BF16)],
        compiler_params=_params("parallel", "arbitrary"),
        name="ffn",
    )(x, gain, wg, wu, wd, gain2)


def _proj_t_kernel(wt_ref, h_ref, o_ref, *, scale, scaled_rows, bn):
    y = _dot_nt(wt_ref[...], h_ref[...])
    if scaled_rows:
        j = pl.program_id(1)
        y = y * jnp.where(j * bn < scaled_rows, scale, 1.0).astype(F32)
    o_ref[...] = y.astype(o_ref.dtype)


def _proj_t(h, wt, layer, out_dtype, *, scale=1.0, scaled_rows=0):
    t, d = h.shape
    n = wt.shape[-2]
    bm, bn = min(MM_ROWS, t), min(MM_COLS, n)
    assert t % bm == 0 and n % bn == 0 and scaled_rows % bn == 0
    return pl.pallas_call(
        functools.partial(_proj_t_kernel, scale=scale, scaled_rows=scaled_rows, bn=bn),
        grid=(t // bm, n // bn),
        in_specs=[
            pl.BlockSpec((None, bn, d), lambda i, j: (layer, j, 0)),
            pl.BlockSpec((bm, d), lambda i, j: (i, 0)),
        ],
        out_specs=pl.BlockSpec((bn, bm), lambda i, j: (j, i)),
        out_shape=jax.ShapeDtypeStruct((n, t), out_dtype),
        compiler_params=_params("parallel", "arbitrary"),
        name="proj_t",
    )(wt, h)


def _proj_kernel(h_ref, w_ref, k_ref, z_ref):
    y = _dot(h_ref[...], w_ref[...])
    nk = k_ref.shape[1]
    k_ref[...] = y[:, :nk].astype(k_ref.dtype)
    z_ref[...] = y[:, nk:]


def _proj_kz(h, w, layer, nk):
    t, d = h.shape
    n = w.shape[-1]
    bm = min(MM_ROWS, t)
    assert t % bm == 0
    return pl.pallas_call(
        _proj_kernel,
        grid=(t // bm,),
        in_specs=[
            pl.BlockSpec((bm, d), lambda i: (i, 0)),
            pl.BlockSpec((None, d, n), lambda i: (layer, 0, 0)),
        ],
        out_specs=[
            pl.BlockSpec((bm, nk), lambda i: (i, 0)),
            pl.BlockSpec((bm, n - nk), lambda i: (i, 0)),
        ],
        out_shape=[
            jax.ShapeDtypeStruct((t, nk), BF16),
            jax.ShapeDtypeStruct((t, n - nk), F32),
        ],
        compiler_params=_params("parallel"),
        name="proj_kz",
    )(h, w)


def _cumsum_kernel(z_ref, b_ref, o_ref):
    s = z_ref.shape[0]
    blk = 128
    r = lax.broadcasted_iota(jnp.int32, (blk, blk), 0)
    c = lax.broadcasted_iota(jnp.int32, (blk, blk), 1)
    tril = jnp.where(r >= c, 1.0, 0.0).astype(F32)
    carry = jnp.zeros((1, z_ref.shape[1]), F32)
    for i in range(s // blk):
        z = z_ref[i * blk:(i + 1) * blk, :] + b_ref[...]
        log_f = jnp.minimum(z, 0.0) - jnp.log1p(jnp.exp(-jnp.abs(z)))
        cs = jnp.dot(tril, log_f, precision=lax.Precision.HIGHEST,
                     preferred_element_type=F32) + carry
        o_ref[i * blk:(i + 1) * blk, :] = cs
        carry = cs[blk - 1:blk, :]


def _forget_cumsum(z, bias):
    b, s, w = z.shape
    return pl.pallas_call(
        _cumsum_kernel,
        grid=(b,),
        in_specs=[
            pl.BlockSpec((None, s, w), lambda i: (i, 0, 0)),
            pl.BlockSpec((1, w), lambda i: (0, 0)),
        ],
        out_specs=pl.BlockSpec((None, s, w), lambda i: (i, 0, 0)),
        out_shape=jax.ShapeDtypeStruct((b, s, w), F32),
        compiler_params=_params("parallel"),
        name="forget_cumsum",
    )(z, bias)


LOG2E = float(np.log2(np.e))
BIAS_TERMS = 3
AUG_ROWS = 16


def _split_bf16(x):
    hi = x.astype(BF16).astype(F32)
    r = x - hi
    mid = r.astype(BF16).astype(F32)
    lo = (r - mid).astype(BF16).astype(F32)
    return hi, mid, lo


def _bias_rows(c, idx, sign):
    hi, mid, lo = _split_bf16(c)
    first = 0 if sign > 0 else BIAS_TERMS
    ones_first = BIAS_TERMS if sign > 0 else 0
    out = jnp.where((idx >= ones_first) & (idx < ones_first + BIAS_TERMS), 1.0, 0.0)
    out = jnp.where(idx == first, sign * hi, out)
    out = jnp.where(idx == first + 1, sign * mid, out)
    out = jnp.where(idx == first + 2, sign * lo, out)
    return out.astype(BF16)


def _fox_kernel(qt_ref, k_ref, vt_ref, c_ref, g_ref, o_ref, kaug_sc, qaug_sc, m_sc, l_sc, acc_sc,
                *, bq, bk):
    qi = pl.program_id(2)
    s_len = k_ref.shape[0]
    d = HEAD_DIM
    blk = bq

    @pl.when(qi == 0)
    def _():
        kaug_sc[:, 0:d] = k_ref[...]
        lane = lax.broadcasted_iota(jnp.int32, (d, d), 1)
        for i in range(s_len // d):
            c_row = c_ref[:, i * d:(i + 1) * d] * LOG2E
            c_col = jnp.transpose(jnp.broadcast_to(c_row, (d, d)))
            kaug_sc[i * d:(i + 1) * d, d:2 * d] = _bias_rows(c_col, lane, -1.0)

    q0 = pl.multiple_of(qi * blk, blk)
    cq = jnp.broadcast_to(c_ref[:, pl.ds(q0, blk)] * LOG2E, (AUG_ROWS, blk))
    sub = lax.broadcasted_iota(jnp.int32, (AUG_ROWS, blk), 0)
    qaug_sc[0:d, :] = qt_ref[...]
    qaug_sc[d:d + AUG_ROWS, :] = _bias_rows(cq, sub, 1.0)
    qaug_sc[d + AUG_ROWS:, :] = jnp.zeros((d - AUG_ROWS, blk), BF16)

    m_sc[...] = jnp.full_like(m_sc, -jnp.inf)
    l_sc[...] = jnp.zeros_like(l_sc)
    acc_sc[...] = jnp.zeros_like(acc_sc)

    def step(ki, lo, width, masked):
        cols = slice(lo, lo + width)
        k0 = pl.multiple_of(ki * bk, bk)
        s = _dot(kaug_sc[pl.ds(k0, bk), :], qaug_sc[:, cols])
        if masked:
            r = lax.broadcasted_iota(jnp.int32, (bk, width), 0)
            c = lax.broadcasted_iota(jnp.int32, (bk, width), 1)
            s = jnp.where(r - c <= q0 + lo - k0, s, -jnp.inf)
        m_prev = m_sc[:, cols]
        m_new = jnp.maximum(m_prev, jnp.max(s, axis=0, keepdims=True))
        alpha = jnp.exp2(m_prev - m_new)
        p = jnp.exp2(s - m_new)
        l_sc[:, cols] = alpha * l_sc[:, cols] + jnp.sum(p, axis=0, keepdims=True)
        acc_sc[:, cols] = alpha * acc_sc[:, cols] + _dot(vt_ref[:, pl.ds(k0, bk)], p.astype(BF16))
        m_sc[:, cols] = m_new

    def body(ki, carry):
        step(ki, 0, bq, False)
        return carry

    lax.fori_loop(0, 2 * qi, body, 0)
    step(2 * qi, 0, bq, True)
    step(2 * qi + 1, bk, bk, True)
    y = jnp.transpose(acc_sc[...] / l_sc[...])
    o_ref[...] = _rms(y, g_ref[...]).astype(o_ref.dtype)


def _fox(qvt, k, c_row, gain, heads):
    b, s, _ = k.shape
    blk = min(FOX_Q_ROWS, s)
    nq = s // blk
    assert s % blk == 0 and blk % 2 == 0
    return pl.pallas_call(
        functools.partial(_fox_kernel, bq=blk, bk=blk // 2),
        grid=(b, heads, nq),
        in_specs=[
            pl.BlockSpec((HEAD_DIM, blk), lambda bi, h, qi: (h, bi * nq + qi)),
            pl.BlockSpec((None, s, HEAD_DIM), lambda bi, h, qi: (bi, 0, h)),
            pl.BlockSpec((HEAD_DIM, s), lambda bi, h, qi: (heads + h, bi)),
            pl.BlockSpec((None, None, 1, s), lambda bi, h, qi: (bi, h, 0, 0)),
            pl.BlockSpec((1, HEAD_DIM), lambda bi, h, qi: (0, h)),
        ],
        out_specs=pl.BlockSpec((None, blk, HEAD_DIM), lambda bi, h, qi: (bi, qi, h)),
        out_shape=jax.ShapeDtypeStruct((b, s, heads * HEAD_DIM), BF16),
        scratch_shapes=[
            pltpu.VMEM((s, 2 * HEAD_DIM), BF16),
            pltpu.VMEM((2 * HEAD_DIM, blk), BF16),
            pltpu.VMEM((1, blk), F32),
            pltpu.VMEM((1, blk), F32),
            pltpu.VMEM((HEAD_DIM, blk), F32),
        ],
        compiler_params=_params("parallel", "parallel", "arbitrary"),
        name="fox",
    )(qvt, k, qvt, c_row, gain)


def _ret_kernel(h_ref, w_ref, cos_ref, sin_ref, gain_ref, o_ref, p_sc, state_sc, *, chunk, heads):
    n = pl.program_id(1)
    d = HEAD_DIM
    w = heads * d

    @pl.when(n == 0)
    def _():
        state_sc[...] = jnp.zeros_like(state_sc)

    p_sc[...] = _dot(h_ref[...], w_ref[...])
    cos = cos_ref[...]
    sin = sin_ref[...]

    def rope(x):
        return x * cos + pltpu.roll(x, d // 2, 1) * sin

    t = lax.broadcasted_iota(jnp.int32, (chunk, 1), 0).astype(F32)
    r = lax.broadcasted_iota(jnp.int32, (chunk, chunk), 0)
    c = lax.broadcasted_iota(jnp.int32, (chunk, chunk), 1)
    diff = (r - c).astype(F32)
    for hh in range(heads):
        lg = float(np.log1p(-np.exp2(-(RET_GAMMA_BASE + hh))))
        lo, hi = hh * d, (hh + 1) * d
        q = rope(p_sc[:, lo:hi])
        k = rope(p_sc[:, w + lo:w + hi]) * (d ** -0.5)
        v = p_sc[:, 2 * w + lo:2 * w + hi].astype(BF16)
        gate = p_sc[:, 3 * w + lo:3 * w + hi]
        xi = jnp.exp((t + 1.0) * lg)
        zeta = jnp.exp((chunk - 1.0 - t) * lg)
        decay = jnp.where(diff >= 0.0, jnp.exp(jnp.maximum(diff, 0.0) * lg), 0.0)
        qb = q.astype(BF16)
        scores = _dot_nt(qb, k.astype(BF16)) * decay
        state = state_sc[hh]
        o = _dot(scores.astype(BF16), v) + _dot(qb, state.astype(BF16)) * xi
        kz_t = jnp.transpose(k * zeta).astype(BF16)
        state_sc[hh] = state * float(np.exp(chunk * lg)) + _dot(kz_t, v)
        y = _rms(o, gain_ref[:, lo:hi]) * (gate * jax.nn.sigmoid(gate))
        o_ref[:, lo:hi] = y.astype(o_ref.dtype)


def _retention(h, w_ret, layer, cos2, sin2, gain, heads, b, s):
    t, dm = h.shape
    chunk = min(RET_CHUNK, s)
    nc = s // chunk
    w = heads * HEAD_DIM
    assert s % chunk == 0 and t == b * s
    return pl.pallas_call(
        functools.partial(_ret_kernel, chunk=chunk, heads=heads),
        grid=(b, nc),
        in_specs=[
            pl.BlockSpec((chunk, dm), lambda bi, n: (bi * nc + n, 0)),
            pl.BlockSpec((None, dm, 4 * w), lambda bi, n: (layer, 0, 0)),
            pl.BlockSpec((chunk, HEAD_DIM), lambda bi, n: (n, 0)),
            pl.BlockSpec((chunk, HEAD_DIM), lambda bi, n: (n, 0)),
            pl.BlockSpec((1, w), lambda bi, n: (0, 0)),
        ],
        out_specs=pl.BlockSpec((chunk, w), lambda bi, n: (bi * nc + n, 0)),
        out_shape=jax.ShapeDtypeStruct((t, w), BF16),
        scratch_shapes=[
            pltpu.VMEM((chunk, 4 * w), F32),
            pltpu.VMEM((heads, HEAD_DIM, HEAD_DIM), F32),
        ],
        compiler_params=_params("parallel", "arbitrary"),
        name="retention",
    )(h, w_ret, cos2, sin2, gain)


def _gelu(x):
    return 0.5 * x * (1.0 + jnp.tanh(np.sqrt(2.0 / np.pi).astype(np.float32) * (x + 0.044715 * (x * x * x))))


def _gmlp_kernel(h_ref, w_ref, lng_ref, lnb_ref, ws_ref, bs_ref, gain_ref, o_ref, uv_ref,
                 *, groups, rows):
    w = groups * HEAD_DIM
    blk = CHUNK_GMLP
    uv_ref[...] = _dot(h_ref[...], w_ref[...])
    r = lax.broadcasted_iota(jnp.int32, (blk, blk), 0)
    c = lax.broadcasted_iota(jnp.int32, (blk, blk), 1)
    for g in range(groups):
        lo, hi = g * HEAD_DIM, (g + 1) * HEAD_DIM
        wm = jnp.where(r >= c, ws_ref[g], 0.0).astype(BF16)
        bias = bs_ref[g]
        for ci in range(rows // blk):
            rs = slice(ci * blk, (ci + 1) * blk)
            u = _gelu(uv_ref[rs, lo:hi])
            v = _gelu(uv_ref[rs, w + lo:w + hi])
            mu = jnp.mean(v, axis=-1, keepdims=True)
            var = jnp.mean(jnp.square(v - mu), axis=-1, keepdims=True)
            v = (v - mu) * lax.rsqrt(var + EPS) * lng_ref[:, lo:hi] + lnb_ref[:, lo:hi]
            y = u * (_dot(wm, v.astype(BF16)) + bias)
            o_ref[rs, lo:hi] = _rms(y, gain_ref[:, lo:hi]).astype(o_ref.dtype)


def _gmlp(h, w_gm, layer, ln_g, ln_b, w_s, b_s, gain, groups):
    t, dm = h.shape
    w = groups * HEAD_DIM
    rows = min(GMLP_ROWS, t)
    assert t % rows == 0 and rows % CHUNK_GMLP == 0
    return pl.pallas_call(
        functools.partial(_gmlp_kernel, groups=groups, rows=rows),
        grid=(t // rows,),
        in_specs=[
            pl.BlockSpec((rows, dm), lambda i: (i, 0)),
            pl.BlockSpec((None, dm, 2 * w), lambda i: (layer, 0, 0)),
            pl.BlockSpec((1, w), lambda i: (0, 0)),
            pl.BlockSpec((1, w), lambda i: (0, 0)),
            pl.BlockSpec((groups, CHUNK_GMLP, CHUNK_GMLP), lambda i: (0, 0, 0)),
            pl.BlockSpec((groups, CHUNK_GMLP, 1), lambda i: (0, 0, 0)),
            pl.BlockSpec((1, w), lambda i: (0, 0)),
        ],
        out_specs=pl.BlockSpec((rows, w), lambda i: (i, 0)),
        out_shape=jax.ShapeDtypeStruct((t, w), BF16),
        scratch_shapes=[pltpu.VMEM((rows, 2 * w), F32)],
        compiler_params=_params("parallel"),
        name="gmlp",
    )(h, w_gm, ln_g, ln_b, w_s, b_s, gain)


def _outproj_kernel(x_ref, ya_ref, yb_ref, yc_ref, w_ref, o_ref):
    wa, wb = ya_ref.shape[1], yb_ref.shape[1]
    acc = _dot(ya_ref[...], w_ref[0:wa, :])
    acc += _dot(yb_ref[...], w_ref[wa:wa + wb, :])
    acc += _dot(yc_ref[...], w_ref[wa + wb:, :])
    o_ref[...] = x_ref[...] + acc


def _outproj(x, ya, yb, yc, w, layer):
    t, d = x.shape
    k = w.shape[-2]
    bm, bn = min(MM_ROWS, t), min(MM_COLS, d)
    assert t % bm == 0 and d % bn == 0
    return pl.pallas_call(
        _outproj_kernel,
        grid=(t // bm, d // bn),
        in_specs=[
            pl.BlockSpec((bm, bn), lambda i, j: (i, j)),
            pl.BlockSpec((bm, ya.shape[1]), lambda i, j: (i, 0)),
            pl.BlockSpec((bm, yb.shape[1]), lambda i, j: (i, 0)),
            pl.BlockSpec((bm, yc.shape[1]), lambda i, j: (i, 0)),
            pl.BlockSpec((None, k, bn), lambda i, j: (layer, 0, j)),
        ],
        out_specs=pl.BlockSpec((bm, bn), lambda i, j: (i, j)),
        out_shape=jax.ShapeDtypeStruct((t, d), F32),
        compiler_params=_params("parallel", "arbitrary"),
        name="outproj",
    )(x, ya, yb, yc, w)


def kernel(x, ffn1_norm, ffn1_w_gate, ffn1_w_up, ffn1_w_down, mix_norm, w_in, fox_b_f,
           gmlp_ln_g, gmlp_ln_b, gmlp_w_s, gmlp_b_s, out_norm, w_out, ffn2_norm,
           ffn2_w_gate, ffn2_w_up, ffn2_w_down, final_norm):
    b, s, d = x.shape
    depth = w_in.shape[0]
    t = b * s
    n_heads = d // HEAD_DIM
    fox_h, ret_h = n_heads // 2, n_heads // 4
    gm_g = n_heads - fox_h - ret_h
    fox_w, ret_w, gm_w = fox_h * HEAD_DIM, ret_h * HEAD_DIM, gm_g * HEAD_DIM

    o_fz = 3 * fox_w
    o_ret = o_fz + fox_h
    o_gm = o_ret + 4 * ret_w
    w_qv_t = jnp.transpose(
        jnp.concatenate([w_in[:, :, :fox_w], w_in[:, :, 2 * fox_w:o_fz]], axis=2), (0, 2, 1)
    ).astype(BF16)
    w_kz = jnp.concatenate(
        [w_in[:, :, fox_w:2 * fox_w],
         jnp.pad(w_in[:, :, o_fz:o_ret], ((0, 0), (0, 0), (0, HEAD_DIM - fox_h)))], axis=2
    ).astype(BF16)
    w_ret = w_in[:, :, o_ret:o_gm].astype(BF16)
    w_gm = w_in[:, :, o_gm:].astype(BF16)
    w_o = w_out.astype(BF16)
    f1 = [w.astype(BF16) for w in (ffn1_w_gate, ffn1_w_up, ffn1_w_down)]
    f2 = [w.astype(BF16) for w in (ffn2_w_gate, ffn2_w_up, ffn2_w_down)]
    fz_bias = jnp.pad(fox_b_f, ((0, 0), (0, HEAD_DIM - fox_h)))

    half = HEAD_DIM // 2
    pos = jnp.arange(s, dtype=F32)
    inv_freq = ROPE_BASE ** (-jnp.arange(half, dtype=F32) / half)
    ang = pos[:, None] * inv_freq[None, :]
    cos2 = jnp.concatenate([jnp.cos(ang), jnp.cos(ang)], axis=-1)
    sin2 = jnp.concatenate([-jnp.sin(ang), jnp.sin(ang)], axis=-1)

    xf = x.reshape(t, d)
    for l in range(depth):
        x1, h = _ffn(xf, ffn1_norm[l][None], *f1, mix_norm[l][None], l, emit_h=True)
        qvt = _proj_t(h, w_qv_t, l, BF16, scale=HEAD_DIM ** -0.5 * LOG2E, scaled_rows=fox_w)
        kf, fz = _proj_kz(h, w_kz, l, fox_w)

        c = _forget_cumsum(fz.reshape(b, s, HEAD_DIM), fz_bias[l][None])
        c = jnp.transpose(c[:, :, :fox_h], (0, 2, 1))
        gains = out_norm[l][None]
        ya = _fox(qvt, kf.reshape(b, s, fox_w), c[:, :, None, :], gains[:, :fox_w], fox_h)
        yb = _retention(h, w_ret, l, cos2, sin2, gains[:, fox_w:fox_w + ret_w], ret_h, b, s)
        yc = _gmlp(h, w_gm, l, gmlp_ln_g[l][None], gmlp_ln_b[l][None], gmlp_w_s[l],
                   gmlp_b_s[l][..., None], gains[:, fox_w + ret_w:], gm_g)
        x2 = _outproj(x1, ya.reshape(t, fox_w), yb, yc, w_o, l)

        (xf,) = _ffn(x2, ffn2_norm[l][None], *f2, final_norm[None], l, norm_out=l == depth - 1)
    return xf.reshape(b, s, d)
```

```python
import functools

import jax
import jax.numpy as jnp
import numpy as np
from jax import lax
from jax.experimental import pallas as pl
from jax.experimental.pallas import tpu as pltpu

F32 = jnp.float32
BF16 = jnp.bfloat16

HEAD_DIM = 128
CHUNK_GMLP = 128
ROPE_BASE = 10000.0
RET_GAMMA_BASE = 5.0
EPS = 1e-6

V7X_VMEM_BYTES = 64 * 1024 * 1024
VMEM_LIMIT = V7X_VMEM_BYTES - 1 * 1024 * 1024

FFN_ROWS = 1024
FFN_ROWS_WITH_H = 512
FFN_COLS = 512
MM_ROWS = 1024
MM_COLS = 1024
FOX_Q_ROWS = 1024
FOX_HEADS_PER_STEP = 2
RET_CHUNK = 512
GMLP_ROWS = 512


def _params(*sem):
    return pltpu.CompilerParams(dimension_semantics=sem, vmem_limit_bytes=VMEM_LIMIT)


def _rms(x, gain):
    return x * lax.rsqrt(jnp.mean(x * x, axis=-1, keepdims=True) + EPS) * gain


def _dot(a, b):
    return jnp.dot(a, b, preferred_element_type=F32)


def _dot_nt(a, b):
    return lax.dot_general(a, b, (((1,), (1,)), ((), ())), preferred_element_type=F32)


def _ffn_kernel(x_ref, g_ref, wg_ref, wu_ref, wd_ref, g2_ref, *rest, emit_h, norm_out):
    acc_ref = rest[0]
    xn_sc = rest[-1]
    j = pl.program_id(1)

    @pl.when(j == 0)
    def _():
        xn_sc[...] = _rms(x_ref[...], g_ref[...]).astype(BF16)
        acc_ref[...] = jnp.zeros_like(acc_ref)

    xn = xn_sc[...]
    a = _dot(xn, wg_ref[...])
    b = _dot(xn, wu_ref[...])
    hmid = (a * jax.nn.sigmoid(a) * b).astype(BF16)
    acc_ref[...] += _dot(hmid, wd_ref[...])

    @pl.when(j == pl.num_programs(1) - 1)
    def _():
        y = x_ref[...] + 0.5 * acc_ref[...]
        acc_ref[...] = _rms(y, g2_ref[...]) if norm_out else y
        if emit_h:
            rest[1][...] = _rms(y, g2_ref[...]).astype(BF16)


def _ffn(x, gain, wg, wu, wd, gain2, layer, *, emit_h=False, norm_out=False):
    t, d = x.shape
    f = wg.shape[-1]
    bm, tf = min(FFN_ROWS_WITH_H if emit_h else FFN_ROWS, t), FFN_COLS
    assert t % bm == 0 and f % tf == 0
    out_shape = [jax.ShapeDtypeStruct((t, d), F32)]
    if emit_h:
        out_shape.append(jax.ShapeDtypeStruct((t, d), BF16))
    out_specs = [pl.BlockSpec((bm, d), lambda i, j: (i, 0)) for _ in out_shape]
    return pl.pallas_call(
        functools.partial(_ffn_kernel, emit_h=emit_h, norm_out=norm_out),
        grid=(t // bm, f // tf),
        in_specs=[
            pl.BlockSpec((bm, d), lambda i, j: (i, 0)),
            pl.BlockSpec((1, d), lambda i, j: (0, 0)),
            pl.BlockSpec((None, d, tf), lambda i, j: (layer, 0, j)),
            pl.BlockSpec((None, d, tf), lambda i, j: (layer, 0, j)),
            pl.BlockSpec((None, tf, d), lambda i, j: (layer, j, 0)),
            pl.BlockSpec((1, d), lambda i, j: (0, 0)),
        ],
        out_specs=out_specs,
        out_shape=out_shape,
        scratch_shapes=[pltpu.VMEM((bm, d), BF16)],
        compiler_params=_params("parallel", "arbitrary"),
        name="ffn",
    )(x, gain, wg, wu, wd, gain2)


def _proj_t_kernel(wt_ref, h_ref, o_ref, *, scale, scaled_rows, bn):
    y = _dot_nt(wt_ref[...], h_ref[...])
    if scaled_rows:
        j = pl.program_id(1)
        y = y * jnp.where(j * bn < scaled_rows, scale, 1.0).astype(F32)
    o_ref[...] = y.astype(o_ref.dtype)


def _proj_t(h, wt, layer, out_dtype, *, scale=1.0, scaled_rows=0):
    t, d = h.shape
    n = wt.shape[-2]
    bm, bn = min(MM_ROWS, t), min(MM_COLS, n)
    assert t % bm == 0 and n % bn == 0 and scaled_rows % bn == 0
    return pl.pallas_call(
        functools.partial(_proj_t_kernel, scale=scale, scaled_rows=scaled_rows, bn=bn),
        grid=(t // bm, n // bn),
        in_specs=[
            pl.BlockSpec((None, bn, d), lambda i, j: (layer, j, 0)),
            pl.BlockSpec((bm, d), lambda i, j: (i, 0)),
        ],
        out_specs=pl.BlockSpec((bn, bm), lambda i, j: (j, i)),
        out_shape=jax.ShapeDtypeStruct((n, t), out_dtype),
        compiler_params=_params("parallel", "arbitrary"),
        name="proj_t",
    )(wt, h)


def _proj_kernel(h_ref, w_ref, k_ref, z_ref):
    y = _dot(h_ref[...], w_ref[...])
    nk = k_ref.shape[1]
    k_ref[...] = y[:, :nk].astype(k_ref.dtype)
    z_ref[...] = y[:, nk:]


def _proj_kz(h, w, layer, nk):
    t, d = h.shape
    n = w.shape[-1]
    bm = min(MM_ROWS, t)
    assert t % bm == 0
    return pl.pallas_call(
        _proj_kernel,
        grid=(t // bm,),
        in_specs=[
            pl.BlockSpec((bm, d), lambda i: (i, 0)),
            pl.BlockSpec((None, d, n), lambda i: (layer, 0, 0)),
        ],
        out_specs=[
            pl.BlockSpec((bm, nk), lambda i: (i, 0)),
            pl.BlockSpec((bm, n - nk), lambda i: (i, 0)),
        ],
        out_shape=[
            jax.ShapeDtypeStruct((t, nk), BF16),
            jax.ShapeDtypeStruct((t, n - nk), F32),
        ],
        compiler_params=_params("parallel"),
        name="proj_kz",
    )(h, w)


def _cumsum_kernel(z_ref, b_ref, o_ref):
    s = z_ref.shape[0]
    blk = 128
    r = lax.broadcasted_iota(jnp.int32, (blk, blk), 0)
    c = lax.broadcasted_iota(jnp.int32, (blk, blk), 1)
    tril = jnp.where(r >= c, 1.0, 0.0).astype(F32)
    carry = jnp.zeros((1, z_ref.shape[1]), F32)
    for i in range(s // blk):
        z = z_ref[i * blk:(i + 1) * blk, :] + b_ref[...]
        log_f = jnp.minimum(z, 0.0) - jnp.log1p(jnp.exp(-jnp.abs(z)))
        cs = jnp.dot(tril, log_f, precision=lax.Precision.HIGHEST,
                     preferred_element_type=F32) + carry
        o_ref[i * blk:(i + 1) * blk, :] = cs
        carry = cs[blk - 1:blk, :]


def _forget_cumsum(z, bias):
    b, s, w = z.shape
    return pl.pallas_call(
        _cumsum_kernel,
        grid=(b,),
        in_specs=[
            pl.BlockSpec((None, s, w), lambda i: (i, 0, 0)),
            pl.BlockSpec((1, w), lambda i: (0, 0)),
        ],
        out_specs=pl.BlockSpec((None, s, w), lambda i: (i, 0, 0)),
        out_shape=jax.ShapeDtypeStruct((b, s, w), F32),
        compiler_params=_params("parallel"),
        name="forget_cumsum",
    )(z, bias)


LOG2E = float(np.log2(np.e))
BIAS_TERMS = 3
AUG_ROWS = 16


def _split_bf16(x):
    hi = x.astype(BF16).astype(F32)
    r = x - hi
    mid = r.astype(BF16).astype(F32)
    lo = (r - mid).astype(BF16).astype(F32)
    return hi, mid, lo


def _bias_rows(c, idx, sign):
    hi, mid, lo = _split_bf16(c)
    first = 0 if sign > 0 else BIAS_TERMS
    ones_first = BIAS_TERMS if sign > 0 else 0
    out = jnp.where((idx >= ones_first) & (idx < ones_first + BIAS_TERMS), 1.0, 0.0)
    out = jnp.where(idx == first, sign * hi, out)
    out = jnp.where(idx == first + 1, sign * mid, out)
    out = jnp.where(idx == first + 2, sign * lo, out)
    return out.astype(BF16)


def _fox_kernel(qt_ref, k_ref, vt_ref, c_ref, g_ref, o_ref, kaug_sc, qaug_sc, m_sc, l_sc, acc_sc,
                *, bq, bk, hps):
    qi = pl.program_id(2)
    s_len = k_ref.shape[0]
    d = HEAD_DIM

    @pl.when(qi == 0)
    def _():
        lane = lax.broadcasted_iota(jnp.int32, (d, d), 1)
        for hh in range(hps):
            kaug_sc[hh, :, 0:d] = k_ref[:, hh * d:(hh + 1) * d]
            for i in range(s_len // d):
                c_row = c_ref[hh, :, i * d:(i + 1) * d] * LOG2E
                c_col = jnp.transpose(jnp.broadcast_to(c_row, (d, d)))
                kaug_sc[hh, i * d:(i + 1) * d, d:2 * d] = _bias_rows(c_col, lane, -1.0)

    q0 = pl.multiple_of(qi * bq, bq)
    sub = lax.broadcasted_iota(jnp.int32, (AUG_ROWS, bq), 0)
    for hh in range(hps):
        cq = jnp.broadcast_to(c_ref[hh, :, pl.ds(q0, bq)] * LOG2E, (AUG_ROWS, bq))
        qaug_sc[hh, 0:d, :] = qt_ref[hh * d:(hh + 1) * d, :]
        qaug_sc[hh, d:d + AUG_ROWS, :] = _bias_rows(cq, sub, 1.0)
        qaug_sc[hh, d + AUG_ROWS:, :] = jnp.zeros((d - AUG_ROWS, bq), BF16)
    m_sc[...] = jnp.full_like(m_sc, -jnp.inf)
    l_sc[...] = jnp.zeros_like(l_sc)
    acc_sc[...] = jnp.zeros_like(acc_sc)

    def step(ki, lo, width, masked):
        cols = slice(lo, lo + width)
        k0 = pl.multiple_of(ki * bk, bk)
        scores = [_dot(kaug_sc[hh, pl.ds(k0, bk), :], qaug_sc[hh, :, cols]) for hh in range(hps)]
        for hh in range(hps):
            s = scores[hh]
            if masked:
                r = lax.broadcasted_iota(jnp.int32, (bk, width), 0)
                c = lax.broadcasted_iota(jnp.int32, (bk, width), 1)
                s = jnp.where(r - c <= q0 + lo - k0, s, -jnp.inf)
            m_prev = m_sc[hh, :, cols]
            m_new = jnp.maximum(m_prev, jnp.max(s, axis=0, keepdims=True))
            alpha = jnp.exp2(m_prev - m_new)
            p = jnp.exp2(s - m_new)
            l_sc[hh, :, cols] = alpha * l_sc[hh, :, cols] + jnp.sum(p, axis=0, keepdims=True)
            pv = _dot(vt_ref[hh * d:(hh + 1) * d, pl.ds(k0, bk)], p.astype(BF16))
            acc_sc[hh, :, cols] = alpha * acc_sc[hh, :, cols] + pv
            m_sc[hh, :, cols] = m_new

    def body(ki, carry):
        step(ki, 0, bq, False)
        return carry

    lax.fori_loop(0, 2 * qi, body, 0)
    step(2 * qi, 0, bq, True)
    step(2 * qi + 1, bk, bk, True)
    for hh in range(hps):
        y = jnp.transpose(acc_sc[hh] / l_sc[hh])
        o_ref[:, hh * d:(hh + 1) * d] = _rms(y, g_ref[:, hh * d:(hh + 1) * d]).astype(o_ref.dtype)


def _fox(qvt, k, c_row, gain, heads):
    b, s, _ = k.shape
    blk = min(FOX_Q_ROWS, s)
    nq = s // blk
    hps = FOX_HEADS_PER_STEP
    w = hps * HEAD_DIM
    assert s % blk == 0 and blk % 2 == 0 and heads % hps == 0
    return pl.pallas_call(
        functools.partial(_fox_kernel, bq=blk, bk=blk // 2, hps=hps),
        grid=(b, heads // hps, nq),
        in_specs=[
            pl.BlockSpec((w, blk), lambda bi, h, qi: (h, bi * nq + qi)),
            pl.BlockSpec((None, s, w), lambda bi, h, qi: (bi, 0, h)),
            pl.BlockSpec((w, s), lambda bi, h, qi: (heads // hps + h, bi)),
            pl.BlockSpec((None, hps, 1, s), lambda bi, h, qi: (bi, h, 0, 0)),
            pl.BlockSpec((1, w), lambda bi, h, qi: (0, h)),
        ],
        out_specs=pl.BlockSpec((None, blk, w), lambda bi, h, qi: (bi, qi, h)),
        out_shape=jax.ShapeDtypeStruct((b, s, heads * HEAD_DIM), BF16),
        scratch_shapes=[
            pltpu.VMEM((hps, s, 2 * HEAD_DIM), BF16),
            pltpu.VMEM((hps, 2 * HEAD_DIM, blk), BF16),
            pltpu.VMEM((hps, 1, blk), F32),
            pltpu.VMEM((hps, 1, blk), F32),
            pltpu.VMEM((hps, HEAD_DIM, blk), F32),
        ],
        compiler_params=_params("parallel", "parallel", "arbitrary"),
        name="fox",
    )(qvt, k, qvt, c_row, gain)


def _ret_kernel(h_ref, w_ref, cos_ref, sin_ref, gain_ref, o_ref, p_sc, state_sc, *, chunk, heads):
    n = pl.program_id(1)
    d = HEAD_DIM
    w = heads * d

    @pl.when(n == 0)
    def _():
        state_sc[...] = jnp.zeros_like(state_sc)

    p_sc[...] = _dot(h_ref[...], w_ref[...])
    cos = cos_ref[...]
    sin = sin_ref[...]

    def rope(x):
        return x * cos + pltpu.roll(x, d // 2, 1) * sin

    t = lax.broadcasted_iota(jnp.int32, (chunk, 1), 0).astype(F32)
    r = lax.broadcasted_iota(jnp.int32, (chunk, chunk), 0)
    c = lax.broadcasted_iota(jnp.int32, (chunk, chunk), 1)
    diff = (r - c).astype(F32)
    for hh in range(heads):
        lg = float(np.log1p(-np.exp2(-(RET_GAMMA_BASE + hh))))
        lo, hi = hh * d, (hh + 1) * d
        q = rope(p_sc[:, lo:hi])
        k = rope(p_sc[:, w + lo:w + hi]) * (d ** -0.5)
        v = p_sc[:, 2 * w + lo:2 * w + hi].astype(BF16)
        gate = p_sc[:, 3 * w + lo:3 * w + hi]
        xi = jnp.exp((t + 1.0) * lg)
        zeta = jnp.exp((chunk - 1.0 - t) * lg)
        decay = jnp.where(diff >= 0.0, jnp.exp(jnp.maximum(diff, 0.0) * lg), 0.0)
        qb = q.astype(BF16)
        scores = _dot_nt(qb, k.astype(BF16)) * decay
        state = state_sc[hh]
        o = _dot(scores.astype(BF16), v) + _dot(qb, state.astype(BF16)) * xi
        kz_t = jnp.transpose(k * zeta).astype(BF16)
        state_sc[hh] = state * float(np.exp(chunk * lg)) + _dot(kz_t, v)
        y = _rms(o, gain_ref[:, lo:hi]) * (gate * jax.nn.sigmoid(gate))
        o_ref[:, lo:hi] = y.astype(o_ref.dtype)


def _retention(h, w_ret, layer, cos2, sin2, gain, heads, b, s):
    t, dm = h.shape
    chunk = min(RET_CHUNK, s)
    nc = s // chunk
    w = heads * HEAD_DIM
    assert s % chunk == 0 and t == b * s
    return pl.pallas_call(
        functools.partial(_ret_kernel, chunk=chunk, heads=heads),
        grid=(b, nc),
        in_specs=[
            pl.BlockSpec((chunk, dm), lambda bi, n: (bi * nc + n, 0)),
            pl.BlockSpec((None, dm, 4 * w), lambda bi, n: (layer, 0, 0)),
            pl.BlockSpec((chunk, HEAD_DIM), lambda bi, n: (n, 0)),
            pl.BlockSpec((chunk, HEAD_DIM), lambda bi, n: (n, 0)),
            pl.BlockSpec((1, w), lambda bi, n: (0, 0)),
        ],
        out_specs=pl.BlockSpec((chunk, w), lambda bi, n: (bi * nc + n, 0)),
        out_shape=jax.ShapeDtypeStruct((t, w), BF16),
        scratch_shapes=[
            pltpu.VMEM((chunk, 4 * w), F32),
            pltpu.VMEM((heads, HEAD_DIM, HEAD_DIM), F32),
        ],
        compiler_params=_params("parallel", "arbitrary"),
        name="retention",
    )(h, w_ret, cos2, sin2, gain)


def _gelu(x):
    return 0.5 * x * (1.0 + jnp.tanh(np.sqrt(2.0 / np.pi).astype(np.float32) * (x + 0.044715 * (x * x * x))))


def _gmlp_kernel(h_ref, w_ref, lng_ref, lnb_ref, ws_ref, bs_ref, gain_ref, o_ref, uv_ref,
                 *, groups, rows):
    w = groups * HEAD_DIM
    blk = CHUNK_GMLP
    uv_ref[...] = _dot(h_ref[...], w_ref[...])
    r = lax.broadcasted_iota(jnp.int32, (blk, blk), 0)
    c = lax.broadcasted_iota(jnp.int32, (blk, blk), 1)
    for g in range(groups):
        lo, hi = g * HEAD_DIM, (g + 1) * HEAD_DIM
        wm = jnp.where(r >= c, ws_ref[g], 0.0).astype(BF16)
        bias = bs_ref[g]
        for ci in range(rows // blk):
            rs = slice(ci * blk, (ci + 1) * blk)
            u = _gelu(uv_ref[rs, lo:hi])
            v = _gelu(uv_ref[rs, w + lo:w + hi])
            mu = jnp.mean(v, axis=-1, keepdims=True)
            var = jnp.mean(jnp.square(v - mu), axis=-1, keepdims=True)
            v = (v - mu) * lax.rsqrt(var + EPS) * lng_ref[:, lo:hi] + lnb_ref[:, lo:hi]
            y = u * (_dot(wm, v.astype(BF16)) + bias)
            o_ref[rs, lo:hi] = _rms(y, gain_ref[:, lo:hi]).astype(o_ref.dtype)


def _gmlp(h, w_gm, layer, ln_g, ln_b, w_s, b_s, gain, groups):
    t, dm = h.shape
    w = groups * HEAD_DIM
    rows = min(GMLP_ROWS, t)
    assert t % rows == 0 and rows % CHUNK_GMLP == 0
    return pl.pallas_call(
        functools.partial(_gmlp_kernel, groups=groups, rows=rows),
        grid=(t // rows,),
        in_specs=[
            pl.BlockSpec((rows, dm), lambda i: (i, 0)),
            pl.BlockSpec((None, dm, 2 * w), lambda i: (layer, 0, 0)),
            pl.BlockSpec((1, w), lambda i: (0, 0)),
            pl.BlockSpec((1, w), lambda i: (0, 0)),
            pl.BlockSpec((groups, CHUNK_GMLP, CHUNK_GMLP), lambda i: (0, 0, 0)),
            pl.BlockSpec((groups, CHUNK_GMLP, 1), lambda i: (0, 0, 0)),
            pl.BlockSpec((1, w), lambda i: (0, 0)),
        ],
        out_specs=pl.BlockSpec((rows, w), lambda i: (i, 0)),
        out_shape=jax.ShapeDtypeStruct((t, w), BF16),
        scratch_shapes=[pltpu.VMEM((rows, 2 * w), F32)],
        compiler_params=_params("parallel"),
        name="gmlp",
    )(h, w_gm, ln_g, ln_b, w_s, b_s, gain)


def _outproj_kernel(x_ref, ya_ref, yb_ref, yc_ref, w_ref, o_ref):
    wa, wb = ya_ref.shape[1], yb_ref.shape[1]
    acc = _dot(ya_ref[...], w_ref[0:wa, :])
    acc += _dot(yb_ref[...], w_ref[wa:wa + wb, :])
    acc += _dot(yc_ref[...], w_ref[wa + wb:, :])
    o_ref[...] = x_ref[...] + acc


def _outproj(x, ya, yb, yc, w, layer):
    t, d = x.shape
    k = w.shape[-2]
    bm, bn = min(MM_ROWS, t), min(MM_COLS, d)
    assert t % bm == 0 and d % bn == 0
    return pl.pallas_call(
        _outproj_kernel,
        grid=(t // bm, d // bn),
        in_specs=[
            pl.BlockSpec((bm, bn), lambda i, j: (i, j)),
            pl.BlockSpec((bm, ya.shape[1]), lambda i, j: (i, 0)),
            pl.BlockSpec((bm, yb.shape[1]), lambda i, j: (i, 0)),
            pl.BlockSpec((bm, yc.shape[1]), lambda i, j: (i, 0)),
            pl.BlockSpec((None, k, bn), lambda i, j: (layer, 0, j)),
        ],
        out_specs=pl.BlockSpec((bm, bn), lambda i, j: (i, j)),
        out_shape=jax.ShapeDtypeStruct((t, d), F32),
        compiler_params=_params("parallel", "arbitrary"),
        name="outproj",
    )(x, ya, yb, yc, w)


def kernel(x, ffn1_norm, ffn1_w_gate, ffn1_w_up, ffn1_w_down, mix_norm, w_in, fox_b_f,
           gmlp_ln_g, gmlp_ln_b, gmlp_w_s, gmlp_b_s, out_norm, w_out, ffn2_norm,
           ffn2_w_gate, ffn2_w_up, ffn2_w_down, final_norm):
    b, s, d = x.shape
    depth = w_in.shape[0]
    t = b * s
    n_heads = d // HEAD_DIM
    fox_h, ret_h = n_heads // 2, n_heads // 4
    gm_g = n_heads - fox_h - ret_h
    fox_w, ret_w, gm_w = fox_h * HEAD_DIM, ret_h * HEAD_DIM, gm_g * HEAD_DIM

    o_fz = 3 * fox_w
    o_ret = o_fz + fox_h
    o_gm = o_ret + 4 * ret_w
    w_qv_t = jnp.transpose(
        jnp.concatenate([w_in[:, :, :fox_w], w_in[:, :, 2 * fox_w:o_fz]], axis=2), (0, 2, 1)
    ).astype(BF16)
    w_kz = jnp.concatenate(
        [w_in[:, :, fox_w:2 * fox_w],
         jnp.pad(w_in[:, :, o_fz:o_ret], ((0, 0), (0, 0), (0, HEAD_DIM - fox_h)))], axis=2
    ).astype(BF16)
    w_ret = w_in[:, :, o_ret:o_gm].astype(BF16)
    w_gm = w_in[:, :, o_gm:].astype(BF16)
    w_o = w_out.astype(BF16)
    f1 = [w.astype(BF16) for w in (ffn1_w_gate, ffn1_w_up, ffn1_w_down)]
    f2 = [w.astype(BF16) for w in (ffn2_w_gate, ffn2_w_up, ffn2_w_down)]
    fz_bias = jnp.pad(fox_b_f, ((0, 0), (0, HEAD_DIM - fox_h)))

    half = HEAD_DIM // 2
    pos = jnp.arange(s, dtype=F32)
    inv_freq = ROPE_BASE ** (-jnp.arange(half, dtype=F32) / half)
    ang = pos[:, None] * inv_freq[None, :]
    cos2 = jnp.concatenate([jnp.cos(ang), jnp.cos(ang)], axis=-1)
    sin2 = jnp.concatenate([-jnp.sin(ang), jnp.sin(ang)], axis=-1)

    xf = x.reshape(t, d)
    for l in range(depth):
        x1, h = _ffn(xf, ffn1_norm[l][None], *f1, mix_norm[l][None], l, emit_h=True)
        qvt = _proj_t(h, w_qv_t, l, BF16, scale=HEAD_DIM ** -0.5 * LOG2E, scaled_rows=fox_w)
        kf, fz = _proj_kz(h, w_kz, l, fox_w)

        c = _forget_cumsum(fz.reshape(b, s, HEAD_DIM), fz_bias[l][None])
        c = jnp.transpose(c[:, :, :fox_h], (0, 2, 1))
        gains = out_norm[l][None]
        ya = _fox(qvt, kf.reshape(b, s, fox_w), c[:, :, None, :], gains[:, :fox_w], fox_h)
        yb = _retention(h, w_ret, l, cos2, sin2, gains[:, fox_w:fox_w + ret_w], ret_h, b, s)
        yc = _gmlp(h, w_gm, l, gmlp_ln_g[l][None], gmlp_ln_b[l][None], gmlp_w_s[l],
                   gmlp_b_s[l][..., None], gains[:, fox_w + ret_w:], gm_g)
        x2 = _outproj(x1, ya.reshape(t, fox_w), yb, yc, w_o, l)

        (xf,) = _ffn(x2, ffn2_norm[l][None], *f2, final_norm[None], l, norm_out=l == depth - 1)
    return xf.reshape(b, s, d)
```

```python
import functools

import jax
import jax.numpy as jnp
import numpy as np
from jax import lax
from jax.experimental import pallas as pl
from jax.experimental.pallas import tpu as pltpu

F32 = jnp.float32
BF16 = jnp.bfloat16

HEAD_DIM = 128
CHUNK_GMLP = 128
ROPE_BASE = 10000.0
RET_GAMMA_BASE = 5.0
EPS = 1e-6

V7X_VMEM_BYTES = 64 * 1024 * 1024
VMEM_LIMIT = V7X_VMEM_BYTES - 1 * 1024 * 1024

FFN_ROWS = 1024
FFN_ROWS_WITH_H = 1024
FFN_NORM_ROWS = 128
FFN_COLS = 512
MM_ROWS = 1024
MM_COLS = 1024
FOX_Q_ROWS = 1024
FOX_HEADS_PER_STEP = 2
RET_CHUNK = 512
GMLP_ROWS = 512


def _params(*sem):
    return pltpu.CompilerParams(dimension_semantics=sem, vmem_limit_bytes=VMEM_LIMIT)


def _rms(x, gain):
    return x * lax.rsqrt(jnp.mean(x * x, axis=-1, keepdims=True) + EPS) * gain


def _dot(a, b):
    return jnp.dot(a, b, preferred_element_type=F32)


def _dot_nt(a, b):
    return lax.dot_general(a, b, (((1,), (1,)), ((), ())), preferred_element_type=F32)


def _ffn_kernel(x_ref, g_ref, wg_ref, wu_ref, wd_ref, g2_ref, *rest, emit_h, norm_out):
    acc_ref = rest[0]
    xn_sc = rest[-1]
    j = pl.program_id(1)
    n_strips = x_ref.shape[0] // FFN_NORM_ROWS

    def strip(r):
        return pl.ds(pl.multiple_of(r * FFN_NORM_ROWS, FFN_NORM_ROWS), FFN_NORM_ROWS)

    @pl.when(j == 0)
    def _():
        def body(r, carry):
            xn_sc[strip(r), :] = _rms(x_ref[strip(r), :], g_ref[...]).astype(BF16)
            acc_ref[strip(r), :] = jnp.zeros((FFN_NORM_ROWS, acc_ref.shape[1]), F32)
            return carry

        lax.fori_loop(0, n_strips, body, 0)

    xn = xn_sc[...]
    a = _dot(xn, wg_ref[...])
    b = _dot(xn, wu_ref[...])
    hmid = (a * jax.nn.sigmoid(a) * b).astype(BF16)
    acc_ref[...] += _dot(hmid, wd_ref[...])

    @pl.when(j == pl.num_programs(1) - 1)
    def _():
        def body(r, carry):
            y = x_ref[strip(r), :] + 0.5 * acc_ref[strip(r), :]
            acc_ref[strip(r), :] = _rms(y, g2_ref[...]) if norm_out else y
            if emit_h:
                rest[1][strip(r), :] = _rms(y, g2_ref[...]).astype(BF16)
            return carry

        lax.fori_loop(0, n_strips, body, 0)


def _ffn(x, gain, wg, wu, wd, gain2, layer, *, emit_h=False, norm_out=False):
    t, d = x.shape
    f = wg.shape[-1]
    bm, tf = min(FFN_ROWS_WITH_H if emit_h else FFN_ROWS, t), FFN_COLS
    assert t % bm == 0 and f % tf == 0
    out_shape = [jax.ShapeDtypeStruct((t, d), F32)]
    if emit_h:
        out_shape.append(jax.ShapeDtypeStruct((t, d), BF16))
    out_specs = [pl.BlockSpec((bm, d), lambda i, j: (i, 0)) for _ in out_shape]
    return pl.pallas_call(
        functools.partial(_ffn_kernel, emit_h=emit_h, norm_out=norm_out),
        grid=(t // bm, f // tf),
        in_specs=[
            pl.BlockSpec((bm, d), lambda i, j: (i, 0)),
            pl.BlockSpec((1, d), lambda i, j: (0, 0)),
            pl.BlockSpec((None, d, tf), lambda i, j: (layer, 0, j)),
            pl.BlockSpec((None, d, tf), lambda i, j: (layer, 0, j)),
            pl.BlockSpec((None, tf, d), lambda i, j: (layer, j, 0)),
            pl.BlockSpec((1, d), lambda i, j: (0, 0)),
        ],
        out_specs=out_specs,
        out_shape=out_shape,
        scratch_shapes=[pltpu.VMEM((bm, d), BF16)],
        compiler_params=_params("parallel", "arbitrary"),
        name="ffn",
    )(x, gain, wg, wu, wd, gain2)


def _proj_t_kernel(wt_ref, h_ref, o_ref, *, scale, scaled_rows, bn):
    y = _dot_nt(wt_ref[...], h_ref[...])
    if scaled_rows:
        j = pl.program_id(1)
        y = y * jnp.where(j * bn < scaled_rows, scale, 1.0).astype(F32)
    o_ref[...] = y.astype(o_ref.dtype)


def _proj_t(h, wt, layer, out_dtype, *, scale=1.0, scaled_rows=0):
    t, d = h.shape
    n = wt.shape[-2]
    bm, bn = min(MM_ROWS, t), min(MM_COLS, n)
    assert t % bm == 0 and n % bn == 0 and scaled_rows % bn == 0
    return pl.pallas_call(
        functools.partial(_proj_t_kernel, scale=scale, scaled_rows=scaled_rows, bn=bn),
        grid=(t // bm, n // bn),
        in_specs=[
            pl.BlockSpec((None, bn, d), lambda i, j: (layer, j, 0)),
            pl.BlockSpec((bm, d), lambda i, j: (i, 0)),
        ],
        out_specs=pl.BlockSpec((bn, bm), lambda i, j: (j, i)),
        out_shape=jax.ShapeDtypeStruct((n, t), out_dtype),
        compiler_params=_params("parallel", "arbitrary"),
        name="proj_t",
    )(wt, h)


def _proj_kernel(h_ref, w_ref, k_ref, z_ref):
    y = _dot(h_ref[...], w_ref[...])
    nk = k_ref.shape[1]
    k_ref[...] = y[:, :nk].astype(k_ref.dtype)
    z_ref[...] = y[:, nk:]


def _proj_kz(h, w, layer, nk):
    t, d = h.shape
    n = w.shape[-1]
    bm = min(MM_ROWS, t)
    assert t % bm == 0
    return pl.pallas_call(
        _proj_kernel,
        grid=(t // bm,),
        in_specs=[
            pl.BlockSpec((bm, d), lambda i: (i, 0)),
            pl.BlockSpec((None, d, n), lambda i: (layer, 0, 0)),
        ],
        out_specs=[
            pl.BlockSpec((bm, nk), lambda i: (i, 0)),
            pl.BlockSpec((bm, n - nk), lambda i: (i, 0)),
        ],
        out_shape=[
            jax.ShapeDtypeStruct((t, nk), BF16),
            jax.ShapeDtypeStruct((t, n - nk), F32),
        ],
        compiler_params=_params("parallel"),
        name="proj_kz",
    )(h, w)


def _cumsum_kernel(z_ref, b_ref, o_ref):
    s = z_ref.shape[0]
    blk = 128
    r = lax.broadcasted_iota(jnp.int32, (blk, blk), 0)
    c = lax.broadcasted_iota(jnp.int32, (blk, blk), 1)
    tril = jnp.where(r >= c, 1.0, 0.0).astype(F32)
    carry = jnp.zeros((1, z_ref.shape[1]), F32)
    for i in range(s // blk):
        z = z_ref[i * blk:(i + 1) * blk, :] + b_ref[...]
        log_f = jnp.minimum(z, 0.0) - jnp.log1p(jnp.exp(-jnp.abs(z)))
        cs = jnp.dot(tril, log_f, precision=lax.Precision.HIGHEST,
                     preferred_element_type=F32) + carry
        o_ref[i * blk:(i + 1) * blk, :] = cs
        carry = cs[blk - 1:blk, :]


def _forget_cumsum(z, bias):
    b, s, w = z.shape
    return pl.pallas_call(
        _cumsum_kernel,
        grid=(b,),
        in_specs=[
            pl.BlockSpec((None, s, w), lambda i: (i, 0, 0)),
            pl.BlockSpec((1, w), lambda i: (0, 0)),
        ],
        out_specs=pl.BlockSpec((None, s, w), lambda i: (i, 0, 0)),
        out_shape=jax.ShapeDtypeStruct((b, s, w), F32),
        compiler_params=_params("parallel"),
        name="forget_cumsum",
    )(z, bias)


LOG2E = float(np.log2(np.e))
BIAS_TERMS = 3
AUG_ROWS = 16


def _split_bf16(x):
    hi = x.astype(BF16).astype(F32)
    r = x - hi
    mid = r.astype(BF16).astype(F32)
    lo = (r - mid).astype(BF16).astype(F32)
    return hi, mid, lo


def _bias_rows(c, idx, sign):
    hi, mid, lo = _split_bf16(c)
    first = 0 if sign > 0 else BIAS_TERMS
    ones_first = BIAS_TERMS if sign > 0 else 0
    out = jnp.where((idx >= ones_first) & (idx < ones_first + BIAS_TERMS), 1.0, 0.0)
    out = jnp.where(idx == first, sign * hi, out)
    out = jnp.where(idx == first + 1, sign * mid, out)
    out = jnp.where(idx == first + 2, sign * lo, out)
    return out.astype(BF16)


def _fox_kernel(qt_ref, k_ref, vt_ref, c_ref, g_ref, o_ref, kaug_sc, qaug_sc, s_sc, m_sc, l_sc, acc_sc,
                *, bq, bk, hps):
    qi = pl.program_id(2)
    s_len = k_ref.shape[0]
    d = HEAD_DIM

    @pl.when(qi == 0)
    def _():
        lane = lax.broadcasted_iota(jnp.int32, (d, d), 1)
        for hh in range(hps):
            kaug_sc[hh, :, 0:d] = k_ref[:, hh * d:(hh + 1) * d]
            for i in range(s_len // d):
                c_row = c_ref[hh, :, i * d:(i + 1) * d] * LOG2E
                c_col = jnp.transpose(jnp.broadcast_to(c_row, (d, d)))
                kaug_sc[hh, i * d:(i + 1) * d, d:2 * d] = _bias_rows(c_col, lane, -1.0)

    q0 = pl.multiple_of(qi * bq, bq)
    sub = lax.broadcasted_iota(jnp.int32, (AUG_ROWS, bq), 0)
    for hh in range(hps):
        cq = jnp.broadcast_to(c_ref[hh, :, pl.ds(q0, bq)] * LOG2E, (AUG_ROWS, bq))
        qaug_sc[hh, 0:d, :] = qt_ref[hh * d:(hh + 1) * d, :]
        qaug_sc[hh, d:d + AUG_ROWS, :] = _bias_rows(cq, sub, 1.0)
        qaug_sc[hh, d + AUG_ROWS:, :] = jnp.zeros((d - AUG_ROWS, bq), BF16)
    m_sc[...] = jnp.full_like(m_sc, -jnp.inf)
    l_sc[...] = jnp.zeros_like(l_sc)
    acc_sc[...] = jnp.zeros_like(acc_sc)

    def scores(ki, slot, lo, width):
        k0 = pl.multiple_of(ki * bk, bk)
        for hh in range(hps):
            s_sc[slot, hh, :, lo:lo + width] = _dot(kaug_sc[hh, pl.ds(k0, bk), :],
                                                    qaug_sc[hh, :, lo:lo + width])

    def absorb(ki, slot, lo, width, masked):
        cols = slice(lo, lo + width)
        k0 = pl.multiple_of(ki * bk, bk)
        for hh in range(hps):
            s = s_sc[slot, hh, :, cols]
            if masked:
                r = lax.broadcasted_iota(jnp.int32, (bk, width), 0)
                c = lax.broadcasted_iota(jnp.int32, (bk, width), 1)
                s = jnp.where(r - c <= q0 + lo - k0, s, -jnp.inf)
            m_prev = m_sc[hh, :, cols]
            m_new = jnp.maximum(m_prev, jnp.max(s, axis=0, keepdims=True))
            alpha = jnp.exp2(m_prev - m_new)
            p = jnp.exp2(s - m_new)
            l_sc[hh, :, cols] = alpha * l_sc[hh, :, cols] + jnp.sum(p, axis=0, keepdims=True)
            pv = _dot(vt_ref[hh * d:(hh + 1) * d, pl.ds(k0, bk)], p.astype(BF16))
            acc_sc[hh, :, cols] = alpha * acc_sc[hh, :, cols] + pv
            m_sc[hh, :, cols] = m_new

    scores(0, 0, 0, bq)

    def body(pair, carry):
        ki = 2 * pair
        scores(ki + 1, 1, 0, bq)
        absorb(ki, 0, 0, bq, False)
        scores(ki + 2, 0, 0, bq)
        absorb(ki + 1, 1, 0, bq, False)
        return carry

    lax.fori_loop(0, qi, body, 0)
    scores(2 * qi + 1, 1, bk, bk)
    absorb(2 * qi, 0, 0, bq, True)
    absorb(2 * qi + 1, 1, bk, bk, True)
    for hh in range(hps):
        y = jnp.transpose(acc_sc[hh] / l_sc[hh])
        o_ref[:, hh * d:(hh + 1) * d] = _rms(y, g_ref[:, hh * d:(hh + 1) * d]).astype(o_ref.dtype)


def _fox(qvt, k, c_row, gain, heads):
    b, s, _ = k.shape
    blk = min(FOX_Q_ROWS, s)
    nq = s // blk
    hps = FOX_HEADS_PER_STEP
    w = hps * HEAD_DIM
    assert s % blk == 0 and blk % 2 == 0 and heads % hps == 0
    return pl.pallas_call(
        functools.partial(_fox_kernel, bq=blk, bk=blk // 2, hps=hps),
        grid=(b, heads // hps, nq),
        in_specs=[
            pl.BlockSpec((w, blk), lambda bi, h, qi: (h, bi * nq + qi)),
            pl.BlockSpec((None, s, w), lambda bi, h, qi: (bi, 0, h)),
            pl.BlockSpec((w, s), lambda bi, h, qi: (heads // hps + h, bi)),
            pl.BlockSpec((None, hps, 1, s), lambda bi, h, qi: (bi, h, 0, 0)),
            pl.BlockSpec((1, w), lambda bi, h, qi: (0, h)),
        ],
        out_specs=pl.BlockSpec((None, blk, w), lambda bi, h, qi: (bi, qi, h)),
        out_shape=jax.ShapeDtypeStruct((b, s, heads * HEAD_DIM), BF16),
        scratch_shapes=[
            pltpu.VMEM((hps, s, 2 * HEAD_DIM), BF16),
            pltpu.VMEM((hps, 2 * HEAD_DIM, blk), BF16),
            pltpu.VMEM((2, hps, blk // 2, blk), F32),
            pltpu.VMEM((hps, 1, blk), F32),
            pltpu.VMEM((hps, 1, blk), F32),
            pltpu.VMEM((hps, HEAD_DIM, blk), F32),
        ],
        compiler_params=_params("parallel", "parallel", "arbitrary"),
        name="fox",
    )(qvt, k, qvt, c_row, gain)


def _ret_kernel(h_ref, w_ref, cos_ref, sin_ref, gain_ref, o_ref, p_sc, state_sc, *, chunk, heads):
    n = pl.program_id(1)
    d = HEAD_DIM
    w = heads * d

    @pl.when(n == 0)
    def _():
        state_sc[...] = jnp.zeros_like(state_sc)

    p_sc[...] = _dot(h_ref[...], w_ref[...])
    cos = cos_ref[...]
    sin = sin_ref[...]

    def rope(x):
        return x * cos + pltpu.roll(x, d // 2, 1) * sin

    t = lax.broadcasted_iota(jnp.int32, (chunk, 1), 0).astype(F32)
    r = lax.broadcasted_iota(jnp.int32, (chunk, chunk), 0)
    c = lax.broadcasted_iota(jnp.int32, (chunk, chunk), 1)
    diff = (r - c).astype(F32)
    for hh in range(heads):
        lg = float(np.log1p(-np.exp2(-(RET_GAMMA_BASE + hh))))
        lo, hi = hh * d, (hh + 1) * d
        q = rope(p_sc[:, lo:hi])
        k = rope(p_sc[:, w + lo:w + hi]) * (d ** -0.5)
        v = p_sc[:, 2 * w + lo:2 * w + hi].astype(BF16)
        gate = p_sc[:, 3 * w + lo:3 * w + hi]
        xi = jnp.exp((t + 1.0) * lg)
        zeta = jnp.exp((chunk - 1.0 - t) * lg)
        decay = jnp.where(diff >= 0.0, jnp.exp(jnp.maximum(diff, 0.0) * lg), 0.0)
        qb = q.astype(BF16)
        scores = _dot_nt(qb, k.astype(BF16)) * decay
        state = state_sc[hh]
        o = _dot(scores.astype(BF16), v) + _dot(qb, state.astype(BF16)) * xi
        kz_t = jnp.transpose(k * zeta).astype(BF16)
        state_sc[hh] = state * float(np.exp(chunk * lg)) + _dot(kz_t, v)
        y = _rms(o, gain_ref[:, lo:hi]) * (gate * jax.nn.sigmoid(gate))
        o_ref[:, lo:hi] = y.astype(o_ref.dtype)


def _retention(h, w_ret, layer, cos2, sin2, gain, heads, b, s):
    t, dm = h.shape
    chunk = min(RET_CHUNK, s)
    nc = s // chunk
    w = heads * HEAD_DIM
    assert s % chunk == 0 and t == b * s
    return pl.pallas_call(
        functools.partial(_ret_kernel, chunk=chunk, heads=heads),
        grid=(b, nc),
        in_specs=[
            pl.BlockSpec((chunk, dm), lambda bi, n: (bi * nc + n, 0)),
            pl.BlockSpec((None, dm, 4 * w), lambda bi, n: (layer, 0, 0)),
            pl.BlockSpec((chunk, HEAD_DIM), lambda bi, n: (n, 0)),
            pl.BlockSpec((chunk, HEAD_DIM), lambda bi, n: (n, 0)),
            pl.BlockSpec((1, w), lambda bi, n: (0, 0)),
        ],
        out_specs=pl.BlockSpec((chunk, w), lambda bi, n: (bi * nc + n, 0)),
        out_shape=jax.ShapeDtypeStruct((t, w), BF16),
        scratch_shapes=[
            pltpu.VMEM((chunk, 4 * w), F32),
            pltpu.VMEM((heads, HEAD_DIM, HEAD_DIM), F32),
        ],
        compiler_params=_params("parallel", "arbitrary"),
        name="retention",
    )(h, w_ret, cos2, sin2, gain)


def _gelu(x):
    return 0.5 * x * (1.0 + jnp.tanh(np.sqrt(2.0 / np.pi).astype(np.float32) * (x + 0.044715 * (x * x * x))))


def _gmlp_kernel(h_ref, w_ref, lng_ref, lnb_ref, ws_ref, bs_ref, gain_ref, o_ref, uv_ref,
                 *, groups, rows):
    w = groups * HEAD_DIM
    blk = CHUNK_GMLP
    uv_ref[...] = _dot(h_ref[...], w_ref[...])
    r = lax.broadcasted_iota(jnp.int32, (blk, blk), 0)
    c = lax.broadcasted_iota(jnp.int32, (blk, blk), 1)
    for g in range(groups):
        lo, hi = g * HEAD_DIM, (g + 1) * HEAD_DIM
        wm = jnp.where(r >= c, ws_ref[g], 0.0).astype(BF16)
        bias = bs_ref[g]
        for ci in range(rows // blk):
            rs = slice(ci * blk, (ci + 1) * blk)
            u = _gelu(uv_ref[rs, lo:hi])
            v = _gelu(uv_ref[rs, w + lo:w + hi])
            mu = jnp.mean(v, axis=-1, keepdims=True)
            var = jnp.mean(jnp.square(v - mu), axis=-1, keepdims=True)
            v = (v - mu) * lax.rsqrt(var + EPS) * lng_ref[:, lo:hi] + lnb_ref[:, lo:hi]
            y = u * (_dot(wm, v.astype(BF16)) + bias)
            o_ref[rs, lo:hi] = _rms(y, gain_ref[:, lo:hi]).astype(o_ref.dtype)


def _gmlp(h, w_gm, layer, ln_g, ln_b, w_s, b_s, gain, groups):
    t, dm = h.shape
    w = groups * HEAD_DIM
    rows = min(GMLP_ROWS, t)
    assert t % rows == 0 and rows % CHUNK_GMLP == 0
    return pl.pallas_call(
        functools.partial(_gmlp_kernel, groups=groups, rows=rows),
        grid=(t // rows,),
        in_specs=[
            pl.BlockSpec((rows, dm), lambda i: (i, 0)),
            pl.BlockSpec((None, dm, 2 * w), lambda i: (layer, 0, 0)),
            pl.BlockSpec((1, w), lambda i: (0, 0)),
            pl.BlockSpec((1, w), lambda i: (0, 0)),
            pl.BlockSpec((groups, CHUNK_GMLP, CHUNK_GMLP), lambda i: (0, 0, 0)),
            pl.BlockSpec((groups, CHUNK_GMLP, 1), lambda i: (0, 0, 0)),
            pl.BlockSpec((1, w), lambda i: (0, 0)),
        ],
        out_specs=pl.BlockSpec((rows, w), lambda i: (i, 0)),
        out_shape=jax.ShapeDtypeStruct((t, w), BF16),
        scratch_shapes=[pltpu.VMEM((rows, 2 * w), F32)],
        compiler_params=_params("parallel"),
        name="gmlp",
    )(h, w_gm, ln_g, ln_b, w_s, b_s, gain)


def _outproj_kernel(x_ref, ya_ref, yb_ref, yc_ref, w_ref, o_ref):
    wa, wb = ya_ref.shape[1], yb_ref.shape[1]
    acc = _dot(ya_ref[...], w_ref[0:wa, :])
    acc += _dot(yb_ref[...], w_ref[wa:wa + wb, :])
    acc += _dot(yc_ref[...], w_ref[wa + wb:, :])
    o_ref[...] = x_ref[...] + acc


def _outproj(x, ya, yb, yc, w, layer):
    t, d = x.shape
    k = w.shape[-2]
    bm, bn = min(MM_ROWS, t), min(MM_COLS, d)
    assert t % bm == 0 and d % bn == 0
    return pl.pallas_call(
        _outproj_kernel,
        grid=(t // bm, d // bn),
        in_specs=[
            pl.BlockSpec((bm, bn), lambda i, j: (i, j)),
            pl.BlockSpec((bm, ya.shape[1]), lambda i, j: (i, 0)),
            pl.BlockSpec((bm, yb.shape[1]), lambda i, j: (i, 0)),
            pl.BlockSpec((bm, yc.shape[1]), lambda i, j: (i, 0)),
            pl.BlockSpec((None, k, bn), lambda i, j: (layer, 0, j)),
        ],
        out_specs=pl.BlockSpec((bm, bn), lambda i, j: (i, j)),
        out_shape=jax.ShapeDtypeStruct((t, d), F32),
        compiler_params=_params("parallel", "arbitrary"),
        name="outproj",
    )(x, ya, yb, yc, w)


def kernel(x, ffn1_norm, ffn1_w_gate, ffn1_w_up, ffn1_w_down, mix_norm, w_in, fox_b_f,
           gmlp_ln_g, gmlp_ln_b, gmlp_w_s, gmlp_b_s, out_norm, w_out, ffn2_norm,
           ffn2_w_gate, ffn2_w_up, ffn2_w_down, final_norm):
    b, s, d = x.shape
    depth = w_in.shape[0]
    t = b * s
    n_heads = d // HEAD_DIM
    fox_h, ret_h = n_heads // 2, n_heads // 4
    gm_g = n_heads - fox_h - ret_h
    fox_w, ret_w, gm_w = fox_h * HEAD_DIM, ret_h * HEAD_DIM, gm_g * HEAD_DIM

    o_fz = 3 * fox_w
    o_ret = o_fz + fox_h
    o_gm = o_ret + 4 * ret_w
    w_qv_t = jnp.transpose(
        jnp.concatenate([w_in[:, :, :fox_w], w_in[:, :, 2 * fox_w:o_fz]], axis=2), (0, 2, 1)
    ).astype(BF16)
    w_kz = jnp.concatenate(
        [w_in[:, :, fox_w:2 * fox_w],
         jnp.pad(w_in[:, :, o_fz:o_ret], ((0, 0), (0, 0), (0, HEAD_DIM - fox_h)))], axis=2
    ).astype(BF16)
    w_ret = w_in[:, :, o_ret:o_gm].astype(BF16)
    w_gm = w_in[:, :, o_gm:].astype(BF16)
    w_o = w_out.astype(BF16)
    f1 = [w.astype(BF16) for w in (ffn1_w_gate, ffn1_w_up, ffn1_w_down)]
    f2 = [w.astype(BF16) for w in (ffn2_w_gate, ffn2_w_up, ffn2_w_down)]
    fz_bias = jnp.pad(fox_b_f, ((0, 0), (0, HEAD_DIM - fox_h)))

    half = HEAD_DIM // 2
    pos = jnp.arange(s, dtype=F32)
    inv_freq = ROPE_BASE ** (-jnp.arange(half, dtype=F32) / half)
    ang = pos[:, None] * inv_freq[None, :]
    cos2 = jnp.concatenate([jnp.cos(ang), jnp.cos(ang)], axis=-1)
    sin2 = jnp.concatenate([-jnp.sin(ang), jnp.sin(ang)], axis=-1)

    xf = x.reshape(t, d)
    for l in range(depth):
        x1, h = _ffn(xf, ffn1_norm[l][None], *f1, mix_norm[l][None], l, emit_h=True)
        qvt = _proj_t(h, w_qv_t, l, BF16, scale=HEAD_DIM ** -0.5 * LOG2E, scaled_rows=fox_w)
        kf, fz = _proj_kz(h, w_kz, l, fox_w)

        c = _forget_cumsum(fz.reshape(b, s, HEAD_DIM), fz_bias[l][None])
        c = jnp.transpose(c[:, :, :fox_h], (0, 2, 1))
        gains = out_norm[l][None]
        ya = _fox(qvt, kf.reshape(b, s, fox_w), c[:, :, None, :], gains[:, :fox_w], fox_h)
        yb = _retention(h, w_ret, l, cos2, sin2, gains[:, fox_w:fox_w + ret_w], ret_h, b, s)
        yc = _gmlp(h, w_gm, l, gmlp_ln_g[l][None], gmlp_ln_b[l][None], gmlp_w_s[l],
                   gmlp_b_s[l][..., None], gains[:, fox_w + ret_w:], gm_g)
        x2 = _outproj(x1, ya.reshape(t, fox_w), yb, yc, w_o, l)

        (xf,) = _ffn(x2, ffn2_norm[l][None], *f2, final_norm[None], l, norm_out=l == depth - 1)
    return xf.reshape(b, s, d)
```

```python
import functools

import jax
import jax.numpy as jnp
import numpy as np
from jax import lax
from jax.experimental import pallas as pl
from jax.experimental.pallas import tpu as pltpu

F32 = jnp.float32
BF16 = jnp.bfloat16

HEAD_DIM = 128
CHUNK_GMLP = 128
ROPE_BASE = 10000.0
RET_GAMMA_BASE = 5.0
EPS = 1e-6

V7X_VMEM_BYTES = 64 * 1024 * 1024
VMEM_LIMIT = V7X_VMEM_BYTES - 1 * 1024 * 1024

FFN_ROWS = 1024
FFN_ROWS_WITH_H = 1024
FFN_NORM_ROWS = 128
FFN_COLS = 512
MM_ROWS = 1024
MM_COLS = 1024
FOX_Q_ROWS = 1024
FOX_HEADS_PER_STEP = 2
RET_ROWS = 512
RET_CHUNK = 256
GMLP_ROWS = 512


def _params(*sem):
    return pltpu.CompilerParams(dimension_semantics=sem, vmem_limit_bytes=VMEM_LIMIT)


def _rms(x, gain):
    return x * lax.rsqrt(jnp.mean(x * x, axis=-1, keepdims=True) + EPS) * gain


def _dot(a, b):
    return jnp.dot(a, b, preferred_element_type=F32)


def _dot_nt(a, b):
    return lax.dot_general(a, b, (((1,), (1,)), ((), ())), preferred_element_type=F32)


def _ffn_kernel(x_ref, g_ref, wg_ref, wu_ref, wd_ref, g2_ref, *rest, emit_h, norm_out):
    acc_ref = rest[0]
    xn_sc = rest[-1]
    j = pl.program_id(1)
    n_strips = x_ref.shape[0] // FFN_NORM_ROWS

    def strip(r):
        return pl.ds(pl.multiple_of(r * FFN_NORM_ROWS, FFN_NORM_ROWS), FFN_NORM_ROWS)

    @pl.when(j == 0)
    def _():
        def body(r, carry):
            xn_sc[strip(r), :] = _rms(x_ref[strip(r), :], g_ref[...]).astype(BF16)
            acc_ref[strip(r), :] = jnp.zeros((FFN_NORM_ROWS, acc_ref.shape[1]), F32)
            return carry

        lax.fori_loop(0, n_strips, body, 0)

    xn = xn_sc[...]
    a = _dot(xn, wg_ref[...])
    b = _dot(xn, wu_ref[...])
    hmid = (a * jax.nn.sigmoid(a) * b).astype(BF16)
    acc_ref[...] += _dot(hmid, wd_ref[...])

    @pl.when(j == pl.num_programs(1) - 1)
    def _():
        def body(r, carry):
            y = x_ref[strip(r), :] + 0.5 * acc_ref[strip(r), :]
            acc_ref[strip(r), :] = _rms(y, g2_ref[...]) if norm_out else y
            if emit_h:
                rest[1][strip(r), :] = _rms(y, g2_ref[...]).astype(BF16)
            return carry

        lax.fori_loop(0, n_strips, body, 0)


def _ffn(x, gain, wg, wu, wd, gain2, layer, *, emit_h=False, norm_out=False):
    t, d = x.shape
    f = wg.shape[-1]
    bm, tf = min(FFN_ROWS_WITH_H if emit_h else FFN_ROWS, t), FFN_COLS
    assert t % bm == 0 and f % tf == 0
    out_shape = [jax.ShapeDtypeStruct((t, d), F32)]
    if emit_h:
        out_shape.append(jax.ShapeDtypeStruct((t, d), BF16))
    out_specs = [pl.BlockSpec((bm, d), lambda i, j: (i, 0)) for _ in out_shape]
    return pl.pallas_call(
        functools.partial(_ffn_kernel, emit_h=emit_h, norm_out=norm_out),
        grid=(t // bm, f // tf),
        in_specs=[
            pl.BlockSpec((bm, d), lambda i, j: (i, 0)),
            pl.BlockSpec((1, d), lambda i, j: (0, 0)),
            pl.BlockSpec((None, d, tf), lambda i, j: (layer, 0, j)),
            pl.BlockSpec((None, d, tf), lambda i, j: (layer, 0, j)),
            pl.BlockSpec((None, tf, d), lambda i, j: (layer, j, 0)),
            pl.BlockSpec((1, d), lambda i, j: (0, 0)),
        ],
        out_specs=out_specs,
        out_shape=out_shape,
        scratch_shapes=[pltpu.VMEM((bm, d), BF16)],
        compiler_params=_params("parallel", "arbitrary"),
        name="ffn",
    )(x, gain, wg, wu, wd, gain2)


def _proj_t_kernel(wt_ref, h_ref, o_ref, *, scale, scaled_rows, bn):
    y = _dot_nt(wt_ref[...], h_ref[...])
    if scaled_rows:
        j = pl.program_id(1)
        y = y * jnp.where(j * bn < scaled_rows, scale, 1.0).astype(F32)
    o_ref[...] = y.astype(o_ref.dtype)


def _proj_t(h, wt, layer, out_dtype, *, scale=1.0, scaled_rows=0):
    t, d = h.shape
    n = wt.shape[-2]
    bm, bn = min(MM_ROWS, t), min(MM_COLS, n)
    assert t % bm == 0 and n % bn == 0 and scaled_rows % bn == 0
    return pl.pallas_call(
        functools.partial(_proj_t_kernel, scale=scale, scaled_rows=scaled_rows, bn=bn),
        grid=(t // bm, n // bn),
        in_specs=[
            pl.BlockSpec((None, bn, d), lambda i, j: (layer, j, 0)),
            pl.BlockSpec((bm, d), lambda i, j: (i, 0)),
        ],
        out_specs=pl.BlockSpec((bn, bm), lambda i, j: (j, i)),
        out_shape=jax.ShapeDtypeStruct((n, t), out_dtype),
        compiler_params=_params("parallel", "arbitrary"),
        name="proj_t",
    )(wt, h)


def _proj_kernel(h_ref, w_ref, k_ref, z_ref):
    y = _dot(h_ref[...], w_ref[...])
    nk = k_ref.shape[1]
    k_ref[...] = y[:, :nk].astype(k_ref.dtype)
    z_ref[...] = y[:, nk:]


def _proj_kz(h, w, layer, nk):
    t, d = h.shape
    n = w.shape[-1]
    bm = min(MM_ROWS, t)
    assert t % bm == 0
    return pl.pallas_call(
        _proj_kernel,
        grid=(t // bm,),
        in_specs=[
            pl.BlockSpec((bm, d), lambda i: (i, 0)),
            pl.BlockSpec((None, d, n), lambda i: (layer, 0, 0)),
        ],
        out_specs=[
            pl.BlockSpec((bm, nk), lambda i: (i, 0)),
            pl.BlockSpec((bm, n - nk), lambda i: (i, 0)),
        ],
        out_shape=[
            jax.ShapeDtypeStruct((t, nk), BF16),
            jax.ShapeDtypeStruct((t, n - nk), F32),
        ],
        compiler_params=_params("parallel"),
        name="proj_kz",
    )(h, w)


def _cumsum_kernel(z_ref, b_ref, o_ref):
    s = z_ref.shape[0]
    blk = 128
    r = lax.broadcasted_iota(jnp.int32, (blk, blk), 0)
    c = lax.broadcasted_iota(jnp.int32, (blk, blk), 1)
    tril = jnp.where(r >= c, 1.0, 0.0).astype(F32)
    carry = jnp.zeros((1, z_ref.shape[1]), F32)
    for i in range(s // blk):
        z = z_ref[i * blk:(i + 1) * blk, :] + b_ref[...]
        log_f = jnp.minimum(z, 0.0) - jnp.log1p(jnp.exp(-jnp.abs(z)))
        cs = jnp.dot(tril, log_f, precision=lax.Precision.HIGHEST,
                     preferred_element_type=F32) + carry
        o_ref[i * blk:(i + 1) * blk, :] = cs
        carry = cs[blk - 1:blk, :]


def _forget_cumsum(z, bias):
    b, s, w = z.shape
    return pl.pallas_call(
        _cumsum_kernel,
        grid=(b,),
        in_specs=[
            pl.BlockSpec((None, s, w), lambda i: (i, 0, 0)),
            pl.BlockSpec((1, w), lambda i: (0, 0)),
        ],
        out_specs=pl.BlockSpec((None, s, w), lambda i: (i, 0, 0)),
        out_shape=jax.ShapeDtypeStruct((b, s, w), F32),
        compiler_params=_params("parallel"),
        name="forget_cumsum",
    )(z, bias)


LOG2E = float(np.log2(np.e))
BIAS_TERMS = 3
AUG_ROWS = 16


def _split_bf16(x):
    hi = x.astype(BF16).astype(F32)
    r = x - hi
    mid = r.astype(BF16).astype(F32)
    lo = (r - mid).astype(BF16).astype(F32)
    return hi, mid, lo


def _bias_rows(c, idx, sign):
    hi, mid, lo = _split_bf16(c)
    first = 0 if sign > 0 else BIAS_TERMS
    ones_first = BIAS_TERMS if sign > 0 else 0
    out = jnp.where((idx >= ones_first) & (idx < ones_first + BIAS_TERMS), 1.0, 0.0)
    out = jnp.where(idx == first, sign * hi, out)
    out = jnp.where(idx == first + 1, sign * mid, out)
    out = jnp.where(idx == first + 2, sign * lo, out)
    return out.astype(BF16)


def _fox_kernel(qt_ref, k_ref, vt_ref, c_ref, g_ref, o_ref, kaug_sc, qaug_sc, s_sc, m_sc, l_sc, acc_sc,
                *, bq, bk, hps):
    qi = pl.program_id(2)
    s_len = k_ref.shape[0]
    d = HEAD_DIM

    @pl.when(qi == 0)
    def _():
        lane = lax.broadcasted_iota(jnp.int32, (d, d), 1)
        for hh in range(hps):
            kaug_sc[hh, :, 0:d] = k_ref[:, hh * d:(hh + 1) * d]
            for i in range(s_len // d):
                c_row = c_ref[hh, :, i * d:(i + 1) * d] * LOG2E
                c_col = jnp.transpose(jnp.broadcast_to(c_row, (d, d)))
                kaug_sc[hh, i * d:(i + 1) * d, d:2 * d] = _bias_rows(c_col, lane, -1.0)

    q0 = pl.multiple_of(qi * bq, bq)
    sub = lax.broadcasted_iota(jnp.int32, (AUG_ROWS, bq), 0)
    for hh in range(hps):
        cq = jnp.broadcast_to(c_ref[hh, :, pl.ds(q0, bq)] * LOG2E, (AUG_ROWS, bq))
        qaug_sc[hh, 0:d, :] = qt_ref[hh * d:(hh + 1) * d, :]
        qaug_sc[hh, d:d + AUG_ROWS, :] = _bias_rows(cq, sub, 1.0)
        qaug_sc[hh, d + AUG_ROWS:, :] = jnp.zeros((d - AUG_ROWS, bq), BF16)
    m_sc[...] = jnp.full_like(m_sc, -jnp.inf)
    l_sc[...] = jnp.zeros_like(l_sc)
    acc_sc[...] = jnp.zeros_like(acc_sc)

    def scores(ki, slot, lo, width):
        k0 = pl.multiple_of(ki * bk, bk)
        for hh in range(hps):
            s_sc[slot, hh, :, lo:lo + width] = _dot(kaug_sc[hh, pl.ds(k0, bk), :],
                                                    qaug_sc[hh, :, lo:lo + width])

    def absorb(ki, slot, lo, width, masked):
        cols = slice(lo, lo + width)
        k0 = pl.multiple_of(ki * bk, bk)
        for hh in range(hps):
            s = s_sc[slot, hh, :, cols]
            if masked:
                r = lax.broadcasted_iota(jnp.int32, (bk, width), 0)
                c = lax.broadcasted_iota(jnp.int32, (bk, width), 1)
                s = jnp.where(r - c <= q0 + lo - k0, s, -jnp.inf)
            m_prev = m_sc[hh, :, cols]
            m_new = jnp.maximum(m_prev, jnp.max(s, axis=0, keepdims=True))
            alpha = jnp.exp2(m_prev - m_new)
            p = jnp.exp2(s - m_new)
            l_sc[hh, :, cols] = alpha * l_sc[hh, :, cols] + jnp.sum(p, axis=0, keepdims=True)
            pv = _dot(vt_ref[hh * d:(hh + 1) * d, pl.ds(k0, bk)], p.astype(BF16))
            acc_sc[hh, :, cols] = alpha * acc_sc[hh, :, cols] + pv
            m_sc[hh, :, cols] = m_new

    scores(0, 0, 0, bq)

    def body(pair, carry):
        ki = 2 * pair
        scores(ki + 1, 1, 0, bq)
        absorb(ki, 0, 0, bq, False)
        scores(ki + 2, 0, 0, bq)
        absorb(ki + 1, 1, 0, bq, False)
        return carry

    lax.fori_loop(0, qi, body, 0)
    scores(2 * qi + 1, 1, bk, bk)
    absorb(2 * qi, 0, 0, bq, True)
    absorb(2 * qi + 1, 1, bk, bk, True)
    for hh in range(hps):
        y_t = acc_sc[hh] * (1.0 / l_sc[hh])
        y_t = y_t * lax.rsqrt(jnp.mean(y_t * y_t, axis=0, keepdims=True) + EPS)
        o_ref[:, hh * d:(hh + 1) * d] = (
            jnp.transpose(y_t) * g_ref[:, hh * d:(hh + 1) * d]).astype(o_ref.dtype)


def _fox(qvt, k, c_row, gain, heads):
    b, s, _ = k.shape
    blk = min(FOX_Q_ROWS, s)
    nq = s // blk
    hps = FOX_HEADS_PER_STEP
    w = hps * HEAD_DIM
    assert s % blk == 0 and blk % 2 == 0 and heads % hps == 0
    return pl.pallas_call(
        functools.partial(_fox_kernel, bq=blk, bk=blk // 2, hps=hps),
        grid=(b, heads // hps, nq),
        in_specs=[
            pl.BlockSpec((w, blk), lambda bi, h, qi: (h, bi * nq + qi)),
            pl.BlockSpec((None, s, w), lambda bi, h, qi: (bi, 0, h)),
            pl.BlockSpec((w, s), lambda bi, h, qi: (heads // hps + h, bi)),
            pl.BlockSpec((None, hps, 1, s), lambda bi, h, qi: (bi, h, 0, 0)),
            pl.BlockSpec((1, w), lambda bi, h, qi: (0, h)),
        ],
        out_specs=pl.BlockSpec((None, blk, w), lambda bi, h, qi: (bi, qi, h)),
        out_shape=jax.ShapeDtypeStruct((b, s, heads * HEAD_DIM), BF16),
        scratch_shapes=[
            pltpu.VMEM((hps, s, 2 * HEAD_DIM), BF16),
            pltpu.VMEM((hps, 2 * HEAD_DIM, blk), BF16),
            pltpu.VMEM((2, hps, blk // 2, blk), F32),
            pltpu.VMEM((hps, 1, blk), F32),
            pltpu.VMEM((hps, 1, blk), F32),
            pltpu.VMEM((hps, HEAD_DIM, blk), F32),
        ],
        compiler_params=_params("parallel", "parallel", "arbitrary"),
        name="fox",
    )(qvt, k, qvt, c_row, gain)


def _ret_kernel(h_ref, w_ref, cos_ref, sin_ref, gain_ref, o_ref, p_sc, state_sc,
                *, rows, chunk, heads):
    n = pl.program_id(1)
    d = HEAD_DIM
    w = heads * d

    @pl.when(n == 0)
    def _():
        state_sc[...] = jnp.zeros_like(state_sc)

    p_sc[...] = _dot(h_ref[...], w_ref[...])
    t = lax.broadcasted_iota(jnp.int32, (chunk, 1), 0).astype(F32)
    r = lax.broadcasted_iota(jnp.int32, (chunk, chunk), 0)
    c = lax.broadcasted_iota(jnp.int32, (chunk, chunk), 1)
    diff = (r - c).astype(F32)
    for hh in range(heads):
        lg = float(np.log1p(-np.exp2(-(RET_GAMMA_BASE + hh))))
        lo, hi = hh * d, (hh + 1) * d
        xi = jnp.exp((t + 1.0) * lg)
        zeta = jnp.exp((chunk - 1.0 - t) * lg)
        decay = jnp.where(diff >= 0.0, jnp.exp(jnp.maximum(diff, 0.0) * lg), 0.0)
        for ci in range(rows // chunk):
            rs = slice(ci * chunk, (ci + 1) * chunk)
            cos = cos_ref[rs, :]
            sin = sin_ref[rs, :]

            def rope(x):
                return x * cos + pltpu.roll(x, d // 2, 1) * sin

            q = rope(p_sc[rs, lo:hi])
            k = rope(p_sc[rs, w + lo:w + hi]) * (d ** -0.5)
            v = p_sc[rs, 2 * w + lo:2 * w + hi].astype(BF16)
            gate = p_sc[rs, 3 * w + lo:3 * w + hi]
            qb = q.astype(BF16)
            scores = _dot_nt(qb, k.astype(BF16)) * decay
            state = state_sc[hh]
            o = _dot(scores.astype(BF16), v) + _dot(qb, state.astype(BF16)) * xi
            kz_t = jnp.transpose(k * zeta).astype(BF16)
            state_sc[hh] = state * float(np.exp(chunk * lg)) + _dot(kz_t, v)
            y = _rms(o, gain_ref[:, lo:hi]) * (gate * jax.nn.sigmoid(gate))
            o_ref[rs, lo:hi] = y.astype(o_ref.dtype)


def _retention(h, w_ret, layer, cos2, sin2, gain, heads, b, s):
    t, dm = h.shape
    rows = min(RET_ROWS, s)
    chunk = min(RET_CHUNK, rows)
    nc = s // rows
    w = heads * HEAD_DIM
    assert s % rows == 0 and rows % chunk == 0 and t == b * s
    return pl.pallas_call(
        functools.partial(_ret_kernel, rows=rows, chunk=chunk, heads=heads),
        grid=(b, nc),
        in_specs=[
            pl.BlockSpec((rows, dm), lambda bi, n: (bi * nc + n, 0)),
            pl.BlockSpec((None, dm, 4 * w), lambda bi, n: (layer, 0, 0)),
            pl.BlockSpec((rows, HEAD_DIM), lambda bi, n: (n, 0)),
            pl.BlockSpec((rows, HEAD_DIM), lambda bi, n: (n, 0)),
            pl.BlockSpec((1, w), lambda bi, n: (0, 0)),
        ],
        out_specs=pl.BlockSpec((rows, w), lambda bi, n: (bi * nc + n, 0)),
        out_shape=jax.ShapeDtypeStruct((t, w), BF16),
        scratch_shapes=[
            pltpu.VMEM((rows, 4 * w), F32),
            pltpu.VMEM((heads, HEAD_DIM, HEAD_DIM), F32),
        ],
        compiler_params=_params("parallel", "arbitrary"),
        name="retention",
    )(h, w_ret, cos2, sin2, gain)


def _gelu(x):
    return 0.5 * x * (1.0 + jnp.tanh(np.sqrt(2.0 / np.pi).astype(np.float32) * (x + 0.044715 * (x * x * x))))


def _gmlp_kernel(h_ref, w_ref, lng_ref, lnb_ref, ws_ref, bs_ref, gain_ref, o_ref, uv_ref,
                 *, groups, rows):
    blk = CHUNK_GMLP
    pair = 2 * HEAD_DIM
    for g in range(groups):
        uv_ref[:, g * pair:(g + 1) * pair] = _dot(h_ref[...], w_ref[:, g * pair:(g + 1) * pair])
    r = lax.broadcasted_iota(jnp.int32, (blk, blk), 0)
    c = lax.broadcasted_iota(jnp.int32, (blk, blk), 1)
    for g in range(groups):
        lo, hi = g * HEAD_DIM, (g + 1) * HEAD_DIM
        u_lo, v_lo = g * pair, g * pair + HEAD_DIM
        wm = jnp.where(r >= c, ws_ref[g], 0.0).astype(BF16)
        bias = bs_ref[g]
        v = _gelu(uv_ref[:, v_lo:v_lo + HEAD_DIM])
        mu = jnp.mean(v, axis=-1, keepdims=True)
        var = jnp.mean(jnp.square(v - mu), axis=-1, keepdims=True)
        v = ((v - mu) * lax.rsqrt(var + EPS) * lng_ref[:, lo:hi] + lnb_ref[:, lo:hi]).astype(BF16)
        v_wide = jnp.concatenate([v[ci * blk:(ci + 1) * blk] for ci in range(rows // blk)], axis=1)
        mixed = _dot(wm, v_wide) + bias
        for ci in range(rows // blk):
            rs = slice(ci * blk, (ci + 1) * blk)
            y = _gelu(uv_ref[rs, u_lo:u_lo + HEAD_DIM]) * mixed[:, ci * HEAD_DIM:(ci + 1) * HEAD_DIM]
            o_ref[rs, lo:hi] = _rms(y, gain_ref[:, lo:hi]).astype(o_ref.dtype)


def _gmlp(h, w_gm, layer, ln_g, ln_b, w_s, b_s, gain, groups):
    t, dm = h.shape
    w = groups * HEAD_DIM
    rows = min(GMLP_ROWS, t)
    assert t % rows == 0 and rows % CHUNK_GMLP == 0
    return pl.pallas_call(
        functools.partial(_gmlp_kernel, groups=groups, rows=rows),
        grid=(t // rows,),
        in_specs=[
            pl.BlockSpec((rows, dm), lambda i: (i, 0)),
            pl.BlockSpec((None, dm, 2 * w), lambda i: (layer, 0, 0)),
            pl.BlockSpec((1, w), lambda i: (0, 0)),
            pl.BlockSpec((1, w), lambda i: (0, 0)),
            pl.BlockSpec((groups, CHUNK_GMLP, CHUNK_GMLP), lambda i: (0, 0, 0)),
            pl.BlockSpec((groups, CHUNK_GMLP, 1), lambda i: (0, 0, 0)),
            pl.BlockSpec((1, w), lambda i: (0, 0)),
        ],
        out_specs=pl.BlockSpec((rows, w), lambda i: (i, 0)),
        out_shape=jax.ShapeDtypeStruct((t, w), BF16),
        scratch_shapes=[pltpu.VMEM((rows, 2 * w), F32)],
        compiler_params=_params("parallel"),
        name="gmlp",
    )(h, w_gm, ln_g, ln_b, w_s, b_s, gain)


def _outproj_kernel(x_ref, ya_ref, yb_ref, yc_ref, w_ref, o_ref):
    wa, wb = ya_ref.shape[1], yb_ref.shape[1]
    acc = _dot(ya_ref[...], w_ref[0:wa, :])
    acc += _dot(yb_ref[...], w_ref[wa:wa + wb, :])
    acc += _dot(yc_ref[...], w_ref[wa + wb:, :])
    o_ref[...] = x_ref[...] + acc


def _outproj(x, ya, yb, yc, w, layer):
    t, d = x.shape
    k = w.shape[-2]
    bm, bn = min(MM_ROWS, t), min(MM_COLS, d)
    assert t % bm == 0 and d % bn == 0
    return pl.pallas_call(
        _outproj_kernel,
        grid=(t // bm, d // bn),
        in_specs=[
            pl.BlockSpec((bm, bn), lambda i, j: (i, j)),
            pl.BlockSpec((bm, ya.shape[1]), lambda i, j: (i, 0)),
            pl.BlockSpec((bm, yb.shape[1]), lambda i, j: (i, 0)),
            pl.BlockSpec((bm, yc.shape[1]), lambda i, j: (i, 0)),
            pl.BlockSpec((None, k, bn), lambda i, j: (layer, 0, j)),
        ],
        out_specs=pl.BlockSpec((bm, bn), lambda i, j: (i, j)),
        out_shape=jax.ShapeDtypeStruct((t, d), F32),
        compiler_params=_params("parallel", "arbitrary"),
        name="outproj",
    )(x, ya, yb, yc, w)


def kernel(x, ffn1_norm, ffn1_w_gate, ffn1_w_up, ffn1_w_down, mix_norm, w_in, fox_b_f,
           gmlp_ln_g, gmlp_ln_b, gmlp_w_s, gmlp_b_s, out_norm, w_out, ffn2_norm,
           ffn2_w_gate, ffn2_w_up, ffn2_w_down, final_norm):
    b, s, d = x.shape
    depth = w_in.shape[0]
    t = b * s
    n_heads = d // HEAD_DIM
    fox_h, ret_h = n_heads // 2, n_heads // 4
    gm_g = n_heads - fox_h - ret_h
    fox_w, ret_w, gm_w = fox_h * HEAD_DIM, ret_h * HEAD_DIM, gm_g * HEAD_DIM

    o_fz = 3 * fox_w
    o_ret = o_fz + fox_h
    o_gm = o_ret + 4 * ret_w
    w_qv_t = jnp.transpose(
        jnp.concatenate([w_in[:, :, :fox_w], w_in[:, :, 2 * fox_w:o_fz]], axis=2), (0, 2, 1)
    ).astype(BF16)
    w_kz = jnp.concatenate(
        [w_in[:, :, fox_w:2 * fox_w],
         jnp.pad(w_in[:, :, o_fz:o_ret], ((0, 0), (0, 0), (0, HEAD_DIM - fox_h)))], axis=2
    ).astype(BF16)
    w_ret = w_in[:, :, o_ret:o_gm].astype(BF16)
    w_gm = jnp.concatenate(
        [w_in[:, :, o_gm + side * gm_w + g * HEAD_DIM:o_gm + side * gm_w + (g + 1) * HEAD_DIM]
         for g in range(gm_g) for side in (0, 1)], axis=2).astype(BF16)
    w_o = w_out.astype(BF16)
    f1 = [w.astype(BF16) for w in (ffn1_w_gate, ffn1_w_up, ffn1_w_down)]
    f2 = [w.astype(BF16) for w in (ffn2_w_gate, ffn2_w_up, ffn2_w_down)]
    fz_bias = jnp.pad(fox_b_f, ((0, 0), (0, HEAD_DIM - fox_h)))

    half = HEAD_DIM // 2
    pos = jnp.arange(s, dtype=F32)
    inv_freq = ROPE_BASE ** (-jnp.arange(half, dtype=F32) / half)
    ang = pos[:, None] * inv_freq[None, :]
    cos2 = jnp.concatenate([jnp.cos(ang), jnp.cos(ang)], axis=-1)
    sin2 = jnp.concatenate([-jnp.sin(ang), jnp.sin(ang)], axis=-1)

    xf = x.reshape(t, d)
    for l in range(depth):
        x1, h = _ffn(xf, ffn1_norm[l][None], *f1, mix_norm[l][None], l, emit_h=True)
        qvt = _proj_t(h, w_qv_t, l, BF16, scale=HEAD_DIM ** -0.5 * LOG2E, scaled_rows=fox_w)
        kf, fz = _proj_kz(h, w_kz, l, fox_w)

        c = _forget_cumsum(fz.reshape(b, s, HEAD_DIM), fz_bias[l][None])
        c = jnp.transpose(c[:, :, :fox_h], (0, 2, 1))
        gains = out_norm[l][None]
        ya = _fox(qvt, kf.reshape(b, s, fox_w), c[:, :, None, :], gains[:, :fox_w], fox_h)
        yb = _retention(h, w_ret, l, cos2, sin2, gains[:, fox_w:fox_w + ret_w], ret_h, b, s)
        yc = _gmlp(h, w_gm, l, gmlp_ln_g[l][None], gmlp_ln_b[l][None], gmlp_w_s[l],
                   gmlp_b_s[l][..., None], gains[:, fox_w + ret_w:], gm_g)
        x2 = _outproj(x1, ya.reshape(t, fox_w), yb, yc, w_o, l)

        (xf,) = _ffn(x2, ffn2_norm[l][None], *f2, final_norm[None], l, norm_out=l == depth - 1)
    return xf.reshape(b, s, d)
```

```python
import functools

import jax
import jax.numpy as jnp
import numpy as np
from jax import lax
from jax.experimental import pallas as pl
from jax.experimental.pallas import tpu as pltpu

F32 = jnp.float32
BF16 = jnp.bfloat16

HEAD_DIM = 128
CHUNK_GMLP = 128
ROPE_BASE = 10000.0
RET_GAMMA_BASE = 5.0
EPS = 1e-6

V7X_VMEM_BYTES = 64 * 1024 * 1024
VMEM_LIMIT = V7X_VMEM_BYTES - 1 * 1024 * 1024

FFN_ROWS = 1024
FFN_ROWS_WITH_H = 1024
FFN_NORM_ROWS = 128
FFN_COLS = 512
MM_ROWS = 1024
MM_COLS = 1024
OUT_ROWS = 2048
OUT_COLS = 512
FOX_Q_ROWS = 1024
FOX_HEADS_PER_STEP = 4
RET_ROWS = 512
RET_CHUNK = 256
GMLP_ROWS = 512


def _params(*sem):
    return pltpu.CompilerParams(dimension_semantics=sem, vmem_limit_bytes=VMEM_LIMIT)


def _rms(x, gain):
    return x * lax.rsqrt(jnp.mean(x * x, axis=-1, keepdims=True) + EPS) * gain


def _dot(a, b):
    return jnp.dot(a, b, preferred_element_type=F32)


def _dot_nt(a, b):
    return lax.dot_general(a, b, (((1,), (1,)), ((), ())), preferred_element_type=F32)


def _ffn_kernel(x_ref, g_ref, wg_ref, wu_ref, wd_ref, g2_ref, *rest, emit_h, norm_out):
    acc_ref = rest[0]
    xn_sc = rest[-1]
    j = pl.program_id(1)
    n_strips = x_ref.shape[0] // FFN_NORM_ROWS

    def strip(r):
        return pl.ds(pl.multiple_of(r * FFN_NORM_ROWS, FFN_NORM_ROWS), FFN_NORM_ROWS)

    @pl.when(j == 0)
    def _():
        def body(r, carry):
            xn_sc[strip(r), :] = _rms(x_ref[strip(r), :], g_ref[...]).astype(BF16)
            acc_ref[strip(r), :] = jnp.zeros((FFN_NORM_ROWS, acc_ref.shape[1]), F32)
            return carry

        lax.fori_loop(0, n_strips, body, 0)

    xn = xn_sc[...]
    a = _dot(xn, wg_ref[...])
    b = _dot(xn, wu_ref[...])
    hmid = (a * jax.nn.sigmoid(a) * b).astype(BF16)
    acc_ref[...] += _dot(hmid, wd_ref[...])

    @pl.when(j == pl.num_programs(1) - 1)
    def _():
        def body(r, carry):
            y = x_ref[strip(r), :] + 0.5 * acc_ref[strip(r), :]
            acc_ref[strip(r), :] = _rms(y, g2_ref[...]) if norm_out else y
            if emit_h:
                rest[1][strip(r), :] = _rms(y, g2_ref[...]).astype(BF16)
            return carry

        lax.fori_loop(0, n_strips, body, 0)


def _ffn(x, gain, wg, wu, wd, gain2, layer, *, emit_h=False, norm_out=False):
    t, d = x.shape
    f = wg.shape[-1]
    bm, tf = min(FFN_ROWS_WITH_H if emit_h else FFN_ROWS, t), FFN_COLS
    assert t % bm == 0 and f % tf == 0
    out_shape = [jax.ShapeDtypeStruct((t, d), F32)]
    if emit_h:
        out_shape.append(jax.ShapeDtypeStruct((t, d), BF16))
    out_specs = [pl.BlockSpec((bm, d), lambda i, j: (i, 0)) for _ in out_shape]
    return pl.pallas_call(
        functools.partial(_ffn_kernel, emit_h=emit_h, norm_out=norm_out),
        grid=(t // bm, f // tf),
        in_specs=[
            pl.BlockSpec((bm, d), lambda i, j: (i, 0)),
            pl.BlockSpec((1, d), lambda i, j: (0, 0)),
            pl.BlockSpec((None, d, tf), lambda i, j: (layer, 0, j)),
            pl.BlockSpec((None, d, tf), lambda i, j: (layer, 0, j)),
            pl.BlockSpec((None, tf, d), lambda i, j: (layer, j, 0)),
            pl.BlockSpec((1, d), lambda i, j: (0, 0)),
        ],
        out_specs=out_specs,
        out_shape=out_shape,
        scratch_shapes=[pltpu.VMEM((bm, d), BF16)],
        compiler_params=_params("parallel", "arbitrary"),
        name="ffn",
    )(x, gain, wg, wu, wd, gain2)


def _proj_t_kernel(h_ref, w_ref, o_ref, *, scale, scaled_rows, bn):
    y = _dot(h_ref[...], w_ref[...])
    if scaled_rows:
        j = pl.program_id(1)
        y = y * jnp.where(j * bn < scaled_rows, scale, 1.0).astype(F32)
    o_ref[...] = jnp.transpose(y).astype(o_ref.dtype)


def _proj_t(h, w, layer, out_dtype, *, scale=1.0, scaled_rows=0):
    t, d = h.shape
    n = w.shape[-1]
    bm, bn = min(MM_ROWS, t), min(MM_COLS, n)
    assert t % bm == 0 and n % bn == 0 and scaled_rows % bn == 0
    return pl.pallas_call(
        functools.partial(_proj_t_kernel, scale=scale, scaled_rows=scaled_rows, bn=bn),
        grid=(t // bm, n // bn),
        in_specs=[
            pl.BlockSpec((bm, d), lambda i, j: (i, 0)),
            pl.BlockSpec((None, d, bn), lambda i, j: (layer, 0, j)),
        ],
        out_specs=pl.BlockSpec((bn, bm), lambda i, j: (j, i)),
        out_shape=jax.ShapeDtypeStruct((n, t), out_dtype),
        compiler_params=_params("parallel", "arbitrary"),
        name="proj_t",
    )(h, w)


def _proj_kernel(h_ref, w_ref, k_ref, z_ref):
    y = _dot(h_ref[...], w_ref[...])
    nk = k_ref.shape[1]
    k_ref[...] = y[:, :nk].astype(k_ref.dtype)
    z_ref[...] = y[:, nk:]


def _proj_kz(h, w, layer, nk):
    t, d = h.shape
    n = w.shape[-1]
    bm = min(MM_ROWS, t)
    assert t % bm == 0
    return pl.pallas_call(
        _proj_kernel,
        grid=(t // bm,),
        in_specs=[
            pl.BlockSpec((bm, d), lambda i: (i, 0)),
            pl.BlockSpec((None, d, n), lambda i: (layer, 0, 0)),
        ],
        out_specs=[
            pl.BlockSpec((bm, nk), lambda i: (i, 0)),
            pl.BlockSpec((bm, n - nk), lambda i: (i, 0)),
        ],
        out_shape=[
            jax.ShapeDtypeStruct((t, nk), BF16),
            jax.ShapeDtypeStruct((t, n - nk), F32),
        ],
        compiler_params=_params("parallel"),
        name="proj_kz",
    )(h, w)


def _cumsum_kernel(z_ref, b_ref, o_ref):
    s = z_ref.shape[0]
    blk = 128
    r = lax.broadcasted_iota(jnp.int32, (blk, blk), 0)
    c = lax.broadcasted_iota(jnp.int32, (blk, blk), 1)
    tril = jnp.where(r >= c, 1.0, 0.0).astype(F32)
    carry = jnp.zeros((1, z_ref.shape[1]), F32)
    for i in range(s // blk):
        z = z_ref[i * blk:(i + 1) * blk, :] + b_ref[...]
        log_f = jnp.minimum(z, 0.0) - jnp.log1p(jnp.exp(-jnp.abs(z)))
        cs = jnp.dot(tril, log_f, precision=lax.Precision.HIGHEST,
                     preferred_element_type=F32) + carry
        o_ref[i * blk:(i + 1) * blk, :] = cs
        carry = cs[blk - 1:blk, :]


def _forget_cumsum(z, bias):
    b, s, w = z.shape
    return pl.pallas_call(
        _cumsum_kernel,
        grid=(b,),
        in_specs=[
            pl.BlockSpec((None, s, w), lambda i: (i, 0, 0)),
            pl.BlockSpec((1, w), lambda i: (0, 0)),
        ],
        out_specs=pl.BlockSpec((None, s, w), lambda i: (i, 0, 0)),
        out_shape=jax.ShapeDtypeStruct((b, s, w), F32),
        compiler_params=_params("parallel"),
        name="forget_cumsum",
    )(z, bias)


LOG2E = float(np.log2(np.e))
BIAS_TERMS = 3
AUG_ROWS = 16


def _split_bf16(x):
    hi = x.astype(BF16).astype(F32)
    r = x - hi
    mid = r.astype(BF16).astype(F32)
    lo = (r - mid).astype(BF16).astype(F32)
    return hi, mid, lo


def _bias_rows(c, idx, sign):
    hi, mid, lo = _split_bf16(c)
    first = 0 if sign > 0 else BIAS_TERMS
    ones_first = BIAS_TERMS if sign > 0 else 0
    out = jnp.where((idx >= ones_first) & (idx < ones_first + BIAS_TERMS), 1.0, 0.0)
    out = jnp.where(idx == first, sign * hi, out)
    out = jnp.where(idx == first + 1, sign * mid, out)
    out = jnp.where(idx == first + 2, sign * lo, out)
    return out.astype(BF16)


def _fox_kernel(qt_ref, k_ref, vt_ref, c_ref, g_ref, o_ref, kaug_sc, qaug_sc, s_sc, m_sc, l_sc, acc_sc,
                *, bq, bk, hps):
    qi = pl.program_id(2)
    s_len = k_ref.shape[0]
    d = HEAD_DIM

    @pl.when(qi == 0)
    def _():
        lane = lax.broadcasted_iota(jnp.int32, (d, d), 1)
        for hh in range(hps):
            kaug_sc[hh, :, 0:d] = k_ref[:, hh * d:(hh + 1) * d]
            for i in range(s_len // d):
                c_row = c_ref[hh, :, i * d:(i + 1) * d] * LOG2E
                c_col = jnp.transpose(jnp.broadcast_to(c_row, (d, d)))
                kaug_sc[hh, i * d:(i + 1) * d, d:2 * d] = _bias_rows(c_col, lane, -1.0)

    q0 = pl.multiple_of(qi * bq, bq)
    sub = lax.broadcasted_iota(jnp.int32, (AUG_ROWS, bq), 0)
    for hh in range(hps):
        cq = jnp.broadcast_to(c_ref[hh, :, pl.ds(q0, bq)] * LOG2E, (AUG_ROWS, bq))
        qaug_sc[hh, 0:d, :] = qt_ref[hh * d:(hh + 1) * d, :]
        qaug_sc[hh, d:d + AUG_ROWS, :] = _bias_rows(cq, sub, 1.0)
        qaug_sc[hh, d + AUG_ROWS:, :] = jnp.zeros((d - AUG_ROWS, bq), BF16)
    m_sc[...] = jnp.full_like(m_sc, -jnp.inf)
    l_sc[...] = jnp.zeros_like(l_sc)
    acc_sc[...] = jnp.zeros_like(acc_sc)

    def scores(ki, slot, lo, width):
        k0 = pl.multiple_of(ki * bk, bk)
        for hh in range(hps):
            s_sc[slot, hh, :, lo:lo + width] = _dot(kaug_sc[hh, pl.ds(k0, bk), :],
                                                    qaug_sc[hh, :, lo:lo + width])

    def absorb(ki, slot, lo, width, masked):
        cols = slice(lo, lo + width)
        k0 = pl.multiple_of(ki * bk, bk)
        for hh in range(hps):
            s = s_sc[slot, hh, :, cols]
            if masked:
                r = lax.broadcasted_iota(jnp.int32, (bk, width), 0)
                c = lax.broadcasted_iota(jnp.int32, (bk, width), 1)
                s = jnp.where(r - c <= q0 + lo - k0, s, -jnp.inf)
            m_prev = m_sc[hh, :, cols]
            m_new = jnp.maximum(m_prev, jnp.max(s, axis=0, keepdims=True))
            alpha = jnp.exp2(m_prev - m_new)
            p = jnp.exp2(s - m_new)
            l_sc[hh, :, cols] = alpha * l_sc[hh, :, cols] + jnp.sum(p, axis=0, keepdims=True)
            pv = _dot(vt_ref[hh * d:(hh + 1) * d, pl.ds(k0, bk)], p.astype(BF16))
            acc_sc[hh, :, cols] = alpha * acc_sc[hh, :, cols] + pv
            m_sc[hh, :, cols] = m_new

    scores(0, 0, 0, bq)

    def body(pair, carry):
        ki = 2 * pair
        scores(ki + 1, 1, 0, bq)
        absorb(ki, 0, 0, bq, False)
        scores(ki + 2, 0, 0, bq)
        absorb(ki + 1, 1, 0, bq, False)
        return carry

    lax.fori_loop(0, qi, body, 0)
    scores(2 * qi + 1, 1, bk, bk)
    absorb(2 * qi, 0, 0, bq, True)
    absorb(2 * qi + 1, 1, bk, bk, True)
    for hh in range(hps):
        y_t = acc_sc[hh] * (1.0 / l_sc[hh])
        y_t = y_t * lax.rsqrt(jnp.mean(y_t * y_t, axis=0, keepdims=True) + EPS)
        o_ref[:, hh * d:(hh + 1) * d] = (
            jnp.transpose(y_t) * g_ref[:, hh * d:(hh + 1) * d]).astype(o_ref.dtype)


def _fox(qvt, k, c_row, gain, heads):
    b, s, _ = k.shape
    blk = min(FOX_Q_ROWS, s)
    nq = s // blk
    hps = FOX_HEADS_PER_STEP
    w = hps * HEAD_DIM
    assert s % blk == 0 and blk % 2 == 0 and heads % hps == 0
    return pl.pallas_call(
        functools.partial(_fox_kernel, bq=blk, bk=blk // 2, hps=hps),
        grid=(b, heads // hps, nq),
        in_specs=[
            pl.BlockSpec((w, blk), lambda bi, h, qi: (h, bi * nq + qi)),
            pl.BlockSpec((None, s, w), lambda bi, h, qi: (bi, 0, h)),
            pl.BlockSpec((w, s), lambda bi, h, qi: (heads // hps + h, bi)),
            pl.BlockSpec((None, hps, 1, s), lambda bi, h, qi: (bi, h, 0, 0)),
            pl.BlockSpec((1, w), lambda bi, h, qi: (0, h)),
        ],
        out_specs=pl.BlockSpec((None, blk, w), lambda bi, h, qi: (bi, qi, h)),
        out_shape=jax.ShapeDtypeStruct((b, s, heads * HEAD_DIM), BF16),
        scratch_shapes=[
            pltpu.VMEM((hps, s, 2 * HEAD_DIM), BF16),
            pltpu.VMEM((hps, 2 * HEAD_DIM, blk), BF16),
            pltpu.VMEM((2, hps, blk // 2, blk), F32),
            pltpu.VMEM((hps, 1, blk), F32),
            pltpu.VMEM((hps, 1, blk), F32),
            pltpu.VMEM((hps, HEAD_DIM, blk), F32),
        ],
        compiler_params=_params("parallel", "parallel", "arbitrary"),
        name="fox",
    )(qvt, k, qvt, c_row, gain)


def _ret_kernel(h_ref, w_ref, cos_ref, sin_ref, gain_ref, o_ref, p_sc, state_sc,
                *, rows, chunk, heads):
    n = pl.program_id(1)
    d = HEAD_DIM
    w = heads * d

    @pl.when(n == 0)
    def _():
        state_sc[...] = jnp.zeros_like(state_sc)

    p_sc[...] = _dot(h_ref[...], w_ref[...])
    t = lax.broadcasted_iota(jnp.int32, (chunk, 1), 0).astype(F32)
    r = lax.broadcasted_iota(jnp.int32, (chunk, chunk), 0)
    c = lax.broadcasted_iota(jnp.int32, (chunk, chunk), 1)
    diff = (r - c).astype(F32)
    for hh in range(heads):
        lg = float(np.log1p(-np.exp2(-(RET_GAMMA_BASE + hh))))
        lo, hi = hh * d, (hh + 1) * d
        xi = jnp.exp((t + 1.0) * lg)
        zeta = jnp.exp((chunk - 1.0 - t) * lg)
        decay = jnp.where(diff >= 0.0, jnp.exp(jnp.maximum(diff, 0.0) * lg), 0.0)
        for ci in range(rows // chunk):
            rs = slice(ci * chunk, (ci + 1) * chunk)
            cos = cos_ref[rs, :]
            sin = sin_ref[rs, :]

            def rope(x):
                return x * cos + pltpu.roll(x, d // 2, 1) * sin

            q = rope(p_sc[rs, lo:hi])
            k = rope(p_sc[rs, w + lo:w + hi]) * (d ** -0.5)
            v = p_sc[rs, 2 * w + lo:2 * w + hi].astype(BF16)
            gate = p_sc[rs, 3 * w + lo:3 * w + hi]
            qb = q.astype(BF16)
            scores = _dot_nt(qb, k.astype(BF16)) * decay
            state = state_sc[hh]
            o = _dot(scores.astype(BF16), v) + _dot(qb, state.astype(BF16)) * xi
            kz_t = jnp.transpose(k * zeta).astype(BF16)
            state_sc[hh] = state * float(np.exp(chunk * lg)) + _dot(kz_t, v)
            y = _rms(o, gain_ref[:, lo:hi]) * (gate * jax.nn.sigmoid(gate))
            o_ref[rs, lo:hi] = y.astype(o_ref.dtype)


def _retention(h, w_ret, layer, cos2, sin2, gain, heads, b, s):
    t, dm = h.shape
    rows = min(RET_ROWS, s)
    chunk = min(RET_CHUNK, rows)
    nc = s // rows
    w = heads * HEAD_DIM
    assert s % rows == 0 and rows % chunk == 0 and t == b * s
    return pl.pallas_call(
        functools.partial(_ret_kernel, rows=rows, chunk=chunk, heads=heads),
        grid=(b, nc),
        in_specs=[
            pl.BlockSpec((rows, dm), lambda bi, n: (bi * nc + n, 0)),
            pl.BlockSpec((None, dm, 4 * w), lambda bi, n: (layer, 0, 0)),
            pl.BlockSpec((rows, HEAD_DIM), lambda bi, n: (n, 0)),
            pl.BlockSpec((rows, HEAD_DIM), lambda bi, n: (n, 0)),
            pl.BlockSpec((1, w), lambda bi, n: (0, 0)),
        ],
        out_specs=pl.BlockSpec((rows, w), lambda bi, n: (bi * nc + n, 0)),
        out_shape=jax.ShapeDtypeStruct((t, w), BF16),
        scratch_shapes=[
            pltpu.VMEM((rows, 4 * w), F32),
            pltpu.VMEM((heads, HEAD_DIM, HEAD_DIM), F32),
        ],
        compiler_params=_params("parallel", "arbitrary"),
        name="retention",
    )(h, w_ret, cos2, sin2, gain)


def _gelu(x):
    return 0.5 * x * (1.0 + jnp.tanh(np.sqrt(2.0 / np.pi).astype(np.float32) * (x + 0.044715 * (x * x * x))))


def _gmlp_kernel(h_ref, w_ref, lng_ref, lnb_ref, ws_ref, bs_ref, gain_ref, o_ref, uv_ref,
                 *, groups, rows):
    blk = CHUNK_GMLP
    pair = 2 * HEAD_DIM
    for g in range(groups):
        uv_ref[:, g * pair:(g + 1) * pair] = _dot(h_ref[...], w_ref[:, g * pair:(g + 1) * pair])
    r = lax.broadcasted_iota(jnp.int32, (blk, blk), 0)
    c = lax.broadcasted_iota(jnp.int32, (blk, blk), 1)
    for g in range(groups):
        lo, hi = g * HEAD_DIM, (g + 1) * HEAD_DIM
        u_lo, v_lo = g * pair, g * pair + HEAD_DIM
        wm = jnp.where(r >= c, ws_ref[g], 0.0).astype(BF16)
        bias = bs_ref[g]
        v = _gelu(uv_ref[:, v_lo:v_lo + HEAD_DIM])
        mu = jnp.mean(v, axis=-1, keepdims=True)
        var = jnp.mean(jnp.square(v - mu), axis=-1, keepdims=True)
        v = ((v - mu) * lax.rsqrt(var + EPS) * lng_ref[:, lo:hi] + lnb_ref[:, lo:hi]).astype(BF16)
        v_wide = jnp.concatenate([v[ci * blk:(ci + 1) * blk] for ci in range(rows // blk)], axis=1)
        mixed = _dot(wm, v_wide) + bias
        for ci in range(rows // blk):
            rs = slice(ci * blk, (ci + 1) * blk)
            y = _gelu(uv_ref[rs, u_lo:u_lo + HEAD_DIM]) * mixed[:, ci * HEAD_DIM:(ci + 1) * HEAD_DIM]
            o_ref[rs, lo:hi] = _rms(y, gain_ref[:, lo:hi]).astype(o_ref.dtype)


def _gmlp(h, w_gm, layer, ln_g, ln_b, w_s, b_s, gain, groups):
    t, dm = h.shape
    w = groups * HEAD_DIM
    rows = min(GMLP_ROWS, t)
    assert t % rows == 0 and rows % CHUNK_GMLP == 0
    return pl.pallas_call(
        functools.partial(_gmlp_kernel, groups=groups, rows=rows),
        grid=(t // rows,),
        in_specs=[
            pl.BlockSpec((rows, dm), lambda i: (i, 0)),
            pl.BlockSpec((None, dm, 2 * w), lambda i: (layer, 0, 0)),
            pl.BlockSpec((1, w), lambda i: (0, 0)),
            pl.BlockSpec((1, w), lambda i: (0, 0)),
            pl.BlockSpec((groups, CHUNK_GMLP, CHUNK_GMLP), lambda i: (0, 0, 0)),
            pl.BlockSpec((groups, CHUNK_GMLP, 1), lambda i: (0, 0, 0)),
            pl.BlockSpec((1, w), lambda i: (0, 0)),
        ],
        out_specs=pl.BlockSpec((rows, w), lambda i: (i, 0)),
        out_shape=jax.ShapeDtypeStruct((t, w), BF16),
        scratch_shapes=[pltpu.VMEM((rows, 2 * w), F32)],
        compiler_params=_params("parallel"),
        name="gmlp",
    )(h, w_gm, ln_g, ln_b, w_s, b_s, gain)


def _outproj_kernel(x_ref, ya_ref, yb_ref, yc_ref, w_ref, o_ref):
    wa, wb = ya_ref.shape[1], yb_ref.shape[1]
    acc = _dot(ya_ref[...], w_ref[0:wa, :])
    acc += _dot(yb_ref[...], w_ref[wa:wa + wb, :])
    acc += _dot(yc_ref[...], w_ref[wa + wb:, :])
    o_ref[...] = x_ref[...] + acc


def _outproj(x, ya, yb, yc, w, layer):
    t, d = x.shape
    k = w.shape[-2]
    bm, bn = min(OUT_ROWS, t), min(OUT_COLS, d)
    assert t % bm == 0 and d % bn == 0
    return pl.pallas_call(
        _outproj_kernel,
        grid=(t // bm, d // bn),
        in_specs=[
            pl.BlockSpec((bm, bn), lambda i, j: (i, j)),
            pl.BlockSpec((bm, ya.shape[1]), lambda i, j: (i, 0)),
            pl.BlockSpec((bm, yb.shape[1]), lambda i, j: (i, 0)),
            pl.BlockSpec((bm, yc.shape[1]), lambda i, j: (i, 0)),
            pl.BlockSpec((None, k, bn), lambda i, j: (layer, 0, j)),
        ],
        out_specs=pl.BlockSpec((bm, bn), lambda i, j: (i, j)),
        out_shape=jax.ShapeDtypeStruct((t, d), F32),
        compiler_params=_params("parallel", "arbitrary"),
        name="outproj",
    )(x, ya, yb, yc, w)


def kernel(x, ffn1_norm, ffn1_w_gate, ffn1_w_up, ffn1_w_down, mix_norm, w_in, fox_b_f,
           gmlp_ln_g, gmlp_ln_b, gmlp_w_s, gmlp_b_s, out_norm, w_out, ffn2_norm,
           ffn2_w_gate, ffn2_w_up, ffn2_w_down, final_norm):
    b, s, d = x.shape
    depth = w_in.shape[0]
    t = b * s
    n_heads = d // HEAD_DIM
    fox_h, ret_h = n_heads // 2, n_heads // 4
    gm_g = n_heads - fox_h - ret_h
    fox_w, ret_w, gm_w = fox_h * HEAD_DIM, ret_h * HEAD_DIM, gm_g * HEAD_DIM

    o_fz = 3 * fox_w
    o_ret = o_fz + fox_h
    o_gm = o_ret + 4 * ret_w
    w_qv = jnp.concatenate([w_in[:, :, :fox_w], w_in[:, :, 2 * fox_w:o_fz]], axis=2).astype(BF16)
    w_kz = jnp.concatenate(
        [w_in[:, :, fox_w:2 * fox_w],
         jnp.pad(w_in[:, :, o_fz:o_ret], ((0, 0), (0, 0), (0, HEAD_DIM - fox_h)))], axis=2
    ).astype(BF16)
    w_ret = w_in[:, :, o_ret:o_gm].astype(BF16)
    w_gm = jnp.concatenate(
        [w_in[:, :, o_gm + side * gm_w + g * HEAD_DIM:o_gm + side * gm_w + (g + 1) * HEAD_DIM]
         for g in range(gm_g) for side in (0, 1)], axis=2).astype(BF16)
    w_o = w_out.astype(BF16)
    f1 = [w.astype(BF16) for w in (ffn1_w_gate, ffn1_w_up, ffn1_w_down)]
    f2 = [w.astype(BF16) for w in (ffn2_w_gate, ffn2_w_up, ffn2_w_down)]
    fz_bias = jnp.pad(fox_b_f, ((0, 0), (0, HEAD_DIM - fox_h)))

    half = HEAD_DIM // 2
    pos = jnp.arange(s, dtype=F32)
    inv_freq = ROPE_BASE ** (-jnp.arange(half, dtype=F32) / half)
    ang = pos[:, None] * inv_freq[None, :]
    cos2 = jnp.concatenate([jnp.cos(ang), jnp.cos(ang)], axis=-1)
    sin2 = jnp.concatenate([-jnp.sin(ang), jnp.sin(ang)], axis=-1)

    xf = x.reshape(t, d)
    for l in range(depth):
        x1, h = _ffn(xf, ffn1_norm[l][None], *f1, mix_norm[l][None], l, emit_h=True)
        qvt = _proj_t(h, w_qv, l, BF16, scale=HEAD_DIM ** -0.5 * LOG2E, scaled_rows=fox_w)
        kf, fz = _proj_kz(h, w_kz, l, fox_w)

        c = _forget_cumsum(fz.reshape(b, s, HEAD_DIM), fz_bias[l][None])
        c = jnp.transpose(c[:, :, :fox_h], (0, 2, 1))
        gains = out_norm[l][None]
        ya = _fox(qvt, kf.reshape(b, s, fox_w), c[:, :, None, :], gains[:, :fox_w], fox_h)
        yb = _retention(h, w_ret, l, cos2, sin2, gains[:, fox_w:fox_w + ret_w], ret_h, b, s)
        yc = _gmlp(h, w_gm, l, gmlp_ln_g[l][None], gmlp_ln_b[l][None], gmlp_w_s[l],
                   gmlp_b_s[l][..., None], gains[:, fox_w + ret_w:], gm_g)
        x2 = _outproj(x1, ya.reshape(t, fox_w), yb, yc, w_o, l)

        (xf,) = _ffn(x2, ffn2_norm[l][None], *f2, final_norm[None], l, norm_out=l == depth - 1)
    return xf.reshape(b, s, d)
```

```python
import functools

import jax
import jax.numpy as jnp
import numpy as np
from jax import lax
from jax.experimental import pallas as pl
from jax.experimental.pallas import tpu as pltpu

F32 = jnp.float32
BF16 = jnp.bfloat16

HEAD_DIM = 128
CHUNK_GMLP = 128
ROPE_BASE = 10000.0
RET_GAMMA_BASE = 5.0
EPS = 1e-6

V7X_VMEM_BYTES = 64 * 1024 * 1024
VMEM_LIMIT = V7X_VMEM_BYTES - 1 * 1024 * 1024

FFN_ROWS = 1024
FFN_ROWS_WITH_H = 1024
FFN_NORM_ROWS = 128
FFN_COLS = 512
MM_ROWS = 1024
MM_COLS = 1024
REPACK_ROWS = 256
OUT_ROWS = 2048
OUT_COLS = 512
FOX_Q_ROWS = 1024
FOX_HEADS_PER_STEP = 4
RET_ROWS = 512
RET_CHUNK = 256
GMLP_ROWS = 512


def _params(*sem):
    return pltpu.CompilerParams(dimension_semantics=sem, vmem_limit_bytes=VMEM_LIMIT)


def _rms(x, gain):
    return x * lax.rsqrt(jnp.mean(x * x, axis=-1, keepdims=True) + EPS) * gain


def _dot(a, b):
    return jnp.dot(a, b, preferred_element_type=F32)


def _dot_nt(a, b):
    return lax.dot_general(a, b, (((1,), (1,)), ((), ())), preferred_element_type=F32)


def _ffn_kernel(x_ref, g_ref, wg_ref, wu_ref, wd_ref, g2_ref, *rest, emit_h, norm_out):
    acc_ref = rest[0]
    xn_sc = rest[-1]
    j = pl.program_id(1)
    n_strips = x_ref.shape[0] // FFN_NORM_ROWS

    def strip(r):
        return pl.ds(pl.multiple_of(r * FFN_NORM_ROWS, FFN_NORM_ROWS), FFN_NORM_ROWS)

    @pl.when(j == 0)
    def _():
        def body(r, carry):
            xn_sc[strip(r), :] = _rms(x_ref[strip(r), :], g_ref[...]).astype(BF16)
            acc_ref[strip(r), :] = jnp.zeros((FFN_NORM_ROWS, acc_ref.shape[1]), F32)
            return carry

        lax.fori_loop(0, n_strips, body, 0)

    xn = xn_sc[...]
    a = _dot(xn, wg_ref[...])
    b = _dot(xn, wu_ref[...])
    hmid = (a * jax.nn.sigmoid(a) * b).astype(BF16)
    acc_ref[...] += _dot(hmid, wd_ref[...])

    @pl.when(j == pl.num_programs(1) - 1)
    def _():
        def body(r, carry):
            y = x_ref[strip(r), :] + 0.5 * acc_ref[strip(r), :]
            acc_ref[strip(r), :] = _rms(y, g2_ref[...]) if norm_out else y
            if emit_h:
                rest[1][strip(r), :] = _rms(y, g2_ref[...]).astype(BF16)
            return carry

        lax.fori_loop(0, n_strips, body, 0)


def _ffn(x, gain, wg, wu, wd, gain2, layer, *, emit_h=False, norm_out=False):
    t, d = x.shape
    f = wg.shape[-1]
    bm, tf = min(FFN_ROWS_WITH_H if emit_h else FFN_ROWS, t), FFN_COLS
    assert t % bm == 0 and f % tf == 0
    out_shape = [jax.ShapeDtypeStruct((t, d), F32)]
    if emit_h:
        out_shape.append(jax.ShapeDtypeStruct((t, d), BF16))
    out_specs = [pl.BlockSpec((bm, d), lambda i, j: (i, 0)) for _ in out_shape]
    return pl.pallas_call(
        functools.partial(_ffn_kernel, emit_h=emit_h, norm_out=norm_out),
        grid=(t // bm, f // tf),
        in_specs=[
            pl.BlockSpec((bm, d), lambda i, j: (i, 0)),
            pl.BlockSpec((1, d), lambda i, j: (0, 0)),
            pl.BlockSpec((None, d, tf), lambda i, j: (layer, 0, j)),
            pl.BlockSpec((None, d, tf), lambda i, j: (layer, 0, j)),
            pl.BlockSpec((None, tf, d), lambda i, j: (layer, j, 0)),
            pl.BlockSpec((1, d), lambda i, j: (0, 0)),
        ],
        out_specs=out_specs,
        out_shape=out_shape,
        scratch_shapes=[pltpu.VMEM((bm, d), BF16)],
        compiler_params=_params("parallel", "arbitrary"),
        name="ffn",
    )(x, gain, wg, wu, wd, gain2)


def _proj_t_kernel(h_ref, w_ref, o_ref, *, scale, scaled_rows, bn):
    y = _dot(h_ref[...], w_ref[...])
    if scaled_rows:
        j = pl.program_id(1)
        y = y * jnp.where(j * bn < scaled_rows, scale, 1.0).astype(F32)
    o_ref[...] = jnp.transpose(y).astype(o_ref.dtype)


def _proj_t(h, w, layer, out_dtype, *, scale=1.0, scaled_rows=0):
    t, d = h.shape
    n = w.shape[-1]
    bm, bn = min(MM_ROWS, t), min(MM_COLS, n)
    assert t % bm == 0 and n % bn == 0 and scaled_rows % bn == 0
    return pl.pallas_call(
        functools.partial(_proj_t_kernel, scale=scale, scaled_rows=scaled_rows, bn=bn),
        grid=(t // bm, n // bn),
        in_specs=[
            pl.BlockSpec((bm, d), lambda i, j: (i, 0)),
            pl.BlockSpec((None, d, bn), lambda i, j: (layer, 0, j)),
        ],
        out_specs=pl.BlockSpec((bn, bm), lambda i, j: (j, i)),
        out_shape=jax.ShapeDtypeStruct((n, t), out_dtype),
        compiler_params=_params("parallel", "arbitrary"),
        name="proj_t",
    )(h, w)


def _proj_kernel(h_ref, w_ref, k_ref, z_ref):
    y = _dot(h_ref[...], w_ref[...])
    nk = k_ref.shape[1]
    k_ref[...] = y[:, :nk].astype(k_ref.dtype)
    z_ref[...] = y[:, nk:]


def _proj_kz(h, w, layer, nk):
    t, d = h.shape
    n = w.shape[-1]
    bm = min(MM_ROWS, t)
    assert t % bm == 0
    return pl.pallas_call(
        _proj_kernel,
        grid=(t // bm,),
        in_specs=[
            pl.BlockSpec((bm, d), lambda i: (i, 0)),
            pl.BlockSpec((None, d, n), lambda i: (layer, 0, 0)),
        ],
        out_specs=[
            pl.BlockSpec((bm, nk), lambda i: (i, 0)),
            pl.BlockSpec((bm, n - nk), lambda i: (i, 0)),
        ],
        out_shape=[
            jax.ShapeDtypeStruct((t, nk), BF16),
            jax.ShapeDtypeStruct((t, n - nk), F32),
        ],
        compiler_params=_params("parallel"),
        name="proj_kz",
    )(h, w)


def _cumsum_kernel(z_ref, b_ref, o_ref):
    s = z_ref.shape[0]
    blk = 128
    r = lax.broadcasted_iota(jnp.int32, (blk, blk), 0)
    c = lax.broadcasted_iota(jnp.int32, (blk, blk), 1)
    tril = jnp.where(r >= c, 1.0, 0.0).astype(F32)
    carry = jnp.zeros((1, z_ref.shape[1]), F32)
    for i in range(s // blk):
        z = z_ref[i * blk:(i + 1) * blk, :] + b_ref[...]
        log_f = jnp.minimum(z, 0.0) - jnp.log1p(jnp.exp(-jnp.abs(z)))
        cs = jnp.dot(tril, log_f, precision=lax.Precision.HIGHEST,
                     preferred_element_type=F32) + carry
        o_ref[i * blk:(i + 1) * blk, :] = cs
        carry = cs[blk - 1:blk, :]


def _forget_cumsum(z, bias):
    b, s, w = z.shape
    return pl.pallas_call(
        _cumsum_kernel,
        grid=(b,),
        in_specs=[
            pl.BlockSpec((None, s, w), lambda i: (i, 0, 0)),
            pl.BlockSpec((1, w), lambda i: (0, 0)),
        ],
        out_specs=pl.BlockSpec((None, s, w), lambda i: (i, 0, 0)),
        out_shape=jax.ShapeDtypeStruct((b, s, w), F32),
        compiler_params=_params("parallel"),
        name="forget_cumsum",
    )(z, bias)


LOG2E = float(np.log2(np.e))
BIAS_TERMS = 3
AUG_ROWS = 16


def _split_bf16(x):
    hi = x.astype(BF16).astype(F32)
    r = x - hi
    mid = r.astype(BF16).astype(F32)
    lo = (r - mid).astype(BF16).astype(F32)
    return hi, mid, lo


def _bias_rows(c, idx, sign):
    hi, mid, lo = _split_bf16(c)
    first = 0 if sign > 0 else BIAS_TERMS
    ones_first = BIAS_TERMS if sign > 0 else 0
    out = jnp.where((idx >= ones_first) & (idx < ones_first + BIAS_TERMS), 1.0, 0.0)
    out = jnp.where(idx == first, sign * hi, out)
    out = jnp.where(idx == first + 1, sign * mid, out)
    out = jnp.where(idx == first + 2, sign * lo, out)
    return out.astype(BF16)


def _fox_kernel(qt_ref, k_ref, vt_ref, c_ref, g_ref, o_ref, kaug_sc, qaug_sc, s_sc, m_sc, l_sc, acc_sc,
                *, bq, bk, hps):
    qi = pl.program_id(2)
    s_len = k_ref.shape[0]
    d = HEAD_DIM

    @pl.when(qi == 0)
    def _():
        lane = lax.broadcasted_iota(jnp.int32, (d, d), 1)
        for hh in range(hps):
            kaug_sc[hh, :, 0:d] = k_ref[:, hh * d:(hh + 1) * d]
            for i in range(s_len // d):
                c_row = c_ref[hh, :, i * d:(i + 1) * d] * LOG2E
                c_col = jnp.transpose(jnp.broadcast_to(c_row, (d, d)))
                kaug_sc[hh, i * d:(i + 1) * d, d:2 * d] = _bias_rows(c_col, lane, -1.0)

    q0 = pl.multiple_of(qi * bq, bq)
    sub = lax.broadcasted_iota(jnp.int32, (AUG_ROWS, bq), 0)
    for hh in range(hps):
        cq = jnp.broadcast_to(c_ref[hh, :, pl.ds(q0, bq)] * LOG2E, (AUG_ROWS, bq))
        qaug_sc[hh, 0:d, :] = qt_ref[hh * d:(hh + 1) * d, :]
        qaug_sc[hh, d:d + AUG_ROWS, :] = _bias_rows(cq, sub, 1.0)
        qaug_sc[hh, d + AUG_ROWS:, :] = jnp.zeros((d - AUG_ROWS, bq), BF16)
    m_sc[...] = jnp.full_like(m_sc, -jnp.inf)
    l_sc[...] = jnp.zeros_like(l_sc)
    acc_sc[...] = jnp.zeros_like(acc_sc)

    def scores(ki, slot, lo, width):
        k0 = pl.multiple_of(ki * bk, bk)
        for hh in range(hps):
            s_sc[slot, hh, :, lo:lo + width] = _dot(kaug_sc[hh, pl.ds(k0, bk), :],
                                                    qaug_sc[hh, :, lo:lo + width])

    def absorb(ki, slot, lo, width, masked):
        cols = slice(lo, lo + width)
        k0 = pl.multiple_of(ki * bk, bk)
        for hh in range(hps):
            s = s_sc[slot, hh, :, cols]
            if masked:
                r = lax.broadcasted_iota(jnp.int32, (bk, width), 0)
                c = lax.broadcasted_iota(jnp.int32, (bk, width), 1)
                s = jnp.where(r - c <= q0 + lo - k0, s, -jnp.inf)
            m_prev = m_sc[hh, :, cols]
            m_new = jnp.maximum(m_prev, jnp.max(s, axis=0, keepdims=True))
            alpha = jnp.exp2(m_prev - m_new)
            p = jnp.exp2(s - m_new)
            l_sc[hh, :, cols] = alpha * l_sc[hh, :, cols] + jnp.sum(p, axis=0, keepdims=True)
            pv = _dot(vt_ref[hh * d:(hh + 1) * d, pl.ds(k0, bk)], p.astype(BF16))
            acc_sc[hh, :, cols] = alpha * acc_sc[hh, :, cols] + pv
            m_sc[hh, :, cols] = m_new

    scores(0, 0, 0, bq)

    def body(pair, carry):
        ki = 2 * pair
        scores(ki + 1, 1, 0, bq)
        absorb(ki, 0, 0, bq, False)
        scores(ki + 2, 0, 0, bq)
        absorb(ki + 1, 1, 0, bq, False)
        return carry

    lax.fori_loop(0, qi, body, 0)
    scores(2 * qi + 1, 1, bk, bk)
    absorb(2 * qi, 0, 0, bq, True)
    absorb(2 * qi + 1, 1, bk, bk, True)
    for hh in range(hps):
        y_t = acc_sc[hh] * (1.0 / l_sc[hh])
        y_t = y_t * lax.rsqrt(jnp.mean(y_t * y_t, axis=0, keepdims=True) + EPS)
        o_ref[:, hh * d:(hh + 1) * d] = (
            jnp.transpose(y_t) * g_ref[:, hh * d:(hh + 1) * d]).astype(o_ref.dtype)


def _fox(qvt, k, c_row, gain, heads):
    b, s, _ = k.shape
    blk = min(FOX_Q_ROWS, s)
    nq = s // blk
    hps = FOX_HEADS_PER_STEP
    w = hps * HEAD_DIM
    assert s % blk == 0 and blk % 2 == 0 and heads % hps == 0
    return pl.pallas_call(
        functools.partial(_fox_kernel, bq=blk, bk=blk // 2, hps=hps),
        grid=(b, heads // hps, nq),
        in_specs=[
            pl.BlockSpec((w, blk), lambda bi, h, qi: (h, bi * nq + qi)),
            pl.BlockSpec((None, s, w), lambda bi, h, qi: (bi, 0, h)),
            pl.BlockSpec((w, s), lambda bi, h, qi: (heads // hps + h, bi)),
            pl.BlockSpec((None, hps, 1, s), lambda bi, h, qi: (bi, h, 0, 0)),
            pl.BlockSpec((1, w), lambda bi, h, qi: (0, h)),
        ],
        out_specs=pl.BlockSpec((None, blk, w), lambda bi, h, qi: (bi, qi, h)),
        out_shape=jax.ShapeDtypeStruct((b, s, heads * HEAD_DIM), BF16),
        scratch_shapes=[
            pltpu.VMEM((hps, s, 2 * HEAD_DIM), BF16),
            pltpu.VMEM((hps, 2 * HEAD_DIM, blk), BF16),
            pltpu.VMEM((2, hps, blk // 2, blk), F32),
            pltpu.VMEM((hps, 1, blk), F32),
            pltpu.VMEM((hps, 1, blk), F32),
            pltpu.VMEM((hps, HEAD_DIM, blk), F32),
        ],
        compiler_params=_params("parallel", "parallel", "arbitrary"),
        name="fox",
    )(qvt, k, qvt, c_row, gain)


def _ret_kernel(h_ref, w_ref, cos_ref, sin_ref, gain_ref, o_ref, p_sc, state_sc,
                *, rows, chunk, heads):
    n = pl.program_id(1)
    d = HEAD_DIM
    w = heads * d

    @pl.when(n == 0)
    def _():
        state_sc[...] = jnp.zeros_like(state_sc)

    p_sc[...] = _dot(h_ref[...], w_ref[...])
    t = lax.broadcasted_iota(jnp.int32, (chunk, 1), 0).astype(F32)
    r = lax.broadcasted_iota(jnp.int32, (chunk, chunk), 0)
    c = lax.broadcasted_iota(jnp.int32, (chunk, chunk), 1)
    diff = (r - c).astype(F32)
    for hh in range(heads):
        lg = float(np.log1p(-np.exp2(-(RET_GAMMA_BASE + hh))))
        lo, hi = hh * d, (hh + 1) * d
        xi = jnp.exp((t + 1.0) * lg)
        zeta = jnp.exp((chunk - 1.0 - t) * lg)
        decay = jnp.where(diff >= 0.0, jnp.exp(jnp.maximum(diff, 0.0) * lg), 0.0)
        for ci in range(rows // chunk):
            rs = slice(ci * chunk, (ci + 1) * chunk)
            cos = cos_ref[rs, :]
            sin = sin_ref[rs, :]

            def rope(x):
                return x * cos + pltpu.roll(x, d // 2, 1) * sin

            q = rope(p_sc[rs, lo:hi])
            k = rope(p_sc[rs, w + lo:w + hi]) * (d ** -0.5)
            v = p_sc[rs, 2 * w + lo:2 * w + hi].astype(BF16)
            gate = p_sc[rs, 3 * w + lo:3 * w + hi]
            qb = q.astype(BF16)
            scores = _dot_nt(qb, k.astype(BF16)) * decay
            state = state_sc[hh]
            o = _dot(scores.astype(BF16), v) + _dot(qb, state.astype(BF16)) * xi
            kz_t = jnp.transpose(k * zeta).astype(BF16)
            state_sc[hh] = state * float(np.exp(chunk * lg)) + _dot(kz_t, v)
            y = _rms(o, gain_ref[:, lo:hi]) * (gate * jax.nn.sigmoid(gate))
            o_ref[rs, lo:hi] = y.astype(o_ref.dtype)


def _retention(h, w_ret, layer, cos2, sin2, gain, heads, b, s):
    t, dm = h.shape
    rows = min(RET_ROWS, s)
    chunk = min(RET_CHUNK, rows)
    nc = s // rows
    w = heads * HEAD_DIM
    assert s % rows == 0 and rows % chunk == 0 and t == b * s
    return pl.pallas_call(
        functools.partial(_ret_kernel, rows=rows, chunk=chunk, heads=heads),
        grid=(b, nc),
        in_specs=[
            pl.BlockSpec((rows, dm), lambda bi, n: (bi * nc + n, 0)),
            pl.BlockSpec((None, dm, 4 * w), lambda bi, n: (layer, 0, 0)),
            pl.BlockSpec((rows, HEAD_DIM), lambda bi, n: (n, 0)),
            pl.BlockSpec((rows, HEAD_DIM), lambda bi, n: (n, 0)),
            pl.BlockSpec((1, w), lambda bi, n: (0, 0)),
        ],
        out_specs=pl.BlockSpec((rows, w), lambda bi, n: (bi * nc + n, 0)),
        out_shape=jax.ShapeDtypeStruct((t, w), BF16),
        scratch_shapes=[
            pltpu.VMEM((rows, 4 * w), F32),
            pltpu.VMEM((heads, HEAD_DIM, HEAD_DIM), F32),
        ],
        compiler_params=_params("parallel", "arbitrary"),
        name="retention",
    )(h, w_ret, cos2, sin2, gain)


def _gelu(x):
    return 0.5 * x * (1.0 + jnp.tanh(np.sqrt(2.0 / np.pi).astype(np.float32) * (x + 0.044715 * (x * x * x))))


def _gmlp_kernel(h_ref, w_ref, lng_ref, lnb_ref, ws_ref, bs_ref, gain_ref, o_ref, uv_ref,
                 *, groups, rows):
    blk = CHUNK_GMLP
    pair = 2 * HEAD_DIM
    for g in range(groups):
        uv_ref[:, g * pair:(g + 1) * pair] = _dot(h_ref[...], w_ref[:, g * pair:(g + 1) * pair])
    r = lax.broadcasted_iota(jnp.int32, (blk, blk), 0)
    c = lax.broadcasted_iota(jnp.int32, (blk, blk), 1)
    for g in range(groups):
        lo, hi = g * HEAD_DIM, (g + 1) * HEAD_DIM
        u_lo, v_lo = g * pair, g * pair + HEAD_DIM
        wm = jnp.where(r >= c, ws_ref[g], 0.0).astype(BF16)
        bias = bs_ref[g]
        v = _gelu(uv_ref[:, v_lo:v_lo + HEAD_DIM])
        mu = jnp.mean(v, axis=-1, keepdims=True)
        var = jnp.mean(jnp.square(v - mu), axis=-1, keepdims=True)
        v = ((v - mu) * lax.rsqrt(var + EPS) * lng_ref[:, lo:hi] + lnb_ref[:, lo:hi]).astype(BF16)
        v_wide = jnp.concatenate([v[ci * blk:(ci + 1) * blk] for ci in range(rows // blk)], axis=1)
        mixed = _dot(wm, v_wide) + bias
        for ci in range(rows // blk):
            rs = slice(ci * blk, (ci + 1) * blk)
            y = _gelu(uv_ref[rs, u_lo:u_lo + HEAD_DIM]) * mixed[:, ci * HEAD_DIM:(ci + 1) * HEAD_DIM]
            o_ref[rs, lo:hi] = _rms(y, gain_ref[:, lo:hi]).astype(o_ref.dtype)


def _gmlp(h, w_gm, layer, ln_g, ln_b, w_s, b_s, gain, groups):
    t, dm = h.shape
    w = groups * HEAD_DIM
    rows = min(GMLP_ROWS, t)
    assert t % rows == 0 and rows % CHUNK_GMLP == 0
    return pl.pallas_call(
        functools.partial(_gmlp_kernel, groups=groups, rows=rows),
        grid=(t // rows,),
        in_specs=[
            pl.BlockSpec((rows, dm), lambda i: (i, 0)),
            pl.BlockSpec((None, dm, 2 * w), lambda i: (layer, 0, 0)),
            pl.BlockSpec((1, w), lambda i: (0, 0)),
            pl.BlockSpec((1, w), lambda i: (0, 0)),
            pl.BlockSpec((groups, CHUNK_GMLP, CHUNK_GMLP), lambda i: (0, 0, 0)),
            pl.BlockSpec((groups, CHUNK_GMLP, 1), lambda i: (0, 0, 0)),
            pl.BlockSpec((1, w), lambda i: (0, 0)),
        ],
        out_specs=pl.BlockSpec((rows, w), lambda i: (i, 0)),
        out_shape=jax.ShapeDtypeStruct((t, w), BF16),
        scratch_shapes=[pltpu.VMEM((rows, 2 * w), F32)],
        compiler_params=_params("parallel"),
        name="gmlp",
    )(h, w_gm, ln_g, ln_b, w_s, b_s, gain)


def _outproj_kernel(x_ref, ya_ref, yb_ref, yc_ref, w_ref, o_ref):
    wa, wb = ya_ref.shape[1], yb_ref.shape[1]
    acc = _dot(ya_ref[...], w_ref[0:wa, :])
    acc += _dot(yb_ref[...], w_ref[wa:wa + wb, :])
    acc += _dot(yc_ref[...], w_ref[wa + wb:, :])
    o_ref[...] = x_ref[...] + acc


def _outproj(x, ya, yb, yc, w, layer):
    t, d = x.shape
    k = w.shape[-2]
    bm, bn = min(OUT_ROWS, t), min(OUT_COLS, d)
    assert t % bm == 0 and d % bn == 0
    return pl.pallas_call(
        _outproj_kernel,
        grid=(t // bm, d // bn),
        in_specs=[
            pl.BlockSpec((bm, bn), lambda i, j: (i, j)),
            pl.BlockSpec((bm, ya.shape[1]), lambda i, j: (i, 0)),
            pl.BlockSpec((bm, yb.shape[1]), lambda i, j: (i, 0)),
            pl.BlockSpec((bm, yc.shape[1]), lambda i, j: (i, 0)),
            pl.BlockSpec((None, k, bn), lambda i, j: (layer, 0, j)),
        ],
        out_specs=pl.BlockSpec((bm, bn), lambda i, j: (i, j)),
        out_shape=jax.ShapeDtypeStruct((t, d), F32),
        compiler_params=_params("parallel", "arbitrary"),
        name="outproj",
    )(x, ya, yb, yc, w)


def _repack_kernel(w_ref, qv_ref, kz_ref, ret_ref, gm_ref, *, fox_w, fox_h, ret_cols, gm_w):
    d = HEAD_DIM
    rows, total = w_ref.shape
    o_fz = 3 * fox_w
    qv_ref[:, :fox_w] = w_ref[:, :fox_w].astype(BF16)
    qv_ref[:, fox_w:] = w_ref[:, 2 * fox_w:o_fz].astype(BF16)
    kz_ref[:, :fox_w] = w_ref[:, fox_w:2 * fox_w].astype(BF16)
    lane = lax.broadcasted_iota(jnp.int32, (rows, d), 1)
    kz_ref[:, fox_w:] = jnp.where(lane < fox_h, w_ref[:, o_fz:o_fz + d], 0.0).astype(BF16)

    def shifted(c0):
        lo = o_fz + c0
        a = w_ref[:, lo:lo + d]
        if lo + 2 * d <= total:
            b = w_ref[:, lo + d:lo + 2 * d]
        else:
            tail = w_ref[:, lo + d:total]
            b = jnp.concatenate([tail, jnp.zeros((rows, 2 * d - (total - lo)), F32)], axis=1)
        a = pltpu.roll(a, d - fox_h, 1)
        b = pltpu.roll(b, d - fox_h, 1)
        return jnp.where(lane < d - fox_h, a, b).astype(BF16)

    for c in range(ret_cols // d):
        ret_ref[:, c * d:(c + 1) * d] = shifted(c * d)
    for g in range(gm_w // d):
        for side in (0, 1):
            gm_ref[:, (2 * g + side) * d:(2 * g + side + 1) * d] = shifted(
                ret_cols + side * gm_w + g * d)


def _repack_w_in(w_in, fox_w, fox_h, ret_cols, gm_w):
    depth, dm, total = w_in.shape
    rows = min(REPACK_ROWS, dm)
    assert dm % rows == 0 and total == 3 * fox_w + fox_h + ret_cols + 2 * gm_w
    widths = (2 * fox_w, fox_w + HEAD_DIM, ret_cols, 2 * gm_w)
    return pl.pallas_call(
        functools.partial(_repack_kernel, fox_w=fox_w, fox_h=fox_h, ret_cols=ret_cols, gm_w=gm_w),
        grid=(depth, dm // rows),
        in_specs=[pl.BlockSpec((None, rows, total), lambda l, i: (l, i, 0))],
        out_specs=[pl.BlockSpec((None, rows, n), lambda l, i: (l, i, 0)) for n in widths],
        out_shape=[jax.ShapeDtypeStruct((depth, dm, n), BF16) for n in widths],
        compiler_params=_params("parallel", "parallel"),
        name="repack_w_in",
    )(w_in)


def kernel(x, ffn1_norm, ffn1_w_gate, ffn1_w_up, ffn1_w_down, mix_norm, w_in, fox_b_f,
           gmlp_ln_g, gmlp_ln_b, gmlp_w_s, gmlp_b_s, out_norm, w_out, ffn2_norm,
           ffn2_w_gate, ffn2_w_up, ffn2_w_down, final_norm):
    b, s, d = x.shape
    depth = w_in.shape[0]
    t = b * s
    n_heads = d // HEAD_DIM
    fox_h, ret_h = n_heads // 2, n_heads // 4
    gm_g = n_heads - fox_h - ret_h
    fox_w, ret_w, gm_w = fox_h * HEAD_DIM, ret_h * HEAD_DIM, gm_g * HEAD_DIM

    w_qv, w_kz, w_ret, w_gm = _repack_w_in(w_in, fox_w, fox_h, 4 * ret_w, gm_w)
    w_o = w_out.astype(BF16)
    f1 = [w.astype(BF16) for w in (ffn1_w_gate, ffn1_w_up, ffn1_w_down)]
    f2 = [w.astype(BF16) for w in (ffn2_w_gate, ffn2_w_up, ffn2_w_down)]
    fz_bias = jnp.pad(fox_b_f, ((0, 0), (0, HEAD_DIM - fox_h)))

    half = HEAD_DIM // 2
    pos = jnp.arange(s, dtype=F32)
    inv_freq = ROPE_BASE ** (-jnp.arange(half, dtype=F32) / half)
    ang = pos[:, None] * inv_freq[None, :]
    cos2 = jnp.concatenate([jnp.cos(ang), jnp.cos(ang)], axis=-1)
    sin2 = jnp.concatenate([-jnp.sin(ang), jnp.sin(ang)], axis=-1)

    xf = x.reshape(t, d)
    for l in range(depth):
        x1, h = _ffn(xf, ffn1_norm[l][None], *f1, mix_norm[l][None], l, emit_h=True)
        qvt = _proj_t(h, w_qv, l, BF16, scale=HEAD_DIM ** -0.5 * LOG2E, scaled_rows=fox_w)
        kf, fz = _proj_kz(h, w_kz, l, fox_w)

        c = _forget_cumsum(fz.reshape(b, s, HEAD_DIM), fz_bias[l][None])
        c = jnp.transpose(c[:, :, :fox_h], (0, 2, 1))
        gains = out_norm[l][None]
        ya = _fox(qvt, kf.reshape(b, s, fox_w), c[:, :, None, :], gains[:, :fox_w], fox_h)
        yb = _retention(h, w_ret, l, cos2, sin2, gains[:, fox_w:fox_w + ret_w], ret_h, b, s)
        yc = _gmlp(h, w_gm, l, gmlp_ln_g[l][None], gmlp_ln_b[l][None], gmlp_w_s[l],
                   gmlp_b_s[l][..., None], gains[:, fox_w + ret_w:], gm_g)
        x2 = _outproj(x1, ya.reshape(t, fox_w), yb, yc, w_o, l)

        (xf,) = _ffn(x2, ffn2_norm[l][None], *f2, final_norm[None], l, norm_out=l == depth - 1)
    return xf.reshape(b, s, d)
```

```python
import functools

import jax
import jax.numpy as jnp
import numpy as np
from jax import lax
from jax.experimental import pallas as pl
from jax.experimental.pallas import tpu as pltpu

F32 = jnp.float32
BF16 = jnp.bfloat16

HEAD_DIM = 128
CHUNK_GMLP = 128
ROPE_BASE = 10000.0
RET_GAMMA_BASE = 5.0
EPS = 1e-6

V7X_VMEM_BYTES = 64 * 1024 * 1024
VMEM_LIMIT = V7X_VMEM_BYTES - 1 * 1024 * 1024

FFN_ROWS = 1024
FFN_PROLOGUE_ROWS = 256
FFN_EPILOGUE_ROWS = 128
FFN_COLS = 512
MM_ROWS = 1024
MM_COLS = 1024
OUT_ROWS = 2048
OUT_COLS = 512
FOX_Q_ROWS = 1024
FOX_HEADS_PER_STEP = 4
RET_ROWS = 512
RET_CHUNK = 256
GMLP_ROWS = 512


def _params(*sem):
    return pltpu.CompilerParams(dimension_semantics=sem, vmem_limit_bytes=VMEM_LIMIT)


def _rms(x, gain):
    return x * lax.rsqrt(jnp.mean(x * x, axis=-1, keepdims=True) + EPS) * gain


def _dot(a, b):
    return jnp.dot(a, b, preferred_element_type=F32)


def _dot_nt(a, b):
    return lax.dot_general(a, b, (((1,), (1,)), ((), ())), preferred_element_type=F32)


def _ffn_kernel(*refs, emit_h, norm_out, n_cast):
    x_ref, g_ref, wg_ref, wu_ref, wd_ref, g2_ref = refs[:6]
    cast_in = refs[6:6 + n_cast]
    outs, xn_sc = refs[6 + n_cast:-1], refs[-1]
    acc_ref = outs[0]
    cast_out = outs[1 + int(emit_h):]
    j = pl.program_id(1)

    def strips(rows):
        return x_ref.shape[0] // rows, lambda r: pl.ds(pl.multiple_of(r * rows, rows), rows)

    @pl.when(j == 0)
    def _():
        n, strip = strips(FFN_PROLOGUE_ROWS)

        def body(r, carry):
            xn_sc[strip(r), :] = _rms(x_ref[strip(r), :], g_ref[...]).astype(BF16)
            acc_ref[strip(r), :] = jnp.zeros((FFN_PROLOGUE_ROWS, acc_ref.shape[1]), F32)
            return carry

        lax.fori_loop(0, n, body, 0)

    xn = xn_sc[...]
    a = _dot(xn, wg_ref[...])
    b = _dot(xn, wu_ref[...])
    hmid = (a * jax.nn.sigmoid(a) * b).astype(BF16)
    acc_ref[...] += _dot(hmid, wd_ref[...])
    for src, dst in zip(cast_in, cast_out):
        dst[...] = src[...].astype(BF16)

    @pl.when(j == pl.num_programs(1) - 1)
    def _():
        n, strip = strips(FFN_EPILOGUE_ROWS)

        def body(r, carry):
            y = x_ref[strip(r), :] + 0.5 * acc_ref[strip(r), :]
            acc_ref[strip(r), :] = _rms(y, g2_ref[...]) if norm_out else y
            if emit_h:
                outs[1][strip(r), :] = _rms(y, g2_ref[...]).astype(BF16)
            return carry

        lax.fori_loop(0, n, body, 0)


def _ffn(x, gain, wg, wu, wd, gain2, *, emit_h=False, norm_out=False, cast=(), cast_layer=0):
    t, d = x.shape
    f = wg.shape[-1]
    bm, tf = min(FFN_ROWS, t), FFN_COLS
    assert t % bm == 0 and f % tf == 0
    ni, nj = t // bm, f // tf
    in_specs = [
        pl.BlockSpec((bm, d), lambda i, j: (i, 0)),
        pl.BlockSpec((1, d), lambda i, j: (0, 0)),
        pl.BlockSpec((d, tf), lambda i, j: (0, j)),
        pl.BlockSpec((d, tf), lambda i, j: (0, j)),
        pl.BlockSpec((tf, d), lambda i, j: (j, 0)),
        pl.BlockSpec((1, d), lambda i, j: (0, 0)),
    ]
    out_shape = [jax.ShapeDtypeStruct((t, d), F32)]
    if emit_h:
        out_shape.append(jax.ShapeDtypeStruct((t, d), BF16))
    out_specs = [pl.BlockSpec((bm, d), lambda i, j: (i, 0)) for _ in out_shape]
    for w in cast:
        _, r, c = w.shape
        if r == d:
            tile, idx = (r // ni, c // nj), lambda i, j: (i, j)
        else:
            tile, idx = (r // nj, c // ni), lambda i, j: (j, i)
        assert tile[0] % 16 == 0 and tile[1] % 128 == 0
        in_specs.append(
            pl.BlockSpec((None,) + tile, lambda i, j, idx=idx: (cast_layer,) + idx(i, j)))
        out_specs.append(pl.BlockSpec(tile, idx))
        out_shape.append(jax.ShapeDtypeStruct((r, c), BF16))
    return pl.pallas_call(
        functools.partial(_ffn_kernel, emit_h=emit_h, norm_out=norm_out, n_cast=len(cast)),
        grid=(ni, nj),
        in_specs=in_specs,
        out_specs=out_specs,
        out_shape=out_shape,
        scratch_shapes=[pltpu.VMEM((bm, d), BF16)],
        compiler_params=_params("parallel", "arbitrary"),
        name="ffn",
    )(x, gain, wg, wu, wd, gain2, *cast)


def _proj_t_kernel(h_ref, w_ref, o_ref, *, scale, scaled_rows, bn):
    y = _dot(h_ref[...], w_ref[...])
    if scaled_rows:
        j = pl.program_id(1)
        y = y * jnp.where(j * bn < scaled_rows, scale, 1.0).astype(F32)
    o_ref[...] = jnp.transpose(y).astype(o_ref.dtype)


def _proj_t(h, w, layer, out_dtype, *, scale=1.0, scaled_rows=0):
    t, d = h.shape
    n = w.shape[-1]
    bm, bn = min(MM_ROWS, t), min(MM_COLS, n)
    assert t % bm == 0 and n % bn == 0 and scaled_rows % bn == 0
    return pl.pallas_call(
        functools.partial(_proj_t_kernel, scale=scale, scaled_rows=scaled_rows, bn=bn),
        grid=(t // bm, n // bn),
        in_specs=[
            pl.BlockSpec((bm, d), lambda i, j: (i, 0)),
            pl.BlockSpec((None, d, bn), lambda i, j: (layer, 0, j)),
        ],
        out_specs=pl.BlockSpec((bn, bm), lambda i, j: (j, i)),
        out_shape=jax.ShapeDtypeStruct((n, t), out_dtype),
        compiler_params=_params("parallel", "arbitrary"),
        name="proj_t",
    )(h, w)


def _proj_kernel(h_ref, w_ref, k_ref, z_ref):
    y = _dot(h_ref[...], w_ref[...])
    nk = k_ref.shape[1]
    k_ref[...] = y[:, :nk].astype(k_ref.dtype)
    z_ref[...] = y[:, nk:]


def _proj_kz(h, w, layer, nk):
    t, d = h.shape
    n = w.shape[-1]
    bm = min(MM_ROWS, t)
    assert t % bm == 0
    return pl.pallas_call(
        _proj_kernel,
        grid=(t // bm,),
        in_specs=[
            pl.BlockSpec((bm, d), lambda i: (i, 0)),
            pl.BlockSpec((None, d, n), lambda i: (layer, 0, 0)),
        ],
        out_specs=[
            pl.BlockSpec((bm, nk), lambda i: (i, 0)),
            pl.BlockSpec((bm, n - nk), lambda i: (i, 0)),
        ],
        out_shape=[
            jax.ShapeDtypeStruct((t, nk), BF16),
            jax.ShapeDtypeStruct((t, n - nk), F32),
        ],
        compiler_params=_params("parallel"),
        name="proj_kz",
    )(h, w)


def _cumsum_kernel(z_ref, b_ref, o_ref):
    s = z_ref.shape[0]
    blk = 128
    r = lax.broadcasted_iota(jnp.int32, (blk, blk), 0)
    c = lax.broadcasted_iota(jnp.int32, (blk, blk), 1)
    tril = jnp.where(r >= c, 1.0, 0.0).astype(F32)
    carry = jnp.zeros((1, z_ref.shape[1]), F32)
    for i in range(s // blk):
        z = z_ref[i * blk:(i + 1) * blk, :] + b_ref[...]
        log_f = jnp.minimum(z, 0.0) - jnp.log1p(jnp.exp(-jnp.abs(z)))
        cs = jnp.dot(tril, log_f, precision=lax.Precision.HIGHEST,
                     preferred_element_type=F32) + carry
        o_ref[i * blk:(i + 1) * blk, :] = cs
        carry = cs[blk - 1:blk, :]


def _forget_cumsum(z, bias):
    b, s, w = z.shape
    return pl.pallas_call(
        _cumsum_kernel,
        grid=(b,),
        in_specs=[
            pl.BlockSpec((None, s, w), lambda i: (i, 0, 0)),
            pl.BlockSpec((1, w), lambda i: (0, 0)),
        ],
        out_specs=pl.BlockSpec((None, s, w), lambda i: (i, 0, 0)),
        out_shape=jax.ShapeDtypeStruct((b, s, w), F32),
        compiler_params=_params("parallel"),
        name="forget_cumsum",
    )(z, bias)


LOG2E = float(np.log2(np.e))
BIAS_TERMS = 3
AUG_ROWS = 16


def _split_bf16(x):
    hi = x.astype(BF16).astype(F32)
    r = x - hi
    mid = r.astype(BF16).astype(F32)
    lo = (r - mid).astype(BF16).astype(F32)
    return hi, mid, lo


def _bias_rows(c, idx, sign):
    hi, mid, lo = _split_bf16(c)
    first = 0 if sign > 0 else BIAS_TERMS
    ones_first = BIAS_TERMS if sign > 0 else 0
    out = jnp.where((idx >= ones_first) & (idx < ones_first + BIAS_TERMS), 1.0, 0.0)
    out = jnp.where(idx == first, sign * hi, out)
    out = jnp.where(idx == first + 1, sign * mid, out)
    out = jnp.where(idx == first + 2, sign * lo, out)
    return out.astype(BF16)


def _fox_kernel(qt_ref, k_ref, vt_ref, c_ref, g_ref, o_ref, kaug_sc, qaug_sc, s_sc, m_sc, l_sc, acc_sc,
                *, bq, bk, hps):
    qi = pl.program_id(2)
    s_len = k_ref.shape[0]
    d = HEAD_DIM

    @pl.when(qi == 0)
    def _():
        lane = lax.broadcasted_iota(jnp.int32, (d, d), 1)
        for hh in range(hps):
            kaug_sc[hh, :, 0:d] = k_ref[:, hh * d:(hh + 1) * d]
            for i in range(s_len // d):
                c_row = c_ref[hh, :, i * d:(i + 1) * d] * LOG2E
                c_col = jnp.transpose(jnp.broadcast_to(c_row, (d, d)))
                kaug_sc[hh, i * d:(i + 1) * d, d:2 * d] = _bias_rows(c_col, lane, -1.0)

    q0 = pl.multiple_of(qi * bq, bq)
    sub = lax.broadcasted_iota(jnp.int32, (AUG_ROWS, bq), 0)
    for hh in range(hps):
        cq = jnp.broadcast_to(c_ref[hh, :, pl.ds(q0, bq)] * LOG2E, (AUG_ROWS, bq))
        qaug_sc[hh, 0:d, :] = qt_ref[hh * d:(hh + 1) * d, :]
        qaug_sc[hh, d:d + AUG_ROWS, :] = _bias_rows(cq, sub, 1.0)
        qaug_sc[hh, d + AUG_ROWS:, :] = jnp.zeros((d - AUG_ROWS, bq), BF16)
    m_sc[...] = jnp.full_like(m_sc, -jnp.inf)
    l_sc[...] = jnp.zeros_like(l_sc)
    acc_sc[...] = jnp.zeros_like(acc_sc)

    def scores(ki, slot, lo, width):
        k0 = pl.multiple_of(ki * bk, bk)
        for hh in range(hps):
            s_sc[slot, hh, :, lo:lo + width] = _dot(kaug_sc[hh, pl.ds(k0, bk), :],
                                                    qaug_sc[hh, :, lo:lo + width])

    def absorb(ki, slot, lo, width, masked):
        cols = slice(lo, lo + width)
        k0 = pl.multiple_of(ki * bk, bk)
        for hh in range(hps):
            s = s_sc[slot, hh, :, cols]
            if masked:
                r = lax.broadcasted_iota(jnp.int32, (bk, width), 0)
                c = lax.broadcasted_iota(jnp.int32, (bk, width), 1)
                s = jnp.where(r - c <= q0 + lo - k0, s, -jnp.inf)
            m_prev = m_sc[hh, :, cols]
            m_new = jnp.maximum(m_prev, jnp.max(s, axis=0, keepdims=True))
            alpha = jnp.exp2(m_prev - m_new)
            p = jnp.exp2(s - m_new)
            l_sc[hh, :, cols] = alpha * l_sc[hh, :, cols] + jnp.sum(p, axis=0, keepdims=True)
            pv = _dot(vt_ref[hh * d:(hh + 1) * d, pl.ds(k0, bk)], p.astype(BF16))
            acc_sc[hh, :, cols] = alpha * acc_sc[hh, :, cols] + pv
            m_sc[hh, :, cols] = m_new

    scores(0, 0, 0, bq)

    def body(pair, carry):
        ki = 2 * pair
        scores(ki + 1, 1, 0, bq)
        absorb(ki, 0, 0, bq, False)
        scores(ki + 2, 0, 0, bq)
        absorb(ki + 1, 1, 0, bq, False)
        return carry

    lax.fori_loop(0, qi, body, 0)
    scores(2 * qi + 1, 1, bk, bk)
    absorb(2 * qi, 0, 0, bq, True)
    absorb(2 * qi + 1, 1, bk, bk, True)
    for hh in range(hps):
        y_t = acc_sc[hh] * (1.0 / l_sc[hh])
        y_t = y_t * lax.rsqrt(jnp.mean(y_t * y_t, axis=0, keepdims=True) + EPS)
        o_ref[:, hh * d:(hh + 1) * d] = (
            jnp.transpose(y_t) * g_ref[:, hh * d:(hh + 1) * d]).astype(o_ref.dtype)


def _fox(qvt, k, c_row, gain, heads):
    b, s, _ = k.shape
    blk = min(FOX_Q_ROWS, s)
    nq = s // blk
    hps = FOX_HEADS_PER_STEP
    w = hps * HEAD_DIM
    assert s % blk == 0 and blk % 2 == 0 and heads % hps == 0
    return pl.pallas_call(
        functools.partial(_fox_kernel, bq=blk, bk=blk // 2, hps=hps),
        grid=(b, heads // hps, nq),
        in_specs=[
            pl.BlockSpec((w, blk), lambda bi, h, qi: (h, bi * nq + qi)),
            pl.BlockSpec((None, s, w), lambda bi, h, qi: (bi, 0, h)),
            pl.BlockSpec((w, s), lambda bi, h, qi: (heads // hps + h, bi)),
            pl.BlockSpec((None, hps, 1, s), lambda bi, h, qi: (bi, h, 0, 0)),
            pl.BlockSpec((1, w), lambda bi, h, qi: (0, h)),
        ],
        out_specs=pl.BlockSpec((None, blk, w), lambda bi, h, qi: (bi, qi, h)),
        out_shape=jax.ShapeDtypeStruct((b, s, heads * HEAD_DIM), BF16),
        scratch_shapes=[
            pltpu.VMEM((hps, s, 2 * HEAD_DIM), BF16),
            pltpu.VMEM((hps, 2 * HEAD_DIM, blk), BF16),
            pltpu.VMEM((2, hps, blk // 2, blk), F32),
            pltpu.VMEM((hps, 1, blk), F32),
            pltpu.VMEM((hps, 1, blk), F32),
            pltpu.VMEM((hps, HEAD_DIM, blk), F32),
        ],
        compiler_params=_params("parallel", "parallel", "arbitrary"),
        name="fox",
    )(qvt, k, qvt, c_row, gain)


def _ret_kernel(h_ref, w_ref, cos_ref, sin_ref, gain_ref, o_ref, p_sc, state_sc,
                *, rows, chunk, heads):
    n = pl.program_id(1)
    d = HEAD_DIM
    w = heads * d

    @pl.when(n == 0)
    def _():
        state_sc[...] = jnp.zeros_like(state_sc)

    p_sc[...] = _dot(h_ref[...], w_ref[...])
    t = lax.broadcasted_iota(jnp.int32, (chunk, 1), 0).astype(F32)
    r = lax.broadcasted_iota(jnp.int32, (chunk, chunk), 0)
    c = lax.broadcasted_iota(jnp.int32, (chunk, chunk), 1)
    diff = (r - c).astype(F32)
    for hh in range(heads):
        lg = float(np.log1p(-np.exp2(-(RET_GAMMA_BASE + hh))))
        lo, hi = hh * d, (hh + 1) * d
        xi = jnp.exp((t + 1.0) * lg)
        zeta = jnp.exp((chunk - 1.0 - t) * lg)
        decay = jnp.where(diff >= 0.0, jnp.exp(jnp.maximum(diff, 0.0) * lg), 0.0)
        for ci in range(rows // chunk):
            rs = slice(ci * chunk, (ci + 1) * chunk)
            cos = cos_ref[rs, :]
            sin = sin_ref[rs, :]

            def rope(x):
                return x * cos + pltpu.roll(x, d // 2, 1) * sin

            q = rope(p_sc[rs, lo:hi])
            k = rope(p_sc[rs, w + lo:w + hi]) * (d ** -0.5)
            v = p_sc[rs, 2 * w + lo:2 * w + hi].astype(BF16)
            gate = p_sc[rs, 3 * w + lo:3 * w + hi]
            qb = q.astype(BF16)
            scores = _dot_nt(qb, k.astype(BF16)) * decay
            state = state_sc[hh]
            o = _dot(scores.astype(BF16), v) + _dot(qb, state.astype(BF16)) * xi
            kz_t = jnp.transpose(k * zeta).astype(BF16)
            state_sc[hh] = state * float(np.exp(chunk * lg)) + _dot(kz_t, v)
            y = _rms(o, gain_ref[:, lo:hi]) * (gate * jax.nn.sigmoid(gate))
            o_ref[rs, lo:hi] = y.astype(o_ref.dtype)


def _retention(h, w_ret, layer, cos2, sin2, gain, heads, b, s):
    t, dm = h.shape
    rows = min(RET_ROWS, s)
    chunk = min(RET_CHUNK, rows)
    nc = s // rows
    w = heads * HEAD_DIM
    assert s % rows == 0 and rows % chunk == 0 and t == b * s
    return pl.pallas_call(
        functools.partial(_ret_kernel, rows=rows, chunk=chunk, heads=heads),
        grid=(b, nc),
        in_specs=[
            pl.BlockSpec((rows, dm), lambda bi, n: (bi * nc + n, 0)),
            pl.BlockSpec((None, dm, 4 * w), lambda bi, n: (layer, 0, 0)),
            pl.BlockSpec((rows, HEAD_DIM), lambda bi, n: (n, 0)),
            pl.BlockSpec((rows, HEAD_DIM), lambda bi, n: (n, 0)),
            pl.BlockSpec((1, w), lambda bi, n: (0, 0)),
        ],
        out_specs=pl.BlockSpec((rows, w), lambda bi, n: (bi * nc + n, 0)),
        out_shape=jax.ShapeDtypeStruct((t, w), BF16),
        scratch_shapes=[
            pltpu.VMEM((rows, 4 * w), F32),
            pltpu.VMEM((heads, HEAD_DIM, HEAD_DIM), F32),
        ],
        compiler_params=_params("parallel", "arbitrary"),
        name="retention",
    )(h, w_ret, cos2, sin2, gain)


def _gelu(x):
    return 0.5 * x * (1.0 + jnp.tanh(np.sqrt(2.0 / np.pi).astype(np.float32) * (x + 0.044715 * (x * x * x))))


def _gmlp_kernel(h_ref, w_ref, lng_ref, lnb_ref, ws_ref, bs_ref, gain_ref, o_ref, uv_ref,
                 *, groups, rows):
    blk = CHUNK_GMLP
    pair = 2 * HEAD_DIM
    for g in range(groups):
        uv_ref[:, g * pair:(g + 1) * pair] = _dot(h_ref[...], w_ref[:, g * pair:(g + 1) * pair])
    r = lax.broadcasted_iota(jnp.int32, (blk, blk), 0)
    c = lax.broadcasted_iota(jnp.int32, (blk, blk), 1)
    for g in range(groups):
        lo, hi = g * HEAD_DIM, (g + 1) * HEAD_DIM
        u_lo, v_lo = g * pair, g * pair + HEAD_DIM
        wm = jnp.where(r >= c, ws_ref[g], 0.0).astype(BF16)
        bias = bs_ref[g]
        v = _gelu(uv_ref[:, v_lo:v_lo + HEAD_DIM])
        mu = jnp.mean(v, axis=-1, keepdims=True)
        var = jnp.mean(jnp.square(v - mu), axis=-1, keepdims=True)
        v = ((v - mu) * lax.rsqrt(var + EPS) * lng_ref[:, lo:hi] + lnb_ref[:, lo:hi]).astype(BF16)
        v_wide = jnp.concatenate([v[ci * blk:(ci + 1) * blk] for ci in range(rows // blk)], axis=1)
        mixed = _dot(wm, v_wide) + bias
        for ci in range(rows // blk):
            rs = slice(ci * blk, (ci + 1) * blk)
            y = _gelu(uv_ref[rs, u_lo:u_lo + HEAD_DIM]) * mixed[:, ci * HEAD_DIM:(ci + 1) * HEAD_DIM]
            o_ref[rs, lo:hi] = _rms(y, gain_ref[:, lo:hi]).astype(o_ref.dtype)


def _gmlp(h, w_gm, layer, ln_g, ln_b, w_s, b_s, gain, groups):
    t, dm = h.shape
    w = groups * HEAD_DIM
    rows = min(GMLP_ROWS, t)
    assert t % rows == 0 and rows % CHUNK_GMLP == 0
    return pl.pallas_call(
        functools.partial(_gmlp_kernel, groups=groups, rows=rows),
        grid=(t // rows,),
        in_specs=[
            pl.BlockSpec((rows, dm), lambda i: (i, 0)),
            pl.BlockSpec((None, dm, 2 * w), lambda i: (layer, 0, 0)),
            pl.BlockSpec((1, w), lambda i: (0, 0)),
            pl.BlockSpec((1, w), lambda i: (0, 0)),
            pl.BlockSpec((groups, CHUNK_GMLP, CHUNK_GMLP), lambda i: (0, 0, 0)),
            pl.BlockSpec((groups, CHUNK_GMLP, 1), lambda i: (0, 0, 0)),
            pl.BlockSpec((1, w), lambda i: (0, 0)),
        ],
        out_specs=pl.BlockSpec((rows, w), lambda i: (i, 0)),
        out_shape=jax.ShapeDtypeStruct((t, w), BF16),
        scratch_shapes=[pltpu.VMEM((rows, 2 * w), F32)],
        compiler_params=_params("parallel"),
        name="gmlp",
    )(h, w_gm, ln_g, ln_b, w_s, b_s, gain)


def _outproj_kernel(x_ref, ya_ref, yb_ref, yc_ref, w_ref, o_ref):
    wa, wb = ya_ref.shape[1], yb_ref.shape[1]
    acc = _dot(ya_ref[...], w_ref[0:wa, :])
    acc += _dot(yb_ref[...], w_ref[wa:wa + wb, :])
    acc += _dot(yc_ref[...], w_ref[wa + wb:, :])
    o_ref[...] = x_ref[...] + acc


def _outproj(x, ya, yb, yc, w, layer):
    t, d = x.shape
    k = w.shape[-2]
    bm, bn = min(OUT_ROWS, t), min(OUT_COLS, d)
    assert t % bm == 0 and d % bn == 0
    return pl.pallas_call(
        _outproj_kernel,
        grid=(t // bm, d // bn),
        in_specs=[
            pl.BlockSpec((bm, bn), lambda i, j: (i, j)),
            pl.BlockSpec((bm, ya.shape[1]), lambda i, j: (i, 0)),
            pl.BlockSpec((bm, yb.shape[1]), lambda i, j: (i, 0)),
            pl.BlockSpec((bm, yc.shape[1]), lambda i, j: (i, 0)),
            pl.BlockSpec((None, k, bn), lambda i, j: (layer, 0, j)),
        ],
        out_specs=pl.BlockSpec((bm, bn), lambda i, j: (i, j)),
        out_shape=jax.ShapeDtypeStruct((t, d), F32),
        compiler_params=_params("parallel", "arbitrary"),
        name="outproj",
    )(x, ya, yb, yc, w)


def kernel(x, ffn1_norm, ffn1_w_gate, ffn1_w_up, ffn1_w_down, mix_norm, w_in, fox_b_f,
           gmlp_ln_g, gmlp_ln_b, gmlp_w_s, gmlp_b_s, out_norm, w_out, ffn2_norm,
           ffn2_w_gate, ffn2_w_up, ffn2_w_down, final_norm):
    b, s, d = x.shape
    depth = w_in.shape[0]
    t = b * s
    n_heads = d // HEAD_DIM
    fox_h, ret_h = n_heads // 2, n_heads // 4
    gm_g = n_heads - fox_h - ret_h
    fox_w, ret_w, gm_w = fox_h * HEAD_DIM, ret_h * HEAD_DIM, gm_g * HEAD_DIM

    o_fz = 3 * fox_w
    o_ret = o_fz + fox_h
    o_gm = o_ret + 4 * ret_w
    w_qv = jnp.concatenate([w_in[:, :, :fox_w], w_in[:, :, 2 * fox_w:o_fz]], axis=2).astype(BF16)
    w_kz = jnp.concatenate(
        [w_in[:, :, fox_w:2 * fox_w],
         jnp.pad(w_in[:, :, o_fz:o_ret], ((0, 0), (0, 0), (0, HEAD_DIM - fox_h)))], axis=2
    ).astype(BF16)
    w_ret = w_in[:, :, o_ret:o_gm].astype(BF16)
    w_gm = jnp.concatenate(
        [w_in[:, :, o_gm + side * gm_w + g * HEAD_DIM:o_gm + side * gm_w + (g + 1) * HEAD_DIM]
         for g in range(gm_g) for side in (0, 1)], axis=2).astype(BF16)
    w_o = w_out.astype(BF16)
    ffn_f32 = (ffn1_w_gate, ffn1_w_up, ffn1_w_down, ffn2_w_gate, ffn2_w_up, ffn2_w_down)
    ffn_w = [w[0].astype(BF16) for w in ffn_f32]
    fz_bias = jnp.pad(fox_b_f, ((0, 0), (0, HEAD_DIM - fox_h)))

    half = HEAD_DIM // 2
    pos = jnp.arange(s, dtype=F32)
    inv_freq = ROPE_BASE ** (-jnp.arange(half, dtype=F32) / half)
    ang = pos[:, None] * inv_freq[None, :]
    cos2 = jnp.concatenate([jnp.cos(ang), jnp.cos(ang)], axis=-1)
    sin2 = jnp.concatenate([-jnp.sin(ang), jnp.sin(ang)], axis=-1)

    xf = x.reshape(t, d)
    for l in range(depth):
        x1, h = _ffn(xf, ffn1_norm[l][None], *ffn_w[:3], mix_norm[l][None], emit_h=True)
        qvt = _proj_t(h, w_qv, l, BF16, scale=HEAD_DIM ** -0.5 * LOG2E, scaled_rows=fox_w)
        kf, fz = _proj_kz(h, w_kz, l, fox_w)

        c = _forget_cumsum(fz.reshape(b, s, HEAD_DIM), fz_bias[l][None])
        c = jnp.transpose(c[:, :, :fox_h], (0, 2, 1))
        gains = out_norm[l][None]
        ya = _fox(qvt, kf.reshape(b, s, fox_w), c[:, :, None, :], gains[:, :fox_w], fox_h)
        yb = _retention(h, w_ret, l, cos2, sin2, gains[:, fox_w:fox_w + ret_w], ret_h, b, s)
        yc = _gmlp(h, w_gm, l, gmlp_ln_g[l][None], gmlp_ln_b[l][None], gmlp_w_s[l],
                   gmlp_b_s[l][..., None], gains[:, fox_w + ret_w:], gm_g)
        x2 = _outproj(x1, ya.reshape(t, fox_w), yb, yc, w_o, l)

        last = l == depth - 1
        xf, *next_w = _ffn(x2, ffn2_norm[l][None], *ffn_w[3:], final_norm[None], norm_out=last,
                           cast=() if last else ffn_f32, cast_layer=l + 1)
        ffn_w = next_w or ffn_w
    return xf.reshape(b, s, d)
```

```python
import functools

import jax
import jax.numpy as jnp
import numpy as np
from jax import lax
from jax.experimental import pallas as pl
from jax.experimental.pallas import tpu as pltpu

F32 = jnp.float32
BF16 = jnp.bfloat16

HEAD_DIM = 128
CHUNK_GMLP = 128
ROPE_BASE = 10000.0
RET_GAMMA_BASE = 5.0
EPS = 1e-6

V7X_VMEM_BYTES = 64 * 1024 * 1024
VMEM_LIMIT = V7X_VMEM_BYTES - 1 * 1024 * 1024

FFN_ROWS = 1024
FFN_PROLOGUE_ROWS = 256
FFN_EPILOGUE_ROWS = 128
FFN_COLS = 512
MM_ROWS = 1024
MM_COLS = 1024
OUT_ROWS = 2048
OUT_COLS = 512
FOX_Q_ROWS = 1024
FOX_HEADS_PER_STEP = 4
RET_ROWS = 512
RET_CHUNK = 256
GMLP_ROWS = 512


def _params(*sem):
    return pltpu.CompilerParams(dimension_semantics=sem, vmem_limit_bytes=VMEM_LIMIT)


def _rms(x, gain):
    return x * lax.rsqrt(jnp.mean(x * x, axis=-1, keepdims=True) + EPS) * gain


def _dot(a, b):
    return jnp.dot(a, b, preferred_element_type=F32)


def _dot_nt(a, b):
    return lax.dot_general(a, b, (((1,), (1,)), ((), ())), preferred_element_type=F32)


def _ffn_kernel(*refs, emit_h, norm_out, n_cast):
    x_ref, g_ref, wg_ref, wu_ref, wd_ref, g2_ref = refs[:6]
    cast_in = refs[6:6 + n_cast]
    outs, xn_sc = refs[6 + n_cast:-1], refs[-1]
    acc_ref = outs[0]
    cast_out = outs[1 + int(emit_h):]
    j = pl.program_id(1)

    def strips(rows):
        return x_ref.shape[0] // rows, lambda r: pl.ds(pl.multiple_of(r * rows, rows), rows)

    @pl.when(j == 0)
    def _():
        n, strip = strips(FFN_PROLOGUE_ROWS)

        def body(r, carry):
            xn_sc[strip(r), :] = _rms(x_ref[strip(r), :], g_ref[...]).astype(BF16)
            acc_ref[strip(r), :] = jnp.zeros((FFN_PROLOGUE_ROWS, acc_ref.shape[1]), F32)
            return carry

        lax.fori_loop(0, n, body, 0)

    xn = xn_sc[...]
    a = _dot(xn, wg_ref[...])
    b = _dot(xn, wu_ref[...])
    hmid = (a * jax.nn.sigmoid(a) * b).astype(BF16)
    acc_ref[...] += _dot(hmid, wd_ref[...])
    for src, dst in zip(cast_in, cast_out):
        dst[...] = src[...].astype(BF16)

    @pl.when(j == pl.num_programs(1) - 1)
    def _():
        n, strip = strips(FFN_EPILOGUE_ROWS)

        def body(r, carry):
            y = x_ref[strip(r), :] + 0.5 * acc_ref[strip(r), :]
            acc_ref[strip(r), :] = _rms(y, g2_ref[...]) if norm_out else y
            if emit_h:
                outs[1][strip(r), :] = _rms(y, g2_ref[...]).astype(BF16)
            return carry

        lax.fori_loop(0, n, body, 0)


def _ffn(x, gain, wg, wu, wd, gain2, *, emit_h=False, norm_out=False, cast=(), cast_layer=0):
    t, d = x.shape
    f = wg.shape[-1]
    bm, tf = min(FFN_ROWS, t), FFN_COLS
    assert t % bm == 0 and f % tf == 0
    ni, nj = t // bm, f // tf
    in_specs = [
        pl.BlockSpec((bm, d), lambda i, j: (i, 0)),
        pl.BlockSpec((1, d), lambda i, j: (0, 0)),
        pl.BlockSpec((d, tf), lambda i, j: (0, j)),
        pl.BlockSpec((d, tf), lambda i, j: (0, j)),
        pl.BlockSpec((tf, d), lambda i, j: (j, 0)),
        pl.BlockSpec((1, d), lambda i, j: (0, 0)),
    ]
    out_shape = [jax.ShapeDtypeStruct((t, d), F32)]
    if emit_h:
        out_shape.append(jax.ShapeDtypeStruct((t, d), BF16))
    out_specs = [pl.BlockSpec((bm, d), lambda i, j: (i, 0)) for _ in out_shape]
    for w in cast:
        _, r, c = w.shape
        if r == d:
            tile, idx = (r // ni, c // nj), lambda i, j: (i, j)
        else:
            tile, idx = (r // nj, c // ni), lambda i, j: (j, i)
        assert tile[0] % 16 == 0 and tile[1] % 128 == 0
        in_specs.append(
            pl.BlockSpec((None,) + tile, lambda i, j, idx=idx: (cast_layer,) + idx(i, j)))
        out_specs.append(pl.BlockSpec(tile, idx))
        out_shape.append(jax.ShapeDtypeStruct((r, c), BF16))
    return pl.pallas_call(
        functools.partial(_ffn_kernel, emit_h=emit_h, norm_out=norm_out, n_cast=len(cast)),
        grid=(ni, nj),
        in_specs=in_specs,
        out_specs=out_specs,
        out_shape=out_shape,
        scratch_shapes=[pltpu.VMEM((bm, d), BF16)],
        compiler_params=_params("parallel", "arbitrary"),
        name="ffn",
    )(x, gain, wg, wu, wd, gain2, *cast)


def _proj_t_kernel(h_ref, w_ref, o_ref, *, scale, scaled_rows, bn):
    y = _dot(h_ref[...], w_ref[...])
    if scaled_rows:
        j = pl.program_id(1)
        y = y * jnp.where(j * bn < scaled_rows, scale, 1.0).astype(F32)
    o_ref[...] = jnp.transpose(y).astype(o_ref.dtype)


def _proj_t(h, w, layer, out_dtype, *, scale=1.0, scaled_rows=0):
    t, d = h.shape
    n = w.shape[-1]
    bm, bn = min(MM_ROWS, t), min(MM_COLS, n)
    assert t % bm == 0 and n % bn == 0 and scaled_rows % bn == 0
    return pl.pallas_call(
        functools.partial(_proj_t_kernel, scale=scale, scaled_rows=scaled_rows, bn=bn),
        grid=(t // bm, n // bn),
        in_specs=[
            pl.BlockSpec((bm, d), lambda i, j: (i, 0)),
            pl.BlockSpec((None, d, bn), lambda i, j: (layer, 0, j)),
        ],
        out_specs=pl.BlockSpec((bn, bm), lambda i, j: (j, i)),
        out_shape=jax.ShapeDtypeStruct((n, t), out_dtype),
        compiler_params=_params("parallel", "arbitrary"),
        name="proj_t",
    )(h, w)


def _proj_kc_kernel(h_ref, w_ref, b_ref, k_ref, c_ref, carry_sc, *, blocks_per_seq):
    y = _dot(h_ref[...], w_ref[...])
    nk = k_ref.shape[1]
    k_ref[...] = y[:, :nk].astype(k_ref.dtype)

    @pl.when(pl.program_id(0) % blocks_per_seq == 0)
    def _():
        carry_sc[...] = jnp.zeros_like(carry_sc)

    blk = 128
    r = lax.broadcasted_iota(jnp.int32, (blk, blk), 0)
    c = lax.broadcasted_iota(jnp.int32, (blk, blk), 1)
    tril = jnp.where(r >= c, 1.0, 0.0).astype(F32)
    carry = carry_sc[...]
    for i in range(h_ref.shape[0] // blk):
        z = y[i * blk:(i + 1) * blk, nk:] + b_ref[...]
        log_f = jnp.minimum(z, 0.0) - jnp.log1p(jnp.exp(-jnp.abs(z)))
        cs = jnp.dot(tril, log_f, precision=lax.Precision.HIGHEST,
                     preferred_element_type=F32) + carry
        c_ref[i * blk:(i + 1) * blk, :] = cs
        carry = cs[blk - 1:blk, :]
    carry_sc[...] = carry


def _proj_kc(h, w, bias, layer, nk, s):
    t, d = h.shape
    n = w.shape[-1]
    bm = min(MM_ROWS, s)
    assert s % bm == 0 and bm % 128 == 0
    return pl.pallas_call(
        functools.partial(_proj_kc_kernel, blocks_per_seq=s // bm),
        grid=(t // bm,),
        in_specs=[
            pl.BlockSpec((bm, d), lambda i: (i, 0)),
            pl.BlockSpec((None, d, n), lambda i: (layer, 0, 0)),
            pl.BlockSpec((1, n - nk), lambda i: (0, 0)),
        ],
        out_specs=[
            pl.BlockSpec((bm, nk), lambda i: (i, 0)),
            pl.BlockSpec((bm, n - nk), lambda i: (i, 0)),
        ],
        out_shape=[
            jax.ShapeDtypeStruct((t, nk), BF16),
            jax.ShapeDtypeStruct((t, n - nk), F32),
        ],
        scratch_shapes=[pltpu.VMEM((1, n - nk), F32)],
        compiler_params=_params("arbitrary"),
        name="proj_kc",
    )(h, w, bias)


LOG2E = float(np.log2(np.e))
BIAS_TERMS = 3
AUG_ROWS = 16


def _split_bf16(x):
    hi = x.astype(BF16).astype(F32)
    r = x - hi
    mid = r.astype(BF16).astype(F32)
    lo = (r - mid).astype(BF16).astype(F32)
    return hi, mid, lo


def _bias_rows(c, idx, sign):
    hi, mid, lo = _split_bf16(c)
    first = 0 if sign > 0 else BIAS_TERMS
    ones_first = BIAS_TERMS if sign > 0 else 0
    out = jnp.where((idx >= ones_first) & (idx < ones_first + BIAS_TERMS), 1.0, 0.0)
    out = jnp.where(idx == first, sign * hi, out)
    out = jnp.where(idx == first + 1, sign * mid, out)
    out = jnp.where(idx == first + 2, sign * lo, out)
    return out.astype(BF16)


def _fox_kernel(qt_ref, k_ref, vt_ref, c_ref, g_ref, o_ref, kaug_sc, qaug_sc, s_sc, m_sc, l_sc, acc_sc,
                *, bq, bk, hps):
    qi = pl.program_id(2)
    s_len = k_ref.shape[0]
    d = HEAD_DIM

    @pl.when(qi == 0)
    def _():
        lane = lax.broadcasted_iota(jnp.int32, (d, d), 1)
        for hh in range(hps):
            kaug_sc[hh, :, 0:d] = k_ref[:, hh * d:(hh + 1) * d]
            for i in range(s_len // d):
                c_row = c_ref[hh, :, i * d:(i + 1) * d] * LOG2E
                c_col = jnp.transpose(jnp.broadcast_to(c_row, (d, d)))
                kaug_sc[hh, i * d:(i + 1) * d, d:2 * d] = _bias_rows(c_col, lane, -1.0)

    q0 = pl.multiple_of(qi * bq, bq)
    sub = lax.broadcasted_iota(jnp.int32, (AUG_ROWS, bq), 0)
    for hh in range(hps):
        cq = jnp.broadcast_to(c_ref[hh, :, pl.ds(q0, bq)] * LOG2E, (AUG_ROWS, bq))
        qaug_sc[hh, 0:d, :] = qt_ref[hh * d:(hh + 1) * d, :]
        qaug_sc[hh, d:d + AUG_ROWS, :] = _bias_rows(cq, sub, 1.0)
        qaug_sc[hh, d + AUG_ROWS:, :] = jnp.zeros((d - AUG_ROWS, bq), BF16)
    m_sc[...] = jnp.full_like(m_sc, -jnp.inf)
    l_sc[...] = jnp.zeros_like(l_sc)
    acc_sc[...] = jnp.zeros_like(acc_sc)

    def scores(ki, slot, lo, width):
        k0 = pl.multiple_of(ki * bk, bk)
        for hh in range(hps):
            s_sc[slot, hh, :, lo:lo + width] = _dot(kaug_sc[hh, pl.ds(k0, bk), :],
                                                    qaug_sc[hh, :, lo:lo + width])

    def absorb(ki, slot, lo, width, masked):
        cols = slice(lo, lo + width)
        k0 = pl.multiple_of(ki * bk, bk)
        for hh in range(hps):
            s = s_sc[slot, hh, :, cols]
            if masked:
                r = lax.broadcasted_iota(jnp.int32, (bk, width), 0)
                c = lax.broadcasted_iota(jnp.int32, (bk, width), 1)
                s = jnp.where(r - c <= q0 + lo - k0, s, -jnp.inf)
            m_prev = m_sc[hh, :, cols]
            m_new = jnp.maximum(m_prev, jnp.max(s, axis=0, keepdims=True))
            alpha = jnp.exp2(m_prev - m_new)
            p = jnp.exp2(s - m_new)
            l_sc[hh, :, cols] = alpha * l_sc[hh, :, cols] + jnp.sum(p, axis=0, keepdims=True)
            pv = _dot(vt_ref[hh * d:(hh + 1) * d, pl.ds(k0, bk)], p.astype(BF16))
            acc_sc[hh, :, cols] = alpha * acc_sc[hh, :, cols] + pv
            m_sc[hh, :, cols] = m_new

    scores(0, 0, 0, bq)

    def body(pair, carry):
        ki = 2 * pair
        scores(ki + 1, 1, 0, bq)
        absorb(ki, 0, 0, bq, False)
        scores(ki + 2, 0, 0, bq)
        absorb(ki + 1, 1, 0, bq, False)
        return carry

    lax.fori_loop(0, qi, body, 0)
    scores(2 * qi + 1, 1, bk, bk)
    absorb(2 * qi, 0, 0, bq, True)
    absorb(2 * qi + 1, 1, bk, bk, True)
    for hh in range(hps):
        y_t = acc_sc[hh] * (1.0 / l_sc[hh])
        y_t = y_t * lax.rsqrt(jnp.mean(y_t * y_t, axis=0, keepdims=True) + EPS)
        o_ref[:, hh * d:(hh + 1) * d] = (
            jnp.transpose(y_t) * g_ref[:, hh * d:(hh + 1) * d]).astype(o_ref.dtype)


def _fox(qvt, k, c_row, gain, heads):
    b, s, _ = k.shape
    blk = min(FOX_Q_ROWS, s)
    nq = s // blk
    hps = FOX_HEADS_PER_STEP
    w = hps * HEAD_DIM
    assert s % blk == 0 and blk % 2 == 0 and heads % hps == 0
    return pl.pallas_call(
        functools.partial(_fox_kernel, bq=blk, bk=blk // 2, hps=hps),
        grid=(b, heads // hps, nq),
        in_specs=[
            pl.BlockSpec((w, blk), lambda bi, h, qi: (h, bi * nq + qi)),
            pl.BlockSpec((None, s, w), lambda bi, h, qi: (bi, 0, h)),
            pl.BlockSpec((w, s), lambda bi, h, qi: (heads // hps + h, bi)),
            pl.BlockSpec((None, hps, 1, s), lambda bi, h, qi: (bi, h, 0, 0)),
            pl.BlockSpec((1, w), lambda bi, h, qi: (0, h)),
        ],
        out_specs=pl.BlockSpec((None, blk, w), lambda bi, h, qi: (bi, qi, h)),
        out_shape=jax.ShapeDtypeStruct((b, s, heads * HEAD_DIM), BF16),
        scratch_shapes=[
            pltpu.VMEM((hps, s, 2 * HEAD_DIM), BF16),
            pltpu.VMEM((hps, 2 * HEAD_DIM, blk), BF16),
            pltpu.VMEM((2, hps, blk // 2, blk), F32),
            pltpu.VMEM((hps, 1, blk), F32),
            pltpu.VMEM((hps, 1, blk), F32),
            pltpu.VMEM((hps, HEAD_DIM, blk), F32),
        ],
        compiler_params=_params("parallel", "parallel", "arbitrary"),
        name="fox",
    )(qvt, k, qvt, c_row, gain)


def _ret_kernel(h_ref, w_ref, cos_ref, sin_ref, gain_ref, o_ref, p_sc, state_sc,
                *, rows, chunk, heads):
    n = pl.program_id(1)
    d = HEAD_DIM
    w = heads * d

    @pl.when(n == 0)
    def _():
        state_sc[...] = jnp.zeros_like(state_sc)

    p_sc[...] = _dot(h_ref[...], w_ref[...])
    t = lax.broadcasted_iota(jnp.int32, (chunk, 1), 0).astype(F32)
    r = lax.broadcasted_iota(jnp.int32, (chunk, chunk), 0)
    c = lax.broadcasted_iota(jnp.int32, (chunk, chunk), 1)
    diff = (r - c).astype(F32)
    for hh in range(heads):
        lg = float(np.log1p(-np.exp2(-(RET_GAMMA_BASE + hh))))
        lo, hi = hh * d, (hh + 1) * d
        xi = jnp.exp((t + 1.0) * lg)
        zeta = jnp.exp((chunk - 1.0 - t) * lg)
        decay = jnp.where(diff >= 0.0, jnp.exp(jnp.maximum(diff, 0.0) * lg), 0.0)
        for ci in range(rows // chunk):
            rs = slice(ci * chunk, (ci + 1) * chunk)
            cos = cos_ref[rs, :]
            sin = sin_ref[rs, :]

            def rope(x):
                return x * cos + pltpu.roll(x, d // 2, 1) * sin

            q = rope(p_sc[rs, lo:hi])
            k = rope(p_sc[rs, w + lo:w + hi]) * (d ** -0.5)
            v = p_sc[rs, 2 * w + lo:2 * w + hi].astype(BF16)
            gate = p_sc[rs, 3 * w + lo:3 * w + hi]
            qb = q.astype(BF16)
            scores = _dot_nt(qb, k.astype(BF16)) * decay
            state = state_sc[hh]
            o = _dot(scores.astype(BF16), v) + _dot(qb, state.astype(BF16)) * xi
            kz_t = jnp.transpose(k * zeta).astype(BF16)
            state_sc[hh] = state * float(np.exp(chunk * lg)) + _dot(kz_t, v)
            y = _rms(o, gain_ref[:, lo:hi]) * (gate * jax.nn.sigmoid(gate))
            o_ref[rs, lo:hi] = y.astype(o_ref.dtype)


def _retention(h, w_ret, layer, cos2, sin2, gain, heads, b, s):
    t, dm = h.shape
    rows = min(RET_ROWS, s)
    chunk = min(RET_CHUNK, rows)
    nc = s // rows
    w = heads * HEAD_DIM
    assert s % rows == 0 and rows % chunk == 0 and t == b * s
    return pl.pallas_call(
        functools.partial(_ret_kernel, rows=rows, chunk=chunk, heads=heads),
        grid=(b, nc),
        in_specs=[
            pl.BlockSpec((rows, dm), lambda bi, n: (bi * nc + n, 0)),
            pl.BlockSpec((None, dm, 4 * w), lambda bi, n: (layer, 0, 0)),
            pl.BlockSpec((rows, HEAD_DIM), lambda bi, n: (n, 0)),
            pl.BlockSpec((rows, HEAD_DIM), lambda bi, n: (n, 0)),
            pl.BlockSpec((1, w), lambda bi, n: (0, 0)),
        ],
        out_specs=pl.BlockSpec((rows, w), lambda bi, n: (bi * nc + n, 0)),
        out_shape=jax.ShapeDtypeStruct((t, w), BF16),
        scratch_shapes=[
            pltpu.VMEM((rows, 4 * w), F32),
            pltpu.VMEM((heads, HEAD_DIM, HEAD_DIM), F32),
        ],
        compiler_params=_params("parallel", "arbitrary"),
        name="retention",
    )(h, w_ret, cos2, sin2, gain)


def _gelu(x):
    return 0.5 * x * (1.0 + jnp.tanh(np.sqrt(2.0 / np.pi).astype(np.float32) * (x + 0.044715 * (x * x * x))))


def _gmlp_kernel(h_ref, w_ref, lng_ref, lnb_ref, ws_ref, bs_ref, gain_ref, o_ref, uv_ref,
                 *, groups, rows):
    blk = CHUNK_GMLP
    pair = 2 * HEAD_DIM
    for g in range(groups):
        uv_ref[:, g * pair:(g + 1) * pair] = _dot(h_ref[...], w_ref[:, g * pair:(g + 1) * pair])
    r = lax.broadcasted_iota(jnp.int32, (blk, blk), 0)
    c = lax.broadcasted_iota(jnp.int32, (blk, blk), 1)
    for g in range(groups):
        lo, hi = g * HEAD_DIM, (g + 1) * HEAD_DIM
        u_lo, v_lo = g * pair, g * pair + HEAD_DIM
        wm = jnp.where(r >= c, ws_ref[g], 0.0).astype(BF16)
        bias = bs_ref[g]
        v = _gelu(uv_ref[:, v_lo:v_lo + HEAD_DIM])
        mu = jnp.mean(v, axis=-1, keepdims=True)
        var = jnp.mean(jnp.square(v - mu), axis=-1, keepdims=True)
        v = ((v - mu) * lax.rsqrt(var + EPS) * lng_ref[:, lo:hi] + lnb_ref[:, lo:hi]).astype(BF16)
        v_wide = jnp.concatenate([v[ci * blk:(ci + 1) * blk] for ci in range(rows // blk)], axis=1)
        mixed = _dot(wm, v_wide) + bias
        for ci in range(rows // blk):
            rs = slice(ci * blk, (ci + 1) * blk)
            y = _gelu(uv_ref[rs, u_lo:u_lo + HEAD_DIM]) * mixed[:, ci * HEAD_DIM:(ci + 1) * HEAD_DIM]
            o_ref[rs, lo:hi] = _rms(y, gain_ref[:, lo:hi]).astype(o_ref.dtype)


def _gmlp(h, w_gm, layer, ln_g, ln_b, w_s, b_s, gain, groups):
    t, dm = h.shape
    w = groups * HEAD_DIM
    rows = min(GMLP_ROWS, t)
    assert t % rows == 0 and rows % CHUNK_GMLP == 0
    return pl.pallas_call(
        functools.partial(_gmlp_kernel, groups=groups, rows=rows),
        grid=(t // rows,),
        in_specs=[
            pl.BlockSpec((rows, dm), lambda i: (i, 0)),
            pl.BlockSpec((None, dm, 2 * w), lambda i: (layer, 0, 0)),
            pl.BlockSpec((1, w), lambda i: (0, 0)),
            pl.BlockSpec((1, w), lambda i: (0, 0)),
            pl.BlockSpec((groups, CHUNK_GMLP, CHUNK_GMLP), lambda i: (0, 0, 0)),
            pl.BlockSpec((groups, CHUNK_GMLP, 1), lambda i: (0, 0, 0)),
            pl.BlockSpec((1, w), lambda i: (0, 0)),
        ],
        out_specs=pl.BlockSpec((rows, w), lambda i: (i, 0)),
        out_shape=jax.ShapeDtypeStruct((t, w), BF16),
        scratch_shapes=[pltpu.VMEM((rows, 2 * w), F32)],
        compiler_params=_params("parallel"),
        name="gmlp",
    )(h, w_gm, ln_g, ln_b, w_s, b_s, gain)


def _outproj_kernel(x_ref, ya_ref, yb_ref, yc_ref, w_ref, o_ref):
    wa, wb = ya_ref.shape[1], yb_ref.shape[1]
    acc = _dot(ya_ref[...], w_ref[0:wa, :])
    acc += _dot(yb_ref[...], w_ref[wa:wa + wb, :])
    acc += _dot(yc_ref[...], w_ref[wa + wb:, :])
    o_ref[...] = x_ref[...] + acc


def _outproj(x, ya, yb, yc, w, layer):
    t, d = x.shape
    k = w.shape[-2]
    bm, bn = min(OUT_ROWS, t), min(OUT_COLS, d)
    assert t % bm == 0 and d % bn == 0
    return pl.pallas_call(
        _outproj_kernel,
        grid=(t // bm, d // bn),
        in_specs=[
            pl.BlockSpec((bm, bn), lambda i, j: (i, j)),
            pl.BlockSpec((bm, ya.shape[1]), lambda i, j: (i, 0)),
            pl.BlockSpec((bm, yb.shape[1]), lambda i, j: (i, 0)),
            pl.BlockSpec((bm, yc.shape[1]), lambda i, j: (i, 0)),
            pl.BlockSpec((None, k, bn), lambda i, j: (layer, 0, j)),
        ],
        out_specs=pl.BlockSpec((bm, bn), lambda i, j: (i, j)),
        out_shape=jax.ShapeDtypeStruct((t, d), F32),
        compiler_params=_params("parallel", "arbitrary"),
        name="outproj",
    )(x, ya, yb, yc, w)


def kernel(x, ffn1_norm, ffn1_w_gate, ffn1_w_up, ffn1_w_down, mix_norm, w_in, fox_b_f,
           gmlp_ln_g, gmlp_ln_b, gmlp_w_s, gmlp_b_s, out_norm, w_out, ffn2_norm,
           ffn2_w_gate, ffn2_w_up, ffn2_w_down, final_norm):
    b, s, d = x.shape
    depth = w_in.shape[0]
    t = b * s
    n_heads = d // HEAD_DIM
    fox_h, ret_h = n_heads // 2, n_heads // 4
    gm_g = n_heads - fox_h - ret_h
    fox_w, ret_w, gm_w = fox_h * HEAD_DIM, ret_h * HEAD_DIM, gm_g * HEAD_DIM

    o_fz = 3 * fox_w
    o_ret = o_fz + fox_h
    o_gm = o_ret + 4 * ret_w
    w_qv = jnp.concatenate([w_in[:, :, :fox_w], w_in[:, :, 2 * fox_w:o_fz]], axis=2).astype(BF16)
    w_kz = jnp.concatenate(
        [w_in[:, :, fox_w:2 * fox_w],
         jnp.pad(w_in[:, :, o_fz:o_ret], ((0, 0), (0, 0), (0, HEAD_DIM - fox_h)))], axis=2
    ).astype(BF16)
    w_ret = w_in[:, :, o_ret:o_gm].astype(BF16)
    w_gm = jnp.concatenate(
        [w_in[:, :, o_gm + side * gm_w + g * HEAD_DIM:o_gm + side * gm_w + (g + 1) * HEAD_DIM]
         for g in range(gm_g) for side in (0, 1)], axis=2).astype(BF16)
    w_o = w_out.astype(BF16)
    ffn_f32 = (ffn1_w_gate, ffn1_w_up, ffn1_w_down, ffn2_w_gate, ffn2_w_up, ffn2_w_down)
    ffn_w = [w[0].astype(BF16) for w in ffn_f32]
    fz_bias = jnp.pad(fox_b_f, ((0, 0), (0, HEAD_DIM - fox_h)))

    half = HEAD_DIM // 2
    pos = jnp.arange(s, dtype=F32)
    inv_freq = ROPE_BASE ** (-jnp.arange(half, dtype=F32) / half)
    ang = pos[:, None] * inv_freq[None, :]
    cos2 = jnp.concatenate([jnp.cos(ang), jnp.cos(ang)], axis=-1)
    sin2 = jnp.concatenate([-jnp.sin(ang), jnp.sin(ang)], axis=-1)

    xf = x.reshape(t, d)
    for l in range(depth):
        x1, h = _ffn(xf, ffn1_norm[l][None], *ffn_w[:3], mix_norm[l][None], emit_h=True)
        qvt = _proj_t(h, w_qv, l, BF16, scale=HEAD_DIM ** -0.5 * LOG2E, scaled_rows=fox_w)
        kf, c = _proj_kc(h, w_kz, fz_bias[l][None], l, fox_w, s)
        c = jnp.transpose(c.reshape(b, s, HEAD_DIM)[:, :, :fox_h], (0, 2, 1))
        gains = out_norm[l][None]
        ya = _fox(qvt, kf.reshape(b, s, fox_w), c[:, :, None, :], gains[:, :fox_w], fox_h)
        yb = _retention(h, w_ret, l, cos2, sin2, gains[:, fox_w:fox_w + ret_w], ret_h, b, s)
        yc = _gmlp(h, w_gm, l, gmlp_ln_g[l][None], gmlp_ln_b[l][None], gmlp_w_s[l],
                   gmlp_b_s[l][..., None], gains[:, fox_w + ret_w:], gm_g)
        x2 = _outproj(x1, ya.reshape(t, fox_w), yb, yc, w_o, l)

        last = l == depth - 1
        xf, *next_w = _ffn(x2, ffn2_norm[l][None], *ffn_w[3:], final_norm[None], norm_out=last,
                           cast=() if last else ffn_f32, cast_layer=l + 1)
        ffn_w = next_w or ffn_w
    return xf.reshape(b, s, d)
```

```python
import functools

import jax
import jax.numpy as jnp
import numpy as np
from jax import lax
from jax.experimental import pallas as pl
from jax.experimental.pallas import tpu as pltpu

F32 = jnp.float32
BF16 = jnp.bfloat16

HEAD_DIM = 128
CHUNK_GMLP = 128
ROPE_BASE = 10000.0
RET_GAMMA_BASE = 5.0
EPS = 1e-6

V7X_VMEM_BYTES = 64 * 1024 * 1024
VMEM_LIMIT = V7X_VMEM_BYTES - 1 * 1024 * 1024

FFN_ROWS = 1024
FFN_PROLOGUE_ROWS = 256
FFN_EPILOGUE_ROWS = 128
FFN_COLS = 512
MM_ROWS = 1024
MM_COLS = 1024
OUT_ROWS = 2048
OUT_COLS = 512
FOX_Q_ROWS = 1024
FOX_HEADS_PER_STEP = 4
RET_ROWS = 512
RET_CHUNK = 256
GMLP_ROWS = 512


def _params(*sem):
    return pltpu.CompilerParams(dimension_semantics=sem, vmem_limit_bytes=VMEM_LIMIT)


def _rms(x, gain):
    return x * lax.rsqrt(jnp.mean(x * x, axis=-1, keepdims=True) + EPS) * gain


def _dot(a, b):
    return jnp.dot(a, b, preferred_element_type=F32)


def _dot_nt(a, b):
    return lax.dot_general(a, b, (((1,), (1,)), ((), ())), preferred_element_type=F32)


def _ffn_kernel(*refs, emit_h, norm_out, n_cast):
    x_ref, g_ref, wg_ref, wu_ref, wd_ref, g2_ref = refs[:6]
    cast_in = refs[6:6 + n_cast]
    outs, xn_sc = refs[6 + n_cast:-1], refs[-1]
    acc_ref = outs[0]
    cast_out = outs[1 + int(emit_h):]
    j = pl.program_id(1)

    def strips(rows):
        return x_ref.shape[0] // rows, lambda r: pl.ds(pl.multiple_of(r * rows, rows), rows)

    @pl.when(j == 0)
    def _():
        n, strip = strips(FFN_PROLOGUE_ROWS)

        def body(r, carry):
            xn_sc[strip(r), :] = _rms(x_ref[strip(r), :], g_ref[...]).astype(BF16)
            acc_ref[strip(r), :] = jnp.zeros((FFN_PROLOGUE_ROWS, acc_ref.shape[1]), F32)
            return carry

        lax.fori_loop(0, n, body, 0)

    xn = xn_sc[...]
    a = _dot(xn, wg_ref[...])
    b = _dot(xn, wu_ref[...])
    hmid = (a * jax.nn.sigmoid(a) * b).astype(BF16)
    acc_ref[...] += _dot(hmid, wd_ref[...])
    for src, dst in zip(cast_in, cast_out):
        dst[...] = src[...].astype(BF16)

    @pl.when(j == pl.num_programs(1) - 1)
    def _():
        n, strip = strips(FFN_EPILOGUE_ROWS)

        def body(r, carry):
            y = x_ref[strip(r), :] + 0.5 * acc_ref[strip(r), :]
            acc_ref[strip(r), :] = _rms(y, g2_ref[...]) if norm_out else y
            if emit_h:
                outs[1][strip(r), :] = _rms(y, g2_ref[...]).astype(BF16)
            return carry

        lax.fori_loop(0, n, body, 0)


def _ffn(x, gain, wg, wu, wd, gain2, *, emit_h=False, norm_out=False, cast=(), cast_layer=0):
    t, d = x.shape
    f = wg.shape[-1]
    bm, tf = min(FFN_ROWS, t), FFN_COLS
    assert t % bm == 0 and f % tf == 0
    ni, nj = t // bm, f // tf
    in_specs = [
        pl.BlockSpec((bm, d), lambda i, j: (i, 0)),
        pl.BlockSpec((1, d), lambda i, j: (0, 0)),
        pl.BlockSpec((d, tf), lambda i, j: (0, j)),
        pl.BlockSpec((d, tf), lambda i, j: (0, j)),
        pl.BlockSpec((tf, d), lambda i, j: (j, 0)),
        pl.BlockSpec((1, d), lambda i, j: (0, 0)),
    ]
    out_shape = [jax.ShapeDtypeStruct((t, d), F32)]
    out_specs = [pl.BlockSpec((bm, d), lambda i, j: (i, 0))]
    if emit_h:
        out_shape.append(jax.ShapeDtypeStruct((t, d), BF16))
        out_specs.append(pl.BlockSpec((bm, d), lambda i, j: (i, 0), pipeline_mode=pl.Buffered(1)))
    for w in cast:
        _, r, c = w.shape
        if r == d:
            tile, idx = (r // ni, c // nj), lambda i, j: (i, j)
        else:
            tile, idx = (r // nj, c // ni), lambda i, j: (j, i)
        assert tile[0] % 16 == 0 and tile[1] % 128 == 0
        in_specs.append(
            pl.BlockSpec((None,) + tile, lambda i, j, idx=idx: (cast_layer,) + idx(i, j)))
        out_specs.append(pl.BlockSpec(tile, idx))
        out_shape.append(jax.ShapeDtypeStruct((r, c), BF16))
    return pl.pallas_call(
        functools.partial(_ffn_kernel, emit_h=emit_h, norm_out=norm_out, n_cast=len(cast)),
        grid=(ni, nj),
        in_specs=in_specs,
        out_specs=out_specs,
        out_shape=out_shape,
        scratch_shapes=[pltpu.VMEM((bm, d), BF16)],
        compiler_params=_params("parallel", "arbitrary"),
        name="ffn",
    )(x, gain, wg, wu, wd, gain2, *cast)


def _proj_t_kernel(h_ref, w_ref, o_ref, *, scale, scaled_rows, bn):
    y = _dot(h_ref[...], w_ref[...])
    if scaled_rows:
        j = pl.program_id(1)
        y = y * jnp.where(j * bn < scaled_rows, scale, 1.0).astype(F32)
    o_ref[...] = jnp.transpose(y).astype(o_ref.dtype)


def _proj_t(h, w, layer, out_dtype, *, scale=1.0, scaled_rows=0):
    t, d = h.shape
    n = w.shape[-1]
    bm, bn = min(MM_ROWS, t), min(MM_COLS, n)
    assert t % bm == 0 and n % bn == 0 and scaled_rows % bn == 0
    return pl.pallas_call(
        functools.partial(_proj_t_kernel, scale=scale, scaled_rows=scaled_rows, bn=bn),
        grid=(t // bm, n // bn),
        in_specs=[
            pl.BlockSpec((bm, d), lambda i, j: (i, 0)),
            pl.BlockSpec((None, d, bn), lambda i, j: (layer, 0, j)),
        ],
        out_specs=pl.BlockSpec((bn, bm), lambda i, j: (j, i)),
        out_shape=jax.ShapeDtypeStruct((n, t), out_dtype),
        compiler_params=_params("parallel", "arbitrary"),
        name="proj_t",
    )(h, w)


def _proj_kc_kernel(h_ref, w_ref, b_ref, k_ref, c_ref, carry_sc, *, blocks_per_seq):
    y = _dot(h_ref[...], w_ref[...])
    nk = k_ref.shape[1]
    k_ref[...] = y[:, :nk].astype(k_ref.dtype)

    @pl.when(pl.program_id(0) % blocks_per_seq == 0)
    def _():
        carry_sc[...] = jnp.zeros_like(carry_sc)

    blk = 128
    r = lax.broadcasted_iota(jnp.int32, (blk, blk), 0)
    c = lax.broadcasted_iota(jnp.int32, (blk, blk), 1)
    tril = jnp.where(r >= c, 1.0, 0.0).astype(F32)
    carry = carry_sc[...]
    for i in range(h_ref.shape[0] // blk):
        z = y[i * blk:(i + 1) * blk, nk:] + b_ref[...]
        log_f = jnp.minimum(z, 0.0) - jnp.log1p(jnp.exp(-jnp.abs(z)))
        cs = jnp.dot(tril, log_f, precision=lax.Precision.HIGHEST,
                     preferred_element_type=F32) + carry
        c_ref[i * blk:(i + 1) * blk, :] = cs
        carry = cs[blk - 1:blk, :]
    carry_sc[...] = carry


def _proj_kc(h, w, bias, layer, nk, s):
    t, d = h.shape
    n = w.shape[-1]
    bm = min(MM_ROWS, s)
    assert s % bm == 0 and bm % 128 == 0
    return pl.pallas_call(
        functools.partial(_proj_kc_kernel, blocks_per_seq=s // bm),
        grid=(t // bm,),
        in_specs=[
            pl.BlockSpec((bm, d), lambda i: (i, 0)),
            pl.BlockSpec((None, d, n), lambda i: (layer, 0, 0)),
            pl.BlockSpec((1, n - nk), lambda i: (0, 0)),
        ],
        out_specs=[
            pl.BlockSpec((bm, nk), lambda i: (i, 0)),
            pl.BlockSpec((bm, n - nk), lambda i: (i, 0)),
        ],
        out_shape=[
            jax.ShapeDtypeStruct((t, nk), BF16),
            jax.ShapeDtypeStruct((t, n - nk), F32),
        ],
        scratch_shapes=[pltpu.VMEM((1, n - nk), F32)],
        compiler_params=_params("arbitrary"),
        name="proj_kc",
    )(h, w, bias)


LOG2E = float(np.log2(np.e))
BIAS_TERMS = 3
AUG_ROWS = 16


def _split_bf16(x):
    hi = x.astype(BF16).astype(F32)
    r = x - hi
    mid = r.astype(BF16).astype(F32)
    lo = (r - mid).astype(BF16).astype(F32)
    return hi, mid, lo


def _bias_rows(c, idx, sign):
    hi, mid, lo = _split_bf16(c)
    first = 0 if sign > 0 else BIAS_TERMS
    ones_first = BIAS_TERMS if sign > 0 else 0
    out = jnp.where((idx >= ones_first) & (idx < ones_first + BIAS_TERMS), 1.0, 0.0)
    out = jnp.where(idx == first, sign * hi, out)
    out = jnp.where(idx == first + 1, sign * mid, out)
    out = jnp.where(idx == first + 2, sign * lo, out)
    return out.astype(BF16)


def _fox_kernel(qt_ref, k_ref, vt_ref, c_ref, g_ref, o_ref, kaug_sc, qaug_sc, s_sc, m_sc, l_sc, acc_sc,
                *, bq, bk, hps):
    qi = pl.program_id(2)
    s_len = k_ref.shape[0]
    d = HEAD_DIM

    @pl.when(qi == 0)
    def _():
        lane = lax.broadcasted_iota(jnp.int32, (d, d), 1)
        for hh in range(hps):
            kaug_sc[hh, :, 0:d] = k_ref[:, hh * d:(hh + 1) * d]
            for i in range(s_len // d):
                c_row = c_ref[hh, :, i * d:(i + 1) * d] * LOG2E
                c_col = jnp.transpose(jnp.broadcast_to(c_row, (d, d)))
                kaug_sc[hh, i * d:(i + 1) * d, d:2 * d] = _bias_rows(c_col, lane, -1.0)

    q0 = pl.multiple_of(qi * bq, bq)
    sub = lax.broadcasted_iota(jnp.int32, (AUG_ROWS, bq), 0)
    for hh in range(hps):
        cq = jnp.broadcast_to(c_ref[hh, :, pl.ds(q0, bq)] * LOG2E, (AUG_ROWS, bq))
        qaug_sc[hh, 0:d, :] = qt_ref[hh * d:(hh + 1) * d, :]
        qaug_sc[hh, d:d + AUG_ROWS, :] = _bias_rows(cq, sub, 1.0)
        qaug_sc[hh, d + AUG_ROWS:, :] = jnp.zeros((d - AUG_ROWS, bq), BF16)
    m_sc[...] = jnp.full_like(m_sc, -jnp.inf)
    l_sc[...] = jnp.zeros_like(l_sc)
    acc_sc[...] = jnp.zeros_like(acc_sc)

    def scores(ki, slot, lo, width):
        k0 = pl.multiple_of(ki * bk, bk)
        for hh in range(hps):
            s_sc[slot, hh, :, lo:lo + width] = _dot(kaug_sc[hh, pl.ds(k0, bk), :],
                                                    qaug_sc[hh, :, lo:lo + width])

    def absorb(ki, slot, lo, width, masked):
        cols = slice(lo, lo + width)
        k0 = pl.multiple_of(ki * bk, bk)
        for hh in range(hps):
            s = s_sc[slot, hh, :, cols]
            if masked:
                r = lax.broadcasted_iota(jnp.int32, (bk, width), 0)
                c = lax.broadcasted_iota(jnp.int32, (bk, width), 1)
                s = jnp.where(r - c <= q0 + lo - k0, s, -jnp.inf)
            m_prev = m_sc[hh, :, cols]
            m_new = jnp.maximum(m_prev, jnp.max(s, axis=0, keepdims=True))
            alpha = jnp.exp2(m_prev - m_new)
            p = jnp.exp2(s - m_new)
            l_sc[hh, :, cols] = alpha * l_sc[hh, :, cols] + jnp.sum(p, axis=0, keepdims=True)
            pv = _dot(vt_ref[hh * d:(hh + 1) * d, pl.ds(k0, bk)], p.astype(BF16))
            acc_sc[hh, :, cols] = alpha * acc_sc[hh, :, cols] + pv
            m_sc[hh, :, cols] = m_new

    scores(0, 0, 0, bq)

    def body(pair, carry):
        ki = 2 * pair
        scores(ki + 1, 1, 0, bq)
        absorb(ki, 0, 0, bq, False)
        scores(ki + 2, 0, 0, bq)
        absorb(ki + 1, 1, 0, bq, False)
        return carry

    lax.fori_loop(0, qi, body, 0)
    scores(2 * qi + 1, 1, bk, bk)
    absorb(2 * qi, 0, 0, bq, True)
    absorb(2 * qi + 1, 1, bk, bk, True)
    for hh in range(hps):
        y_t = acc_sc[hh] * (1.0 / l_sc[hh])
        y_t = y_t * lax.rsqrt(jnp.mean(y_t * y_t, axis=0, keepdims=True) + EPS)
        o_ref[:, hh * d:(hh + 1) * d] = (
            jnp.transpose(y_t) * g_ref[:, hh * d:(hh + 1) * d]).astype(o_ref.dtype)


def _fox(qvt, k, c_row, gain, heads):
    b, s, _ = k.shape
    blk = min(FOX_Q_ROWS, s)
    nq = s // blk
    hps = FOX_HEADS_PER_STEP
    w = hps * HEAD_DIM
    assert s % blk == 0 and blk % 2 == 0 and heads % hps == 0
    return pl.pallas_call(
        functools.partial(_fox_kernel, bq=blk, bk=blk // 2, hps=hps),
        grid=(b, heads // hps, nq),
        in_specs=[
            pl.BlockSpec((w, blk), lambda bi, h, qi: (h, bi * nq + qi)),
            pl.BlockSpec((None, s, w), lambda bi, h, qi: (bi, 0, h)),
            pl.BlockSpec((w, s), lambda bi, h, qi: (heads // hps + h, bi)),
            pl.BlockSpec((None, hps, 1, s), lambda bi, h, qi: (bi, h, 0, 0)),
            pl.BlockSpec((1, w), lambda bi, h, qi: (0, h)),
        ],
        out_specs=pl.BlockSpec((None, blk, w), lambda bi, h, qi: (bi, qi, h)),
        out_shape=jax.ShapeDtypeStruct((b, s, heads * HEAD_DIM), BF16),
        scratch_shapes=[
            pltpu.VMEM((hps, s, 2 * HEAD_DIM), BF16),
            pltpu.VMEM((hps, 2 * HEAD_DIM, blk), BF16),
            pltpu.VMEM((2, hps, blk // 2, blk), F32),
            pltpu.VMEM((hps, 1, blk), F32),
            pltpu.VMEM((hps, 1, blk), F32),
            pltpu.VMEM((hps, HEAD_DIM, blk), F32),
        ],
        compiler_params=_params("parallel", "parallel", "arbitrary"),
        name="fox",
    )(qvt, k, qvt, c_row, gain)


def _ret_kernel(h_ref, w_ref, cos_ref, sin_ref, gain_ref, o_ref, p_sc, state_sc,
                *, rows, chunk, heads):
    n = pl.program_id(1)
    d = HEAD_DIM
    w = heads * d

    @pl.when(n == 0)
    def _():
        state_sc[...] = jnp.zeros_like(state_sc)

    p_sc[...] = _dot(h_ref[...], w_ref[...])
    t = lax.broadcasted_iota(jnp.int32, (chunk, 1), 0).astype(F32)
    r = lax.broadcasted_iota(jnp.int32, (chunk, chunk), 0)
    c = lax.broadcasted_iota(jnp.int32, (chunk, chunk), 1)
    diff = (r - c).astype(F32)
    for hh in range(heads):
        lg = float(np.log1p(-np.exp2(-(RET_GAMMA_BASE + hh))))
        lo, hi = hh * d, (hh + 1) * d
        xi = jnp.exp((t + 1.0) * lg)
        zeta = jnp.exp((chunk - 1.0 - t) * lg)
        decay = jnp.where(diff >= 0.0, jnp.exp(jnp.maximum(diff, 0.0) * lg), 0.0)
        for ci in range(rows // chunk):
            rs = slice(ci * chunk, (ci + 1) * chunk)
            cos = cos_ref[rs, :]
            sin = sin_ref[rs, :]

            def rope(x):
                return x * cos + pltpu.roll(x, d // 2, 1) * sin

            q = rope(p_sc[rs, lo:hi])
            k = rope(p_sc[rs, w + lo:w + hi]) * (d ** -0.5)
            v = p_sc[rs, 2 * w + lo:2 * w + hi].astype(BF16)
            gate = p_sc[rs, 3 * w + lo:3 * w + hi]
            qb = q.astype(BF16)
            scores = _dot_nt(qb, k.astype(BF16)) * decay
            state = state_sc[hh]
            o = _dot(scores.astype(BF16), v) + _dot(qb, state.astype(BF16)) * xi
            kz_t = jnp.transpose(k * zeta).astype(BF16)
            state_sc[hh] = state * float(np.exp(chunk * lg)) + _dot(kz_t, v)
            y = _rms(o, gain_ref[:, lo:hi]) * (gate * jax.nn.sigmoid(gate))
            o_ref[rs, lo:hi] = y.astype(o_ref.dtype)


def _retention(h, w_ret, layer, cos2, sin2, gain, heads, b, s):
    t, dm = h.shape
    rows = min(RET_ROWS, s)
    chunk = min(RET_CHUNK, rows)
    nc = s // rows
    w = heads * HEAD_DIM
    assert s % rows == 0 and rows % chunk == 0 and t == b * s
    return pl.pallas_call(
        functools.partial(_ret_kernel, rows=rows, chunk=chunk, heads=heads),
        grid=(b, nc),
        in_specs=[
            pl.BlockSpec((rows, dm), lambda bi, n: (bi * nc + n, 0)),
            pl.BlockSpec((None, dm, 4 * w), lambda bi, n: (layer, 0, 0)),
            pl.BlockSpec((rows, HEAD_DIM), lambda bi, n: (n, 0)),
            pl.BlockSpec((rows, HEAD_DIM), lambda bi, n: (n, 0)),
            pl.BlockSpec((1, w), lambda bi, n: (0, 0)),
        ],
        out_specs=pl.BlockSpec((rows, w), lambda bi, n: (bi * nc + n, 0)),
        out_shape=jax.ShapeDtypeStruct((t, w), BF16),
        scratch_shapes=[
            pltpu.VMEM((rows, 4 * w), F32),
            pltpu.VMEM((heads, HEAD_DIM, HEAD_DIM), F32),
        ],
        compiler_params=_params("parallel", "arbitrary"),
        name="retention",
    )(h, w_ret, cos2, sin2, gain)


def _gelu(x):
    return 0.5 * x * (1.0 + jnp.tanh(np.sqrt(2.0 / np.pi).astype(np.float32) * (x + 0.044715 * (x * x * x))))


def _gmlp_kernel(h_ref, w_ref, lng_ref, lnb_ref, ws_ref, bs_ref, gain_ref, o_ref, uv_ref,
                 *, groups, rows):
    blk = CHUNK_GMLP
    pair = 2 * HEAD_DIM
    for g in range(groups):
        uv_ref[:, g * pair:(g + 1) * pair] = _dot(h_ref[...], w_ref[:, g * pair:(g + 1) * pair])
    r = lax.broadcasted_iota(jnp.int32, (blk, blk), 0)
    c = lax.broadcasted_iota(jnp.int32, (blk, blk), 1)
    for g in range(groups):
        lo, hi = g * HEAD_DIM, (g + 1) * HEAD_DIM
        u_lo, v_lo = g * pair, g * pair + HEAD_DIM
        wm = jnp.where(r >= c, ws_ref[g], 0.0).astype(BF16)
        bias = bs_ref[g]
        v = _gelu(uv_ref[:, v_lo:v_lo + HEAD_DIM])
        mu = jnp.mean(v, axis=-1, keepdims=True)
        var = jnp.mean(jnp.square(v - mu), axis=-1, keepdims=True)
        v = ((v - mu) * lax.rsqrt(var + EPS) * lng_ref[:, lo:hi] + lnb_ref[:, lo:hi]).astype(BF16)
        v_wide = jnp.concatenate([v[ci * blk:(ci + 1) * blk] for ci in range(rows // blk)], axis=1)
        mixed = _dot(wm, v_wide) + bias
        for ci in range(rows // blk):
            rs = slice(ci * blk, (ci + 1) * blk)
            y = _gelu(uv_ref[rs, u_lo:u_lo + HEAD_DIM]) * mixed[:, ci * HEAD_DIM:(ci + 1) * HEAD_DIM]
            o_ref[rs, lo:hi] = _rms(y, gain_ref[:, lo:hi]).astype(o_ref.dtype)


def _gmlp(h, w_gm, layer, ln_g, ln_b, w_s, b_s, gain, groups):
    t, dm = h.shape
    w = groups * HEAD_DIM
    rows = min(GMLP_ROWS, t)
    assert t % rows == 0 and rows % CHUNK_GMLP == 0
    return pl.pallas_call(
        functools.partial(_gmlp_kernel, groups=groups, rows=rows),
        grid=(t // rows,),
        in_specs=[
            pl.BlockSpec((rows, dm), lambda i: (i, 0)),
            pl.BlockSpec((None, dm, 2 * w), lambda i: (layer, 0, 0)),
            pl.BlockSpec((1, w), lambda i: (0, 0)),
            pl.BlockSpec((1, w), lambda i: (0, 0)),
            pl.BlockSpec((groups, CHUNK_GMLP, CHUNK_GMLP), lambda i: (0, 0, 0)),
            pl.BlockSpec((groups, CHUNK_GMLP, 1), lambda i: (0, 0, 0)),
            pl.BlockSpec((1, w), lambda i: (0, 0)),
        ],
        out_specs=pl.BlockSpec((rows, w), lambda i: (i, 0)),
        out_shape=jax.ShapeDtypeStruct((t, w), BF16),
        scratch_shapes=[pltpu.VMEM((rows, 2 * w), F32)],
        compiler_params=_params("parallel"),
        name="gmlp",
    )(h, w_gm, ln_g, ln_b, w_s, b_s, gain)


def _outproj_kernel(x_ref, ya_ref, yb_ref, yc_ref, w_ref, o_ref):
    wa, wb = ya_ref.shape[1], yb_ref.shape[1]
    acc = _dot(ya_ref[...], w_ref[0:wa, :])
    acc += _dot(yb_ref[...], w_ref[wa:wa + wb, :])
    acc += _dot(yc_ref[...], w_ref[wa + wb:, :])
    o_ref[...] = x_ref[...] + acc


def _outproj(x, ya, yb, yc, w, layer):
    t, d = x.shape
    k = w.shape[-2]
    bm, bn = min(OUT_ROWS, t), min(OUT_COLS, d)
    assert t % bm == 0 and d % bn == 0
    return pl.pallas_call(
        _outproj_kernel,
        grid=(t // bm, d // bn),
        in_specs=[
            pl.BlockSpec((bm, bn), lambda i, j: (i, j)),
            pl.BlockSpec((bm, ya.shape[1]), lambda i, j: (i, 0)),
            pl.BlockSpec((bm, yb.shape[1]), lambda i, j: (i, 0)),
            pl.BlockSpec((bm, yc.shape[1]), lambda i, j: (i, 0)),
            pl.BlockSpec((None, k, bn), lambda i, j: (layer, 0, j)),
        ],
        out_specs=pl.BlockSpec((bm, bn), lambda i, j: (i, j)),
        out_shape=jax.ShapeDtypeStruct((t, d), F32),
        compiler_params=_params("parallel", "arbitrary"),
        name="outproj",
    )(x, ya, yb, yc, w)


def kernel(x, ffn1_norm, ffn1_w_gate, ffn1_w_up, ffn1_w_down, mix_norm, w_in, fox_b_f,
           gmlp_ln_g, gmlp_ln_b, gmlp_w_s, gmlp_b_s, out_norm, w_out, ffn2_norm,
           ffn2_w_gate, ffn2_w_up, ffn2_w_down, final_norm):
    b, s, d = x.shape
    depth = w_in.shape[0]
    t = b * s
    n_heads = d // HEAD_DIM
    fox_h, ret_h = n_heads // 2, n_heads // 4
    gm_g = n_heads - fox_h - ret_h
    fox_w, ret_w, gm_w = fox_h * HEAD_DIM, ret_h * HEAD_DIM, gm_g * HEAD_DIM

    o_fz = 3 * fox_w
    o_ret = o_fz + fox_h
    o_gm = o_ret + 4 * ret_w
    w_qv = jnp.concatenate([w_in[:, :, :fox_w], w_in[:, :, 2 * fox_w:o_fz]], axis=2).astype(BF16)
    w_kz = jnp.concatenate(
        [w_in[:, :, fox_w:2 * fox_w],
         jnp.pad(w_in[:, :, o_fz:o_ret], ((0, 0), (0, 0), (0, HEAD_DIM - fox_h)))], axis=2
    ).astype(BF16)
    w_ret = w_in[:, :, o_ret:o_gm].astype(BF16)
    w_gm = jnp.concatenate(
        [w_in[:, :, o_gm + side * gm_w + g * HEAD_DIM:o_gm + side * gm_w + (g + 1) * HEAD_DIM]
         for g in range(gm_g) for side in (0, 1)], axis=2).astype(BF16)
    w_o = w_out.astype(BF16)
    ffn1_f32 = (ffn1_w_gate, ffn1_w_up, ffn1_w_down)
    ffn2_f32 = (ffn2_w_gate, ffn2_w_up, ffn2_w_down)
    ffn1_w = [w[0].astype(BF16) for w in ffn1_f32]
    fz_bias = jnp.pad(fox_b_f, ((0, 0), (0, HEAD_DIM - fox_h)))

    half = HEAD_DIM // 2
    pos = jnp.arange(s, dtype=F32)
    inv_freq = ROPE_BASE ** (-jnp.arange(half, dtype=F32) / half)
    ang = pos[:, None] * inv_freq[None, :]
    cos2 = jnp.concatenate([jnp.cos(ang), jnp.cos(ang)], axis=-1)
    sin2 = jnp.concatenate([-jnp.sin(ang), jnp.sin(ang)], axis=-1)

    xf = x.reshape(t, d)
    for l in range(depth):
        x1, h, *ffn2_w = _ffn(xf, ffn1_norm[l][None], *ffn1_w, mix_norm[l][None], emit_h=True,
                              cast=ffn2_f32, cast_layer=l)
        qvt = _proj_t(h, w_qv, l, BF16, scale=HEAD_DIM ** -0.5 * LOG2E, scaled_rows=fox_w)
        kf, c = _proj_kc(h, w_kz, fz_bias[l][None], l, fox_w, s)
        c = jnp.transpose(c.reshape(b, s, HEAD_DIM)[:, :, :fox_h], (0, 2, 1))
        gains = out_norm[l][None]
        ya = _fox(qvt, kf.reshape(b, s, fox_w), c[:, :, None, :], gains[:, :fox_w], fox_h)
        yb = _retention(h, w_ret, l, cos2, sin2, gains[:, fox_w:fox_w + ret_w], ret_h, b, s)
        yc = _gmlp(h, w_gm, l, gmlp_ln_g[l][None], gmlp_ln_b[l][None], gmlp_w_s[l],
                   gmlp_b_s[l][..., None], gains[:, fox_w + ret_w:], gm_g)
        x2 = _outproj(x1, ya.reshape(t, fox_w), yb, yc, w_o, l)

        last = l == depth - 1
        xf, *ffn1_w = _ffn(x2, ffn2_norm[l][None], *ffn2_w, final_norm[None], norm_out=last,
                           cast=() if last else ffn1_f32, cast_layer=l + 1)
    return xf.reshape(b, s, d)
```

```python
import functools

import jax
import jax.numpy as jnp
import numpy as np
from jax import lax
from jax.experimental import pallas as pl
from jax.experimental.pallas import tpu as pltpu

F32 = jnp.float32
BF16 = jnp.bfloat16

HEAD_DIM = 128
CHUNK_GMLP = 128
ROPE_BASE = 10000.0
RET_GAMMA_BASE = 5.0
EPS = 1e-6

V7X_VMEM_BYTES = 64 * 1024 * 1024
VMEM_LIMIT = V7X_VMEM_BYTES - 1 * 1024 * 1024
LANES = 128
BF16_SUBLANES = 16
CUMSUM_ROWS = 128

FFN_ROWS = 1024
FFN_PROLOGUE_ROWS = 256
FFN_EPILOGUE_ROWS = 128
FFN_COLS = 512
MM_ROWS = 1024
MM_COLS = 1024
OUT_ROWS = 2048
OUT_COLS = 512
FOX_Q_ROWS = 1024
FOX_HEADS_PER_STEP = 4
RET_ROWS = 1024
RET_CHUNK = 256
GMLP_ROWS = 1024


def _params(*sem):
    return pltpu.CompilerParams(dimension_semantics=sem, vmem_limit_bytes=VMEM_LIMIT)


def _rms(x, gain):
    return x * lax.rsqrt(jnp.mean(x * x, axis=-1, keepdims=True) + EPS) * gain


def _dot(a, b):
    return jnp.dot(a, b, preferred_element_type=F32)


def _dot_nt(a, b):
    return lax.dot_general(a, b, (((1,), (1,)), ((), ())), preferred_element_type=F32)


def _ffn_kernel(*refs, emit_h, norm_out, n_cast):
    x_ref, g_ref, wg_ref, wu_ref, wd_ref, g2_ref = refs[:6]
    cast_in = refs[6:6 + n_cast]
    outs, xn_sc = refs[6 + n_cast:-1], refs[-1]
    acc_ref = outs[0]
    cast_out = outs[1 + int(emit_h):]
    j = pl.program_id(1)

    def strips(rows):
        return x_ref.shape[0] // rows, lambda r: pl.ds(pl.multiple_of(r * rows, rows), rows)

    @pl.when(j == 0)
    def _():
        n, strip = strips(FFN_PROLOGUE_ROWS)

        def body(r, carry):
            xn_sc[strip(r), :] = _rms(x_ref[strip(r), :], g_ref[...]).astype(BF16)
            acc_ref[strip(r), :] = jnp.zeros((FFN_PROLOGUE_ROWS, acc_ref.shape[1]), F32)
            return carry

        lax.fori_loop(0, n, body, 0)

    xn = xn_sc[...]
    a = _dot(xn, wg_ref[...])
    b = _dot(xn, wu_ref[...])
    hmid = (a * jax.nn.sigmoid(a) * b).astype(BF16)
    acc_ref[...] += _dot(hmid, wd_ref[...])
    for src, dst in zip(cast_in, cast_out):
        dst[...] = src[...].astype(BF16)

    @pl.when(j == pl.num_programs(1) - 1)
    def _():
        n, strip = strips(FFN_EPILOGUE_ROWS)

        def body(r, carry):
            y = x_ref[strip(r), :] + 0.5 * acc_ref[strip(r), :]
            acc_ref[strip(r), :] = _rms(y, g2_ref[...]) if norm_out else y
            if emit_h:
                outs[1][strip(r), :] = _rms(y, g2_ref[...]).astype(BF16)
            return carry

        lax.fori_loop(0, n, body, 0)


def _ffn(x, gain, wg, wu, wd, gain2, *, emit_h=False, norm_out=False, cast=(), cast_layer=0):
    t, d = x.shape
    f = wg.shape[-1]
    bm, tf = min(FFN_ROWS, t), FFN_COLS
    assert t % bm == 0 and f % tf == 0
    ni, nj = t // bm, f // tf
    in_specs = [
        pl.BlockSpec((bm, d), lambda i, j: (i, 0)),
        pl.BlockSpec((1, d), lambda i, j: (0, 0)),
        pl.BlockSpec((d, tf), lambda i, j: (0, j)),
        pl.BlockSpec((d, tf), lambda i, j: (0, j)),
        pl.BlockSpec((tf, d), lambda i, j: (j, 0)),
        pl.BlockSpec((1, d), lambda i, j: (0, 0)),
    ]
    out_shape = [jax.ShapeDtypeStruct((t, d), F32)]
    if emit_h:
        out_shape.append(jax.ShapeDtypeStruct((t, d), BF16))
    out_specs = [pl.BlockSpec((bm, d), lambda i, j: (i, 0)) for _ in out_shape]
    for w in cast:
        _, r, c = w.shape
        if r == d:
            tile, idx = (r // ni, c // nj), lambda i, j: (i, j)
        else:
            tile, idx = (r // nj, c // ni), lambda i, j: (j, i)
        assert tile[0] % BF16_SUBLANES == 0 and tile[1] % LANES == 0
        in_specs.append(
            pl.BlockSpec((None,) + tile, lambda i, j, idx=idx: (cast_layer,) + idx(i, j)))
        out_specs.append(pl.BlockSpec(tile, idx))
        out_shape.append(jax.ShapeDtypeStruct((r, c), BF16))
    return pl.pallas_call(
        functools.partial(_ffn_kernel, emit_h=emit_h, norm_out=norm_out, n_cast=len(cast)),
        grid=(ni, nj),
        in_specs=in_specs,
        out_specs=out_specs,
        out_shape=out_shape,
        scratch_shapes=[pltpu.VMEM((bm, d), BF16)],
        compiler_params=_params("parallel", "arbitrary"),
        name="ffn",
    )(x, gain, wg, wu, wd, gain2, *cast)


def _proj_t_kernel(h_ref, w_ref, o_ref, *, scale, scaled_rows, bn):
    y = _dot(h_ref[...], w_ref[...])
    if scaled_rows:
        j = pl.program_id(1)
        y = y * jnp.where(j * bn < scaled_rows, scale, 1.0).astype(F32)
    o_ref[...] = jnp.transpose(y).astype(o_ref.dtype)


def _proj_t(h, w, layer, out_dtype, *, scale=1.0, scaled_rows=0):
    t, d = h.shape
    n = w.shape[-1]
    bm, bn = min(MM_ROWS, t), min(MM_COLS, n)
    assert t % bm == 0 and n % bn == 0 and scaled_rows % bn == 0
    return pl.pallas_call(
        functools.partial(_proj_t_kernel, scale=scale, scaled_rows=scaled_rows, bn=bn),
        grid=(t // bm, n // bn),
        in_specs=[
            pl.BlockSpec((bm, d), lambda i, j: (i, 0)),
            pl.BlockSpec((None, d, bn), lambda i, j: (layer, 0, j)),
        ],
        out_specs=pl.BlockSpec((bn, bm), lambda i, j: (j, i)),
        out_shape=jax.ShapeDtypeStruct((n, t), out_dtype),
        compiler_params=_params("parallel", "arbitrary"),
        name="proj_t",
    )(h, w)


def _proj_kc_kernel(h_ref, w_ref, b_ref, k_ref, c_ref, carry_sc, *, blocks_per_seq):
    y = _dot(h_ref[...], w_ref[...])
    nk = k_ref.shape[1]
    k_ref[...] = y[:, :nk].astype(k_ref.dtype)

    @pl.when(pl.program_id(0) % blocks_per_seq == 0)
    def _():
        carry_sc[...] = jnp.zeros_like(carry_sc)

    blk = CUMSUM_ROWS
    r = lax.broadcasted_iota(jnp.int32, (blk, blk), 0)
    c = lax.broadcasted_iota(jnp.int32, (blk, blk), 1)
    tril = jnp.where(r >= c, 1.0, 0.0).astype(F32)
    carry = carry_sc[...]
    for i in range(h_ref.shape[0] // blk):
        z = y[i * blk:(i + 1) * blk, nk:] + b_ref[...]
        log_f = jnp.minimum(z, 0.0) - jnp.log1p(jnp.exp(-jnp.abs(z)))
        cs = jnp.dot(tril, log_f, precision=lax.Precision.HIGHEST,
                     preferred_element_type=F32) + carry
        c_ref[i * blk:(i + 1) * blk, :] = cs
        carry = cs[blk - 1:blk, :]
    carry_sc[...] = carry


def _proj_kc(h, w, bias, layer, nk, s):
    t, d = h.shape
    n = w.shape[-1]
    bm = min(MM_ROWS, s)
    assert s % bm == 0 and bm % CUMSUM_ROWS == 0
    return pl.pallas_call(
        functools.partial(_proj_kc_kernel, blocks_per_seq=s // bm),
        grid=(t // bm,),
        in_specs=[
            pl.BlockSpec((bm, d), lambda i: (i, 0)),
            pl.BlockSpec((None, d, n), lambda i: (layer, 0, 0)),
            pl.BlockSpec((1, n - nk), lambda i: (0, 0)),
        ],
        out_specs=[
            pl.BlockSpec((bm, nk), lambda i: (i, 0)),
            pl.BlockSpec((bm, n - nk), lambda i: (i, 0)),
        ],
        out_shape=[
            jax.ShapeDtypeStruct((t, nk), BF16),
            jax.ShapeDtypeStruct((t, n - nk), F32),
        ],
        scratch_shapes=[pltpu.VMEM((1, n - nk), F32)],
        compiler_params=_params("arbitrary"),
        name="proj_kc",
    )(h, w, bias)


LOG2E = float(np.log2(np.e))
BIAS_TERMS = 3
AUG_ROWS = BF16_SUBLANES


def _split_bf16(x):
    hi = x.astype(BF16).astype(F32)
    r = x - hi
    mid = r.astype(BF16).astype(F32)
    lo = (r - mid).astype(BF16).astype(F32)
    return hi, mid, lo


def _bias_rows(c, idx, sign):
    hi, mid, lo = _split_bf16(c)
    first = 0 if sign > 0 else BIAS_TERMS
    ones_first = BIAS_TERMS if sign > 0 else 0
    out = jnp.where((idx >= ones_first) & (idx < ones_first + BIAS_TERMS), 1.0, 0.0)
    out = jnp.where(idx == first, sign * hi, out)
    out = jnp.where(idx == first + 1, sign * mid, out)
    out = jnp.where(idx == first + 2, sign * lo, out)
    return out.astype(BF16)


def _fox_kernel(qt_ref, k_ref, vt_ref, c_ref, g_ref, o_ref, kaug_sc, qaug_sc, s_sc, m_sc, l_sc, acc_sc,
                *, bq, bk, hps):
    qi = pl.program_id(2)
    s_len = k_ref.shape[0]
    d = HEAD_DIM

    @pl.when(qi == 0)
    def _():
        lane = lax.broadcasted_iota(jnp.int32, (d, d), 1)
        for hh in range(hps):
            kaug_sc[hh, :, 0:d] = k_ref[:, hh * d:(hh + 1) * d]
            for i in range(s_len // d):
                c_row = c_ref[hh, :, i * d:(i + 1) * d] * LOG2E
                c_col = jnp.transpose(jnp.broadcast_to(c_row, (d, d)))
                kaug_sc[hh, i * d:(i + 1) * d, d:2 * d] = _bias_rows(c_col, lane, -1.0)

    q0 = pl.multiple_of(qi * bq, bq)
    sub = lax.broadcasted_iota(jnp.int32, (AUG_ROWS, bq), 0)
    for hh in range(hps):
        cq = jnp.broadcast_to(c_ref[hh, :, pl.ds(q0, bq)] * LOG2E, (AUG_ROWS, bq))
        qaug_sc[hh, 0:d, :] = qt_ref[hh * d:(hh + 1) * d, :]
        qaug_sc[hh, d:d + AUG_ROWS, :] = _bias_rows(cq, sub, 1.0)
        qaug_sc[hh, d + AUG_ROWS:, :] = jnp.zeros((d - AUG_ROWS, bq), BF16)
    m_sc[...] = jnp.full_like(m_sc, -jnp.inf)
    l_sc[...] = jnp.zeros_like(l_sc)
    acc_sc[...] = jnp.zeros_like(acc_sc)

    def scores(ki, slot, lo, width):
        k0 = pl.multiple_of(ki * bk, bk)
        for hh in range(hps):
            s_sc[slot, hh, :, lo:lo + width] = _dot(kaug_sc[hh, pl.ds(k0, bk), :],
                                                    qaug_sc[hh, :, lo:lo + width])

    def absorb(ki, slot, lo, width, masked):
        cols = slice(lo, lo + width)
        k0 = pl.multiple_of(ki * bk, bk)
        for hh in range(hps):
            s = s_sc[slot, hh, :, cols]
            if masked:
                r = lax.broadcasted_iota(jnp.int32, (bk, width), 0)
                c = lax.broadcasted_iota(jnp.int32, (bk, width), 1)
                s = jnp.where(r - c <= q0 + lo - k0, s, -jnp.inf)
            m_prev = m_sc[hh, :, cols]
            m_new = jnp.maximum(m_prev, jnp.max(s, axis=0, keepdims=True))
            alpha = jnp.exp2(m_prev - m_new)
            p = jnp.exp2(s - m_new)
            l_sc[hh, :, cols] = alpha * l_sc[hh, :, cols] + jnp.sum(p, axis=0, keepdims=True)
            pv = _dot(vt_ref[hh * d:(hh + 1) * d, pl.ds(k0, bk)], p.astype(BF16))
            acc_sc[hh, :, cols] = alpha * acc_sc[hh, :, cols] + pv
            m_sc[hh, :, cols] = m_new

    scores(0, 0, 0, bq)

    def body(pair, carry):
        ki = 2 * pair
        scores(ki + 1, 1, 0, bq)
        absorb(ki, 0, 0, bq, False)
        scores(ki + 2, 0, 0, bq)
        absorb(ki + 1, 1, 0, bq, False)
        return carry

    lax.fori_loop(0, qi, body, 0)
    scores(2 * qi + 1, 1, bk, bk)
    absorb(2 * qi, 0, 0, bq, True)
    absorb(2 * qi + 1, 1, bk, bk, True)
    for hh in range(hps):
        y_t = acc_sc[hh] * (1.0 / l_sc[hh])
        y_t = y_t * lax.rsqrt(jnp.mean(y_t * y_t, axis=0, keepdims=True) + EPS)
        o_ref[:, hh * d:(hh + 1) * d] = (
            jnp.transpose(y_t) * g_ref[:, hh * d:(hh + 1) * d]).astype(o_ref.dtype)


def _fox(qvt, k, c_row, gain, heads):
    b, s, _ = k.shape
    blk = min(FOX_Q_ROWS, s)
    nq = s // blk
    hps = FOX_HEADS_PER_STEP
    w = hps * HEAD_DIM
    assert s % blk == 0 and blk % 2 == 0 and heads % hps == 0
    return pl.pallas_call(
        functools.partial(_fox_kernel, bq=blk, bk=blk // 2, hps=hps),
        grid=(b, heads // hps, nq),
        in_specs=[
            pl.BlockSpec((w, blk), lambda bi, h, qi: (h, bi * nq + qi)),
            pl.BlockSpec((None, s, w), lambda bi, h, qi: (bi, 0, h)),
            pl.BlockSpec((w, s), lambda bi, h, qi: (heads // hps + h, bi)),
            pl.BlockSpec((None, hps, 1, s), lambda bi, h, qi: (bi, h, 0, 0)),
            pl.BlockSpec((1, w), lambda bi, h, qi: (0, h)),
        ],
        out_specs=pl.BlockSpec((None, blk, w), lambda bi, h, qi: (bi, qi, h)),
        out_shape=jax.ShapeDtypeStruct((b, s, heads * HEAD_DIM), BF16),
        scratch_shapes=[
            pltpu.VMEM((hps, s, 2 * HEAD_DIM), BF16),
            pltpu.VMEM((hps, 2 * HEAD_DIM, blk), BF16),
            pltpu.VMEM((2, hps, blk // 2, blk), F32),
            pltpu.VMEM((hps, 1, blk), F32),
            pltpu.VMEM((hps, 1, blk), F32),
            pltpu.VMEM((hps, HEAD_DIM, blk), F32),
        ],
        compiler_params=_params("parallel", "parallel", "arbitrary"),
        name="fox",
    )(qvt, k, qvt, c_row, gain)


def _ret_kernel(h_ref, w_ref, cos_ref, sin_ref, gain_ref, o_ref, p_sc, state_sc,
                *, rows, chunk, heads):
    n = pl.program_id(1)
    d = HEAD_DIM
    w = heads * d

    @pl.when(n == 0)
    def _():
        state_sc[...] = jnp.zeros_like(state_sc)

    p_sc[...] = _dot(h_ref[...], w_ref[...])
    t = lax.broadcasted_iota(jnp.int32, (chunk, 1), 0).astype(F32)
    r = lax.broadcasted_iota(jnp.int32, (chunk, chunk), 0)
    c = lax.broadcasted_iota(jnp.int32, (chunk, chunk), 1)
    diff = (r - c).astype(F32)
    for hh in range(heads):
        lg = float(np.log1p(-np.exp2(-(RET_GAMMA_BASE + hh))))
        lo, hi = hh * d, (hh + 1) * d
        xi = jnp.exp((t + 1.0) * lg)
        zeta = jnp.exp((chunk - 1.0 - t) * lg)
        decay = jnp.where(diff >= 0.0, jnp.exp(jnp.maximum(diff, 0.0) * lg), 0.0)
        for ci in range(rows // chunk):
            rs = slice(ci * chunk, (ci + 1) * chunk)
            cos = cos_ref[rs, :]
            sin = sin_ref[rs, :]

            def rope(x):
                return x * cos + pltpu.roll(x, d // 2, 1) * sin

            q = rope(p_sc[rs, lo:hi])
            k = rope(p_sc[rs, w + lo:w + hi]) * (d ** -0.5)
            v = p_sc[rs, 2 * w + lo:2 * w + hi].astype(BF16)
            gate = p_sc[rs, 3 * w + lo:3 * w + hi]
            qb = q.astype(BF16)
            scores = _dot_nt(qb, k.astype(BF16)) * decay
            state = state_sc[hh]
            o = _dot(scores.astype(BF16), v) + _dot(qb, state.astype(BF16)) * xi
            kz_t = jnp.transpose(k * zeta).astype(BF16)
            state_sc[hh] = state * float(np.exp(chunk * lg)) + _dot(kz_t, v)
            y = _rms(o, gain_ref[:, lo:hi]) * (gate * jax.nn.sigmoid(gate))
            o_ref[rs, lo:hi] = y.astype(o_ref.dtype)


def _retention(h, w_ret, layer, cos2, sin2, gain, heads, b, s):
    t, dm = h.shape
    rows = min(RET_ROWS, s)
    chunk = min(RET_CHUNK, rows)
    nc = s // rows
    w = heads * HEAD_DIM
    assert s % rows == 0 and rows % chunk == 0 and t == b * s
    return pl.pallas_call(
        functools.partial(_ret_kernel, rows=rows, chunk=chunk, heads=heads),
        grid=(b, nc),
        in_specs=[
            pl.BlockSpec((rows, dm), lambda bi, n: (bi * nc + n, 0)),
            pl.BlockSpec((None, dm, 4 * w), lambda bi, n: (layer, 0, 0)),
            pl.BlockSpec((rows, HEAD_DIM), lambda bi, n: (n, 0)),
            pl.BlockSpec((rows, HEAD_DIM), lambda bi, n: (n, 0)),
            pl.BlockSpec((1, w), lambda bi, n: (0, 0)),
        ],
        out_specs=pl.BlockSpec((rows, w), lambda bi, n: (bi * nc + n, 0)),
        out_shape=jax.ShapeDtypeStruct((t, w), BF16),
        scratch_shapes=[
            pltpu.VMEM((rows, 4 * w), F32),
            pltpu.VMEM((heads, HEAD_DIM, HEAD_DIM), F32),
        ],
        compiler_params=_params("parallel", "arbitrary"),
        name="retention",
    )(h, w_ret, cos2, sin2, gain)


def _gelu(x):
    return 0.5 * x * (1.0 + jnp.tanh(np.sqrt(2.0 / np.pi).astype(np.float32) * (x + 0.044715 * (x * x * x))))


def _gmlp_kernel(h_ref, w_ref, lng_ref, lnb_ref, ws_ref, bs_ref, gain_ref, o_ref, uv_ref,
                 *, groups, rows):
    blk = CHUNK_GMLP
    pair = 2 * HEAD_DIM
    for g in range(groups):
        uv_ref[:, g * pair:(g + 1) * pair] = _dot(h_ref[...], w_ref[:, g * pair:(g + 1) * pair])
    r = lax.broadcasted_iota(jnp.int32, (blk, blk), 0)
    c = lax.broadcasted_iota(jnp.int32, (blk, blk), 1)
    for g in range(groups):
        lo, hi = g * HEAD_DIM, (g + 1) * HEAD_DIM
        u_lo, v_lo = g * pair, g * pair + HEAD_DIM
        wm = jnp.where(r >= c, ws_ref[g], 0.0).astype(BF16)
        bias = bs_ref[g]
        v = _gelu(uv_ref[:, v_lo:v_lo + HEAD_DIM])
        mu = jnp.mean(v, axis=-1, keepdims=True)
        var = jnp.mean(jnp.square(v - mu), axis=-1, keepdims=True)
        v = ((v - mu) * lax.rsqrt(var + EPS) * lng_ref[:, lo:hi] + lnb_ref[:, lo:hi]).astype(BF16)
        v_wide = jnp.concatenate([v[ci * blk:(ci + 1) * blk] for ci in range(rows // blk)], axis=1)
        mixed = _dot(wm, v_wide) + bias
        for ci in range(rows // blk):
            rs = slice(ci * blk, (ci + 1) * blk)
            y = _gelu(uv_ref[rs, u_lo:u_lo + HEAD_DIM]) * mixed[:, ci * HEAD_DIM:(ci + 1) * HEAD_DIM]
            o_ref[rs, lo:hi] = _rms(y, gain_ref[:, lo:hi]).astype(o_ref.dtype)


def _gmlp(h, w_gm, layer, ln_g, ln_b, w_s, b_s, gain, groups):
    t, dm = h.shape
    w = groups * HEAD_DIM
    rows = min(GMLP_ROWS, t)
    assert t % rows == 0 and rows % CHUNK_GMLP == 0
    return pl.pallas_call(
        functools.partial(_gmlp_kernel, groups=groups, rows=rows),
        grid=(t // rows,),
        in_specs=[
            pl.BlockSpec((rows, dm), lambda i: (i, 0)),
            pl.BlockSpec((None, dm, 2 * w), lambda i: (layer, 0, 0)),
            pl.BlockSpec((1, w), lambda i: (0, 0)),
            pl.BlockSpec((1, w), lambda i: (0, 0)),
            pl.BlockSpec((groups, CHUNK_GMLP, CHUNK_GMLP), lambda i: (0, 0, 0)),
            pl.BlockSpec((groups, CHUNK_GMLP, 1), lambda i: (0, 0, 0)),
            pl.BlockSpec((1, w), lambda i: (0, 0)),
        ],
        out_specs=pl.BlockSpec((rows, w), lambda i: (i, 0)),
        out_shape=jax.ShapeDtypeStruct((t, w), BF16),
        scratch_shapes=[pltpu.VMEM((rows, 2 * w), F32)],
        compiler_params=_params("parallel"),
        name="gmlp",
    )(h, w_gm, ln_g, ln_b, w_s, b_s, gain)


def _outproj_kernel(x_ref, ya_ref, yb_ref, yc_ref, w_ref, o_ref):
    wa, wb = ya_ref.shape[1], yb_ref.shape[1]
    acc = _dot(ya_ref[...], w_ref[0:wa, :])
    acc += _dot(yb_ref[...], w_ref[wa:wa + wb, :])
    acc += _dot(yc_ref[...], w_ref[wa + wb:, :])
    o_ref[...] = x_ref[...] + acc


def _outproj(x, ya, yb, yc, w, layer):
    t, d = x.shape
    k = w.shape[-2]
    bm, bn = min(OUT_ROWS, t), min(OUT_COLS, d)
    assert t % bm == 0 and d % bn == 0
    return pl.pallas_call(
        _outproj_kernel,
        grid=(t // bm, d // bn),
        in_specs=[
            pl.BlockSpec((bm, bn), lambda i, j: (i, j)),
            pl.BlockSpec((bm, ya.shape[1]), lambda i, j: (i, 0)),
            pl.BlockSpec((bm, yb.shape[1]), lambda i, j: (i, 0)),
            pl.BlockSpec((bm, yc.shape[1]), lambda i, j: (i, 0)),
            pl.BlockSpec((None, k, bn), lambda i, j: (layer, 0, j)),
        ],
        out_specs=pl.BlockSpec((bm, bn), lambda i, j: (i, j)),
        out_shape=jax.ShapeDtypeStruct((t, d), F32),
        compiler_params=_params("parallel", "arbitrary"),
        name="outproj",
    )(x, ya, yb, yc, w)


def kernel(x, ffn1_norm, ffn1_w_gate, ffn1_w_up, ffn1_w_down, mix_norm, w_in, fox_b_f,
           gmlp_ln_g, gmlp_ln_b, gmlp_w_s, gmlp_b_s, out_norm, w_out, ffn2_norm,
           ffn2_w_gate, ffn2_w_up, ffn2_w_down, final_norm):
    b, s, d = x.shape
    depth = w_in.shape[0]
    t = b * s
    n_heads = d // HEAD_DIM
    fox_h, ret_h = n_heads // 2, n_heads // 4
    gm_g = n_heads - fox_h - ret_h
    fox_w, ret_w, gm_w = fox_h * HEAD_DIM, ret_h * HEAD_DIM, gm_g * HEAD_DIM

    o_fz = 3 * fox_w
    o_ret = o_fz + fox_h
    o_gm = o_ret + 4 * ret_w
    w_qv = jnp.concatenate([w_in[:, :, :fox_w], w_in[:, :, 2 * fox_w:o_fz]], axis=2).astype(BF16)
    w_kz = jnp.concatenate(
        [w_in[:, :, fox_w:2 * fox_w],
         jnp.pad(w_in[:, :, o_fz:o_ret], ((0, 0), (0, 0), (0, HEAD_DIM - fox_h)))], axis=2
    ).astype(BF16)
    w_ret = w_in[:, :, o_ret:o_gm].astype(BF16)
    w_gm = jnp.concatenate(
        [w_in[:, :, o_gm + side * gm_w + g * HEAD_DIM:o_gm + side * gm_w + (g + 1) * HEAD_DIM]
         for g in range(gm_g) for side in (0, 1)], axis=2).astype(BF16)
    w_o = w_out.astype(BF16)
    ffn_f32 = (ffn1_w_gate, ffn1_w_up, ffn1_w_down, ffn2_w_gate, ffn2_w_up, ffn2_w_down)
    ffn_w = [w[0].astype(BF16) for w in ffn_f32]
    fz_bias = jnp.pad(fox_b_f, ((0, 0), (0, HEAD_DIM - fox_h)))

    half = HEAD_DIM // 2
    pos = jnp.arange(s, dtype=F32)
    inv_freq = ROPE_BASE ** (-jnp.arange(half, dtype=F32) / half)
    ang = pos[:, None] * inv_freq[None, :]
    cos2 = jnp.concatenate([jnp.cos(ang), jnp.cos(ang)], axis=-1)
    sin2 = jnp.concatenate([-jnp.sin(ang), jnp.sin(ang)], axis=-1)

    xf = x.reshape(t, d)
    for l in range(depth):
        x1, h = _ffn(xf, ffn1_norm[l][None], *ffn_w[:3], mix_norm[l][None], emit_h=True)
        qvt = _proj_t(h, w_qv, l, BF16, scale=HEAD_DIM ** -0.5 * LOG2E, scaled_rows=fox_w)
        kf, c = _proj_kc(h, w_kz, fz_bias[l][None], l, fox_w, s)
        c = jnp.transpose(c.reshape(b, s, HEAD_DIM)[:, :, :fox_h], (0, 2, 1))
        gains = out_norm[l][None]
        ya = _fox(qvt, kf.reshape(b, s, fox_w), c[:, :, None, :], gains[:, :fox_w], fox_h)
        yb = _retention(h, w_ret, l, cos2, sin2, gains[:, fox_w:fox_w + ret_w], ret_h, b, s)
        yc = _gmlp(h, w_gm, l, gmlp_ln_g[l][None], gmlp_ln_b[l][None], gmlp_w_s[l],
                   gmlp_b_s[l][..., None], gains[:, fox_w + ret_w:], gm_g)
        x2 = _outproj(x1, ya.reshape(t, fox_w), yb, yc, w_o, l)

        last = l == depth - 1
        xf, *next_w = _ffn(x2, ffn2_norm[l][None], *ffn_w[3:], final_norm[None], norm_out=last,
                           cast=() if last else ffn_f32, cast_layer=l + 1)
        ffn_w = next_w or ffn_w
    return xf.reshape(b, s, d)
```

```python
import functools

import jax
import jax.numpy as jnp
import numpy as np
from jax import lax
from jax.experimental import pallas as pl
from jax.experimental.pallas import tpu as pltpu

F32 = jnp.float32
BF16 = jnp.bfloat16

HEAD_DIM = 128
CHUNK_GMLP = 128
ROPE_BASE = 10000.0
RET_GAMMA_BASE = 5.0
EPS = 1e-6

V7X_VMEM_BYTES = 64 * 1024 * 1024
VMEM_LIMIT = V7X_VMEM_BYTES - 1 * 1024 * 1024
LANES = 128
BF16_SUBLANES = 16
CUMSUM_ROWS = 128

FFN_ROWS = 1024
FFN_PROLOGUE_ROWS = 256
FFN_EPILOGUE_ROWS = 128
FFN_COLS = 512
MM_ROWS = 2048
MM_COLS = 1024
OUT_ROWS = 2048
OUT_COLS = 512
FOX_Q_ROWS = 1024
FOX_HEADS_PER_STEP = 4
RET_ROWS = 1024
RET_CHUNK = 256
GMLP_ROWS = 1024


def _params(*sem):
    return pltpu.CompilerParams(dimension_semantics=sem, vmem_limit_bytes=VMEM_LIMIT)


def _rms(x, gain):
    return x * lax.rsqrt(jnp.mean(x * x, axis=-1, keepdims=True) + EPS) * gain


def _dot(a, b):
    return jnp.dot(a, b, preferred_element_type=F32)


def _dot_nt(a, b):
    return lax.dot_general(a, b, (((1,), (1,)), ((), ())), preferred_element_type=F32)


def _ffn_kernel(*refs, emit_h, norm_out, n_cast):
    x_ref, g_ref, wg_ref, wu_ref, wd_ref, g2_ref = refs[:6]
    cast_in = refs[6:6 + n_cast]
    outs, xn_sc = refs[6 + n_cast:-1], refs[-1]
    acc_ref = outs[0]
    cast_out = outs[1 + int(emit_h):]
    j = pl.program_id(1)

    def strips(rows):
        return x_ref.shape[0] // rows, lambda r: pl.ds(pl.multiple_of(r * rows, rows), rows)

    @pl.when(j == 0)
    def _():
        n, strip = strips(FFN_PROLOGUE_ROWS)

        def body(r, carry):
            xn_sc[strip(r), :] = _rms(x_ref[strip(r), :], g_ref[...]).astype(BF16)
            acc_ref[strip(r), :] = jnp.zeros((FFN_PROLOGUE_ROWS, acc_ref.shape[1]), F32)
            return carry

        lax.fori_loop(0, n, body, 0)

    xn = xn_sc[...]
    a = _dot(xn, wg_ref[...])
    b = _dot(xn, wu_ref[...])
    hmid = (a * jax.nn.sigmoid(a) * b).astype(BF16)
    acc_ref[...] += _dot(hmid, wd_ref[...])
    for src, dst in zip(cast_in, cast_out):
        dst[...] = src[...].astype(BF16)

    @pl.when(j == pl.num_programs(1) - 1)
    def _():
        n, strip = strips(FFN_EPILOGUE_ROWS)

        def body(r, carry):
            y = x_ref[strip(r), :] + 0.5 * acc_ref[strip(r), :]
            acc_ref[strip(r), :] = _rms(y, g2_ref[...]) if norm_out else y
            if emit_h:
                outs[1][strip(r), :] = _rms(y, g2_ref[...]).astype(BF16)
            return carry

        lax.fori_loop(0, n, body, 0)


def _ffn(x, gain, wg, wu, wd, gain2, *, emit_h=False, norm_out=False, cast=(), cast_layer=0):
    t, d = x.shape
    f = wg.shape[-1]
    bm, tf = min(FFN_ROWS, t), FFN_COLS
    assert t % bm == 0 and f % tf == 0
    ni, nj = t // bm, f // tf
    in_specs = [
        pl.BlockSpec((bm, d), lambda i, j: (i, 0)),
        pl.BlockSpec((1, d), lambda i, j: (0, 0)),
        pl.BlockSpec((d, tf), lambda i, j: (0, j)),
        pl.BlockSpec((d, tf), lambda i, j: (0, j)),
        pl.BlockSpec((tf, d), lambda i, j: (j, 0)),
        pl.BlockSpec((1, d), lambda i, j: (0, 0)),
    ]
    out_shape = [jax.ShapeDtypeStruct((t, d), F32)]
    if emit_h:
        out_shape.append(jax.ShapeDtypeStruct((t, d), BF16))
    out_specs = [pl.BlockSpec((bm, d), lambda i, j: (i, 0)) for _ in out_shape]
    for w in cast:
        _, r, c = w.shape
        if r == d:
            tile, idx = (r // ni, c // nj), lambda i, j: (i, j)
        else:
            tile, idx = (r // nj, c // ni), lambda i, j: (j, i)
        assert tile[0] % BF16_SUBLANES == 0 and tile[1] % LANES == 0
        in_specs.append(
            pl.BlockSpec((None,) + tile, lambda i, j, idx=idx: (cast_layer,) + idx(i, j)))
        out_specs.append(pl.BlockSpec(tile, idx))
        out_shape.append(jax.ShapeDtypeStruct((r, c), BF16))
    return pl.pallas_call(
        functools.partial(_ffn_kernel, emit_h=emit_h, norm_out=norm_out, n_cast=len(cast)),
        grid=(ni, nj),
        in_specs=in_specs,
        out_specs=out_specs,
        out_shape=out_shape,
        scratch_shapes=[pltpu.VMEM((bm, d), BF16)],
        compiler_params=_params("parallel", "arbitrary"),
        name="ffn",
    )(x, gain, wg, wu, wd, gain2, *cast)


def _proj_t_kernel(h_ref, w_ref, o_ref, *, scale, scaled_rows, bn):
    y = _dot(h_ref[...], w_ref[...])
    if scaled_rows:
        j = pl.program_id(1)
        y = y * jnp.where(j * bn < scaled_rows, scale, 1.0).astype(F32)
    o_ref[...] = jnp.transpose(y).astype(o_ref.dtype)


def _proj_t(h, w, layer, out_dtype, *, scale=1.0, scaled_rows=0):
    t, d = h.shape
    n = w.shape[-1]
    bm, bn = min(MM_ROWS, t), min(MM_COLS, n)
    assert t % bm == 0 and n % bn == 0 and scaled_rows % bn == 0
    return pl.pallas_call(
        functools.partial(_proj_t_kernel, scale=scale, scaled_rows=scaled_rows, bn=bn),
        grid=(t // bm, n // bn),
        in_specs=[
            pl.BlockSpec((bm, d), lambda i, j: (i, 0)),
            pl.BlockSpec((None, d, bn), lambda i, j: (layer, 0, j)),
        ],
        out_specs=pl.BlockSpec((bn, bm), lambda i, j: (j, i)),
        out_shape=jax.ShapeDtypeStruct((n, t), out_dtype),
        compiler_params=_params("parallel", "arbitrary"),
        name="proj_t",
    )(h, w)


def _proj_kc_kernel(h_ref, w_ref, b_ref, k_ref, c_ref, carry_sc, *, blocks_per_seq):
    y = _dot(h_ref[...], w_ref[...])
    nk = k_ref.shape[1]
    k_ref[...] = y[:, :nk].astype(k_ref.dtype)

    @pl.when(pl.program_id(0) % blocks_per_seq == 0)
    def _():
        carry_sc[...] = jnp.zeros_like(carry_sc)

    blk = CUMSUM_ROWS
    r = lax.broadcasted_iota(jnp.int32, (blk, blk), 0)
    c = lax.broadcasted_iota(jnp.int32, (blk, blk), 1)
    tril = jnp.where(r >= c, 1.0, 0.0).astype(F32)
    carry = carry_sc[...]
    for i in range(h_ref.shape[0] // blk):
        z = y[i * blk:(i + 1) * blk, nk:] + b_ref[...]
        log_f = jnp.minimum(z, 0.0) - jnp.log1p(jnp.exp(-jnp.abs(z)))
        cs = jnp.dot(tril, log_f, precision=lax.Precision.HIGHEST,
                     preferred_element_type=F32) + carry
        c_ref[i * blk:(i + 1) * blk, :] = cs
        carry = cs[blk - 1:blk, :]
    carry_sc[...] = carry


def _proj_kc(h, w, bias, layer, nk, s):
    t, d = h.shape
    n = w.shape[-1]
    bm = min(MM_ROWS, s)
    assert s % bm == 0 and bm % CUMSUM_ROWS == 0
    return pl.pallas_call(
        functools.partial(_proj_kc_kernel, blocks_per_seq=s // bm),
        grid=(t // bm,),
        in_specs=[
            pl.BlockSpec((bm, d), lambda i: (i, 0)),
            pl.BlockSpec((None, d, n), lambda i: (layer, 0, 0)),
            pl.BlockSpec((1, n - nk), lambda i: (0, 0)),
        ],
        out_specs=[
            pl.BlockSpec((bm, nk), lambda i: (i, 0)),
            pl.BlockSpec((bm, n - nk), lambda i: (i, 0)),
        ],
        out_shape=[
            jax.ShapeDtypeStruct((t, nk), BF16),
            jax.ShapeDtypeStruct((t, n - nk), F32),
        ],
        scratch_shapes=[pltpu.VMEM((1, n - nk), F32)],
        compiler_params=_params("arbitrary"),
        name="proj_kc",
    )(h, w, bias)


LOG2E = float(np.log2(np.e))
BIAS_TERMS = 3
AUG_ROWS = BF16_SUBLANES


def _split_bf16(x):
    hi = x.astype(BF16).astype(F32)
    r = x - hi
    mid = r.astype(BF16).astype(F32)
    lo = (r - mid).astype(BF16).astype(F32)
    return hi, mid, lo


def _bias_rows(c, idx, sign):
    hi, mid, lo = _split_bf16(c)
    first = 0 if sign > 0 else BIAS_TERMS
    ones_first = BIAS_TERMS if sign > 0 else 0
    out = jnp.where((idx >= ones_first) & (idx < ones_first + BIAS_TERMS), 1.0, 0.0)
    out = jnp.where(idx == first, sign * hi, out)
    out = jnp.where(idx == first + 1, sign * mid, out)
    out = jnp.where(idx == first + 2, sign * lo, out)
    return out.astype(BF16)


def _fox_kernel(qt_ref, k_ref, vt_ref, c_ref, g_ref, o_ref, kaug_sc, qaug_sc, s_sc, m_sc, l_sc, acc_sc,
                *, bq, bk, hps):
    qi = pl.program_id(2)
    s_len = k_ref.shape[0]
    d = HEAD_DIM

    @pl.when(qi == 0)
    def _():
        lane = lax.broadcasted_iota(jnp.int32, (d, d), 1)
        for hh in range(hps):
            kaug_sc[hh, :, 0:d] = k_ref[:, hh * d:(hh + 1) * d]
            for i in range(s_len // d):
                c_row = c_ref[hh, :, i * d:(i + 1) * d] * LOG2E
                c_col = jnp.transpose(jnp.broadcast_to(c_row, (d, d)))
                kaug_sc[hh, i * d:(i + 1) * d, d:2 * d] = _bias_rows(c_col, lane, -1.0)

    q0 = pl.multiple_of(qi * bq, bq)
    sub = lax.broadcasted_iota(jnp.int32, (AUG_ROWS, bq), 0)
    for hh in range(hps):
        cq = jnp.broadcast_to(c_ref[hh, :, pl.ds(q0, bq)] * LOG2E, (AUG_ROWS, bq))
        qaug_sc[hh, 0:d, :] = qt_ref[hh * d:(hh + 1) * d, :]
        qaug_sc[hh, d:d + AUG_ROWS, :] = _bias_rows(cq, sub, 1.0)
        qaug_sc[hh, d + AUG_ROWS:, :] = jnp.zeros((d - AUG_ROWS, bq), BF16)
    m_sc[...] = jnp.full_like(m_sc, -jnp.inf)
    l_sc[...] = jnp.zeros_like(l_sc)
    acc_sc[...] = jnp.zeros_like(acc_sc)

    def scores(ki, slot, lo, width):
        k0 = pl.multiple_of(ki * bk, bk)
        for hh in range(hps):
            s_sc[slot, hh, :, lo:lo + width] = _dot(kaug_sc[hh, pl.ds(k0, bk), :],
                                                    qaug_sc[hh, :, lo:lo + width])

    def absorb(ki, slot, lo, width, masked):
        cols = slice(lo, lo + width)
        k0 = pl.multiple_of(ki * bk, bk)
        for hh in range(hps):
            s = s_sc[slot, hh, :, cols]
            if masked:
                r = lax.broadcasted_iota(jnp.int32, (bk, width), 0)
                c = lax.broadcasted_iota(jnp.int32, (bk, width), 1)
                s = jnp.where(r - c <= q0 + lo - k0, s, -jnp.inf)
            m_prev = m_sc[hh, :, cols]
            m_new = jnp.maximum(m_prev, jnp.max(s, axis=0, keepdims=True))
            alpha = jnp.exp2(m_prev - m_new)
            p = jnp.exp2(s - m_new)
            l_sc[hh, :, cols] = alpha * l_sc[hh, :, cols] + jnp.sum(p, axis=0, keepdims=True)
            pv = _dot(vt_ref[hh * d:(hh + 1) * d, pl.ds(k0, bk)], p.astype(BF16))
            acc_sc[hh, :, cols] = alpha * acc_sc[hh, :, cols] + pv
            m_sc[hh, :, cols] = m_new

    scores(0, 0, 0, bq)

    def body(pair, carry):
        ki = 2 * pair
        scores(ki + 1, 1, 0, bq)
        absorb(ki, 0, 0, bq, False)
        scores(ki + 2, 0, 0, bq)
        absorb(ki + 1, 1, 0, bq, False)
        return carry

    lax.fori_loop(0, qi, body, 0)
    scores(2 * qi + 1, 1, bk, bk)
    absorb(2 * qi, 0, 0, bq, True)
    absorb(2 * qi + 1, 1, bk, bk, True)
    for hh in range(hps):
        y_t = acc_sc[hh] * (1.0 / l_sc[hh])
        y_t = y_t * lax.rsqrt(jnp.mean(y_t * y_t, axis=0, keepdims=True) + EPS)
        o_ref[:, hh * d:(hh + 1) * d] = (
            jnp.transpose(y_t) * g_ref[:, hh * d:(hh + 1) * d]).astype(o_ref.dtype)


def _fox(qvt, k, c_row, gain, heads):
    b, s, _ = k.shape
    blk = min(FOX_Q_ROWS, s)
    nq = s // blk
    hps = FOX_HEADS_PER_STEP
    w = hps * HEAD_DIM
    assert s % blk == 0 and blk % 2 == 0 and heads % hps == 0
    return pl.pallas_call(
        functools.partial(_fox_kernel, bq=blk, bk=blk // 2, hps=hps),
        grid=(b, heads // hps, nq),
        in_specs=[
            pl.BlockSpec((w, blk), lambda bi, h, qi: (h, bi * nq + qi)),
            pl.BlockSpec((None, s, w), lambda bi, h, qi: (bi, 0, h)),
            pl.BlockSpec((w, s), lambda bi, h, qi: (heads // hps + h, bi)),
            pl.BlockSpec((None, hps, 1, s), lambda bi, h, qi: (bi, h, 0, 0)),
            pl.BlockSpec((1, w), lambda bi, h, qi: (0, h)),
        ],
        out_specs=pl.BlockSpec((None, blk, w), lambda bi, h, qi: (bi, qi, h)),
        out_shape=jax.ShapeDtypeStruct((b, s, heads * HEAD_DIM), BF16),
        scratch_shapes=[
            pltpu.VMEM((hps, s, 2 * HEAD_DIM), BF16),
            pltpu.VMEM((hps, 2 * HEAD_DIM, blk), BF16),
            pltpu.VMEM((2, hps, blk // 2, blk), F32),
            pltpu.VMEM((hps, 1, blk), F32),
            pltpu.VMEM((hps, 1, blk), F32),
            pltpu.VMEM((hps, HEAD_DIM, blk), F32),
        ],
        compiler_params=_params("parallel", "parallel", "arbitrary"),
        name="fox",
    )(qvt, k, qvt, c_row, gain)


def _ret_kernel(h_ref, w_ref, cos_ref, sin_ref, gain_ref, o_ref, p_sc, state_sc,
                *, rows, chunk, heads):
    n = pl.program_id(1)
    d = HEAD_DIM
    w = heads * d

    @pl.when(n == 0)
    def _():
        state_sc[...] = jnp.zeros_like(state_sc)

    p_sc[...] = _dot(h_ref[...], w_ref[...])
    t = lax.broadcasted_iota(jnp.int32, (chunk, 1), 0).astype(F32)
    r = lax.broadcasted_iota(jnp.int32, (chunk, chunk), 0)
    c = lax.broadcasted_iota(jnp.int32, (chunk, chunk), 1)
    diff = (r - c).astype(F32)
    for hh in range(heads):
        lg = float(np.log1p(-np.exp2(-(RET_GAMMA_BASE + hh))))
        lo, hi = hh * d, (hh + 1) * d
        xi = jnp.exp((t + 1.0) * lg)
        zeta = jnp.exp((chunk - 1.0 - t) * lg)
        decay = jnp.where(diff >= 0.0, jnp.exp(jnp.maximum(diff, 0.0) * lg), 0.0)
        for ci in range(rows // chunk):
            rs = slice(ci * chunk, (ci + 1) * chunk)
            cos = cos_ref[rs, :]
            sin = sin_ref[rs, :]

            def rope(x):
                return x * cos + pltpu.roll(x, d // 2, 1) * sin

            q = rope(p_sc[rs, lo:hi])
            k = rope(p_sc[rs, w + lo:w + hi]) * (d ** -0.5)
            v = p_sc[rs, 2 * w + lo:2 * w + hi].astype(BF16)
            gate = p_sc[rs, 3 * w + lo:3 * w + hi]
            qb = q.astype(BF16)
            scores = _dot_nt(qb, k.astype(BF16)) * decay
            state = state_sc[hh]
            o = _dot(scores.astype(BF16), v) + _dot(qb, state.astype(BF16)) * xi
            kz_t = jnp.transpose(k * zeta).astype(BF16)
            state_sc[hh] = state * float(np.exp(chunk * lg)) + _dot(kz_t, v)
            y = _rms(o, gain_ref[:, lo:hi]) * (gate * jax.nn.sigmoid(gate))
            o_ref[rs, lo:hi] = y.astype(o_ref.dtype)


def _retention(h, w_ret, layer, cos2, sin2, gain, heads, b, s):
    t, dm = h.shape
    rows = min(RET_ROWS, s)
    chunk = min(RET_CHUNK, rows)
    nc = s // rows
    w = heads * HEAD_DIM
    assert s % rows == 0 and rows % chunk == 0 and t == b * s
    return pl.pallas_call(
        functools.partial(_ret_kernel, rows=rows, chunk=chunk, heads=heads),
        grid=(b, nc),
        in_specs=[
            pl.BlockSpec((rows, dm), lambda bi, n: (bi * nc + n, 0)),
            pl.BlockSpec((None, dm, 4 * w), lambda bi, n: (layer, 0, 0)),
            pl.BlockSpec((rows, HEAD_DIM), lambda bi, n: (n, 0)),
            pl.BlockSpec((rows, HEAD_DIM), lambda bi, n: (n, 0)),
            pl.BlockSpec((1, w), lambda bi, n: (0, 0)),
        ],
        out_specs=pl.BlockSpec((rows, w), lambda bi, n: (bi * nc + n, 0)),
        out_shape=jax.ShapeDtypeStruct((t, w), BF16),
        scratch_shapes=[
            pltpu.VMEM((rows, 4 * w), F32),
            pltpu.VMEM((heads, HEAD_DIM, HEAD_DIM), F32),
        ],
        compiler_params=_params("parallel", "arbitrary"),
        name="retention",
    )(h, w_ret, cos2, sin2, gain)


def _gelu(x):
    return 0.5 * x * (1.0 + jnp.tanh(np.sqrt(2.0 / np.pi).astype(np.float32) * (x + 0.044715 * (x * x * x))))


def _gmlp_kernel(h_ref, w_ref, lng_ref, lnb_ref, ws_ref, bs_ref, gain_ref, o_ref, uv_ref,
                 *, groups, rows):
    blk = CHUNK_GMLP
    pair = 2 * HEAD_DIM
    for g in range(groups):
        uv_ref[:, g * pair:(g + 1) * pair] = _dot(h_ref[...], w_ref[:, g * pair:(g + 1) * pair])
    r = lax.broadcasted_iota(jnp.int32, (blk, blk), 0)
    c = lax.broadcasted_iota(jnp.int32, (blk, blk), 1)
    for g in range(groups):
        lo, hi = g * HEAD_DIM, (g + 1) * HEAD_DIM
        u_lo, v_lo = g * pair, g * pair + HEAD_DIM
        wm = jnp.where(r >= c, ws_ref[g], 0.0).astype(BF16)
        bias = bs_ref[g]
        v = _gelu(uv_ref[:, v_lo:v_lo + HEAD_DIM])
        mu = jnp.mean(v, axis=-1, keepdims=True)
        var = jnp.mean(jnp.square(v - mu), axis=-1, keepdims=True)
        v = ((v - mu) * lax.rsqrt(var + EPS) * lng_ref[:, lo:hi] + lnb_ref[:, lo:hi]).astype(BF16)
        v_wide = jnp.concatenate([v[ci * blk:(ci + 1) * blk] for ci in range(rows // blk)], axis=1)
        mixed = _dot(wm, v_wide) + bias
        for ci in range(rows // blk):
            rs = slice(ci * blk, (ci + 1) * blk)
            y = _gelu(uv_ref[rs, u_lo:u_lo + HEAD_DIM]) * mixed[:, ci * HEAD_DIM:(ci + 1) * HEAD_DIM]
            o_ref[rs, lo:hi] = _rms(y, gain_ref[:, lo:hi]).astype(o_ref.dtype)


def _gmlp(h, w_gm, layer, ln_g, ln_b, w_s, b_s, gain, groups):
    t, dm = h.shape
    w = groups * HEAD_DIM
    rows = min(GMLP_ROWS, t)
    assert t % rows == 0 and rows % CHUNK_GMLP == 0
    return pl.pallas_call(
        functools.partial(_gmlp_kernel, groups=groups, rows=rows),
        grid=(t // rows,),
        in_specs=[
            pl.BlockSpec((rows, dm), lambda i: (i, 0)),
            pl.BlockSpec((None, dm, 2 * w), lambda i: (layer, 0, 0)),
            pl.BlockSpec((1, w), lambda i: (0, 0)),
            pl.BlockSpec((1, w), lambda i: (0, 0)),
            pl.BlockSpec((groups, CHUNK_GMLP, CHUNK_GMLP), lambda i: (0, 0, 0)),
            pl.BlockSpec((groups, CHUNK_GMLP, 1), lambda i: (0, 0, 0)),
            pl.BlockSpec((1, w), lambda i: (0, 0)),
        ],
        out_specs=pl.BlockSpec((rows, w), lambda i: (i, 0)),
        out_shape=jax.ShapeDtypeStruct((t, w), BF16),
        scratch_shapes=[pltpu.VMEM((rows, 2 * w), F32)],
        compiler_params=_params("parallel"),
        name="gmlp",
    )(h, w_gm, ln_g, ln_b, w_s, b_s, gain)


def _outproj_kernel(x_ref, ya_ref, yb_ref, yc_ref, w_ref, o_ref):
    wa, wb = ya_ref.shape[1], yb_ref.shape[1]
    acc = _dot(ya_ref[...], w_ref[0:wa, :])
    acc += _dot(yb_ref[...], w_ref[wa:wa + wb, :])
    acc += _dot(yc_ref[...], w_ref[wa + wb:, :])
    o_ref[...] = x_ref[...] + acc


def _outproj(x, ya, yb, yc, w, layer):
    t, d = x.shape
    k = w.shape[-2]
    bm, bn = min(OUT_ROWS, t), min(OUT_COLS, d)
    assert t % bm == 0 and d % bn == 0
    return pl.pallas_call(
        _outproj_kernel,
        grid=(t // bm, d // bn),
        in_specs=[
            pl.BlockSpec((bm, bn), lambda i, j: (i, j)),
            pl.BlockSpec((bm, ya.shape[1]), lambda i, j: (i, 0)),
            pl.BlockSpec((bm, yb.shape[1]), lambda i, j: (i, 0)),
            pl.BlockSpec((bm, yc.shape[1]), lambda i, j: (i, 0)),
            pl.BlockSpec((None, k, bn), lambda i, j: (layer, 0, j)),
        ],
        out_specs=pl.BlockSpec((bm, bn), lambda i, j: (i, j)),
        out_shape=jax.ShapeDtypeStruct((t, d), F32),
        compiler_params=_params("parallel", "arbitrary"),
        name="outproj",
    )(x, ya, yb, yc, w)


def kernel(x, ffn1_norm, ffn1_w_gate, ffn1_w_up, ffn1_w_down, mix_norm, w_in, fox_b_f,
           gmlp_ln_g, gmlp_ln_b, gmlp_w_s, gmlp_b_s, out_norm, w_out, ffn2_norm,
           ffn2_w_gate, ffn2_w_up, ffn2_w_down, final_norm):
    b, s, d = x.shape
    depth = w_in.shape[0]
    t = b * s
    n_heads = d // HEAD_DIM
    fox_h, ret_h = n_heads // 2, n_heads // 4
    gm_g = n_heads - fox_h - ret_h
    fox_w, ret_w, gm_w = fox_h * HEAD_DIM, ret_h * HEAD_DIM, gm_g * HEAD_DIM

    o_fz = 3 * fox_w
    o_ret = o_fz + fox_h
    o_gm = o_ret + 4 * ret_w
    w_qv = jnp.concatenate([w_in[:, :, :fox_w], w_in[:, :, 2 * fox_w:o_fz]], axis=2).astype(BF16)
    w_kz = jnp.concatenate(
        [w_in[:, :, fox_w:2 * fox_w],
         jnp.pad(w_in[:, :, o_fz:o_ret], ((0, 0), (0, 0), (0, HEAD_DIM - fox_h)))], axis=2
    ).astype(BF16)
    w_ret = w_in[:, :, o_ret:o_gm].astype(BF16)
    w_gm = jnp.concatenate(
        [w_in[:, :, o_gm + side * gm_w + g * HEAD_DIM:o_gm + side * gm_w + (g + 1) * HEAD_DIM]
         for g in range(gm_g) for side in (0, 1)], axis=2).astype(BF16)
    w_o = w_out.astype(BF16)
    ffn_f32 = (ffn1_w_gate, ffn1_w_up, ffn1_w_down, ffn2_w_gate, ffn2_w_up, ffn2_w_down)
    ffn_w = [w[0].astype(BF16) for w in ffn_f32]
    fz_bias = jnp.pad(fox_b_f, ((0, 0), (0, HEAD_DIM - fox_h)))

    half = HEAD_DIM // 2
    pos = jnp.arange(s, dtype=F32)
    inv_freq = ROPE_BASE ** (-jnp.arange(half, dtype=F32) / half)
    ang = pos[:, None] * inv_freq[None, :]
    cos2 = jnp.concatenate([jnp.cos(ang), jnp.cos(ang)], axis=-1)
    sin2 = jnp.concatenate([-jnp.sin(ang), jnp.sin(ang)], axis=-1)

    xf = x.reshape(t, d)
    for l in range(depth):
        x1, h = _ffn(xf, ffn1_norm[l][None], *ffn_w[:3], mix_norm[l][None], emit_h=True)
        qvt = _proj_t(h, w_qv, l, BF16, scale=HEAD_DIM ** -0.5 * LOG2E, scaled_rows=fox_w)
        kf, c = _proj_kc(h, w_kz, fz_bias[l][None], l, fox_w, s)
        c = jnp.transpose(c.reshape(b, s, HEAD_DIM)[:, :, :fox_h], (0, 2, 1))
        gains = out_norm[l][None]
        ya = _fox(qvt, kf.reshape(b, s, fox_w), c[:, :, None, :], gains[:, :fox_w], fox_h)
        yb = _retention(h, w_ret, l, cos2, sin2, gains[:, fox_w:fox_w + ret_w], ret_h, b, s)
        yc = _gmlp(h, w_gm, l, gmlp_ln_g[l][None], gmlp_ln_b[l][None], gmlp_w_s[l],
                   gmlp_b_s[l][..., None], gains[:, fox_w + ret_w:], gm_g)
        x2 = _outproj(x1, ya.reshape(t, fox_w), yb, yc, w_o, l)

        last = l == depth - 1
        xf, *next_w = _ffn(x2, ffn2_norm[l][None], *ffn_w[3:], final_norm[None], norm_out=last,
                           cast=() if last else ffn_f32, cast_layer=l + 1)
        ffn_w = next_w or ffn_w
    return xf.reshape(b, s, d)
```

```python
import functools

import jax
import jax.numpy as jnp
import numpy as np
from jax import lax
from jax.experimental import pallas as pl
from jax.experimental.pallas import tpu as pltpu

F32 = jnp.float32
BF16 = jnp.bfloat16

HEAD_DIM = 128
CHUNK_GMLP = 128
ROPE_BASE = 10000.0
RET_GAMMA_BASE = 5.0
EPS = 1e-6

V7X_VMEM_BYTES = 64 * 1024 * 1024
VMEM_LIMIT = V7X_VMEM_BYTES - 1 * 1024 * 1024
LANES = 128
BF16_SUBLANES = 16
CUMSUM_ROWS = 128

FFN_ROWS = 1024
FFN_PROLOGUE_ROWS = 256
FFN_EPILOGUE_ROWS = 128
FFN_COLS = 512
MM_ROWS = 1024
MM_COLS = 1024
OUT_ROWS = 2048
OUT_COLS = 512
FOX_Q_ROWS = 1024
FOX_HEADS_PER_STEP = 4
RET_ROWS = 1024
RET_CHUNK = 256
GMLP_ROWS = 1024


def _params(*sem):
    return pltpu.CompilerParams(dimension_semantics=sem, vmem_limit_bytes=VMEM_LIMIT)


def _rms(x, gain):
    return x * lax.rsqrt(jnp.mean(x * x, axis=-1, keepdims=True) + EPS) * gain


def _dot(a, b):
    return jnp.dot(a, b, preferred_element_type=F32)


def _dot_nt(a, b):
    return lax.dot_general(a, b, (((1,), (1,)), ((), ())), preferred_element_type=F32)


def _ffn_kernel(*refs, emit_h, norm_out, n_cast):
    x_ref, g_ref, wg_ref, wu_ref, wd_ref, g2_ref = refs[:6]
    cast_in = refs[6:6 + n_cast]
    outs, xn_sc = refs[6 + n_cast:-1], refs[-1]
    acc_ref = outs[0]
    cast_out = outs[1 + int(emit_h):]
    j = pl.program_id(1)

    def strips(rows):
        return x_ref.shape[0] // rows, lambda r: pl.ds(pl.multiple_of(r * rows, rows), rows)

    @pl.when(j == 0)
    def _():
        n, strip = strips(FFN_PROLOGUE_ROWS)

        def body(r, carry):
            xn_sc[strip(r), :] = _rms(x_ref[strip(r), :], g_ref[...]).astype(BF16)
            acc_ref[strip(r), :] = jnp.zeros((FFN_PROLOGUE_ROWS, acc_ref.shape[1]), F32)
            return carry

        lax.fori_loop(0, n, body, 0)

    xn = xn_sc[...]
    a = _dot(xn, wg_ref[...])
    b = _dot(xn, wu_ref[...])
    hmid = (a * jax.nn.sigmoid(a) * b).astype(BF16)
    acc_ref[...] += _dot(hmid, wd_ref[...])
    for src, dst in zip(cast_in, cast_out):
        dst[...] = src[...].astype(BF16)

    @pl.when(j == pl.num_programs(1) - 1)
    def _():
        n, strip = strips(FFN_EPILOGUE_ROWS)

        def body(r, carry):
            y = x_ref[strip(r), :] + 0.5 * acc_ref[strip(r), :]
            acc_ref[strip(r), :] = _rms(y, g2_ref[...]) if norm_out else y
            if emit_h:
                outs[1][strip(r), :] = _rms(y, g2_ref[...]).astype(BF16)
            return carry

        lax.fori_loop(0, n, body, 0)


def _ffn(x, gain, wg, wu, wd, gain2, *, emit_h=False, norm_out=False, cast=(), cast_layer=0):
    t, d = x.shape
    f = wg.shape[-1]
    bm, tf = min(FFN_ROWS, t), FFN_COLS
    assert t % bm == 0 and f % tf == 0
    ni, nj = t // bm, f // tf
    in_specs = [
        pl.BlockSpec((bm, d), lambda i, j: (i, 0)),
        pl.BlockSpec((1, d), lambda i, j: (0, 0)),
        pl.BlockSpec((d, tf), lambda i, j: (0, j)),
        pl.BlockSpec((d, tf), lambda i, j: (0, j)),
        pl.BlockSpec((tf, d), lambda i, j: (j, 0)),
        pl.BlockSpec((1, d), lambda i, j: (0, 0)),
    ]
    out_shape = [jax.ShapeDtypeStruct((t, d), F32)]
    if emit_h:
        out_shape.append(jax.ShapeDtypeStruct((t, d), BF16))
    out_specs = [pl.BlockSpec((bm, d), lambda i, j: (i, 0)) for _ in out_shape]
    for w in cast:
        _, r, c = w.shape
        if r == d:
            tile, idx = (r // ni, c // nj), lambda i, j: (i, j)
        else:
            tile, idx = (r // nj, c // ni), lambda i, j: (j, i)
        assert tile[0] % BF16_SUBLANES == 0 and tile[1] % LANES == 0
        in_specs.append(
            pl.BlockSpec((None,) + tile, lambda i, j, idx=idx: (cast_layer,) + idx(i, j)))
        out_specs.append(pl.BlockSpec(tile, idx))
        out_shape.append(jax.ShapeDtypeStruct((r, c), BF16))
    return pl.pallas_call(
        functools.partial(_ffn_kernel, emit_h=emit_h, norm_out=norm_out, n_cast=len(cast)),
        grid=(ni, nj),
        in_specs=in_specs,
        out_specs=out_specs,
        out_shape=out_shape,
        scratch_shapes=[pltpu.VMEM((bm, d), BF16)],
        compiler_params=_params("parallel", "arbitrary"),
        name="ffn",
    )(x, gain, wg, wu, wd, gain2, *cast)


def _proj_t_kernel(h_ref, w_ref, o_ref, *, scale, scaled_rows, bn):
    y = _dot(h_ref[...], w_ref[...])
    if scaled_rows:
        j = pl.program_id(1)
        y = y * jnp.where(j * bn < scaled_rows, scale, 1.0).astype(F32)
    o_ref[...] = jnp.transpose(y).astype(o_ref.dtype)


def _proj_t(h, w, layer, out_dtype, *, scale=1.0, scaled_rows=0):
    t, d = h.shape
    n = w.shape[-1]
    bm, bn = min(MM_ROWS, t), min(MM_COLS, n)
    assert t % bm == 0 and n % bn == 0 and scaled_rows % bn == 0
    return pl.pallas_call(
        functools.partial(_proj_t_kernel, scale=scale, scaled_rows=scaled_rows, bn=bn),
        grid=(t // bm, n // bn),
        in_specs=[
            pl.BlockSpec((bm, d), lambda i, j: (i, 0)),
            pl.BlockSpec((None, d, bn), lambda i, j: (layer, 0, j)),
        ],
        out_specs=pl.BlockSpec((bn, bm), lambda i, j: (j, i)),
        out_shape=jax.ShapeDtypeStruct((n, t), out_dtype),
        compiler_params=_params("parallel", "arbitrary"),
        name="proj_t",
    )(h, w)


def _proj_kc_kernel(h_ref, w_ref, b_ref, k_ref, c_ref, carry_sc, *, blocks_per_seq):
    y = _dot(h_ref[...], w_ref[...])
    nk = k_ref.shape[1]
    k_ref[...] = y[:, :nk].astype(k_ref.dtype)

    @pl.when(pl.program_id(0) % blocks_per_seq == 0)
    def _():
        carry_sc[...] = jnp.zeros_like(carry_sc)

    blk = CUMSUM_ROWS
    r = lax.broadcasted_iota(jnp.int32, (blk, blk), 0)
    c = lax.broadcasted_iota(jnp.int32, (blk, blk), 1)
    tril = jnp.where(r >= c, 1.0, 0.0).astype(F32)
    carry = carry_sc[...]
    for i in range(h_ref.shape[0] // blk):
        z = y[i * blk:(i + 1) * blk, nk:] + b_ref[...]
        log_f = jnp.minimum(z, 0.0) - jnp.log1p(jnp.exp(-jnp.abs(z)))
        cs = jnp.dot(tril, log_f, precision=lax.Precision.HIGHEST,
                     preferred_element_type=F32) + carry
        c_ref[i * blk:(i + 1) * blk, :] = cs
        carry = cs[blk - 1:blk, :]
    carry_sc[...] = carry


def _proj_kc(h, w, bias, layer, nk, s):
    t, d = h.shape
    n = w.shape[-1]
    bm = min(MM_ROWS, s)
    assert s % bm == 0 and bm % CUMSUM_ROWS == 0
    return pl.pallas_call(
        functools.partial(_proj_kc_kernel, blocks_per_seq=s // bm),
        grid=(t // bm,),
        in_specs=[
            pl.BlockSpec((bm, d), lambda i: (i, 0)),
            pl.BlockSpec((None, d, n), lambda i: (layer, 0, 0)),
            pl.BlockSpec((1, n - nk), lambda i: (0, 0)),
        ],
        out_specs=[
            pl.BlockSpec((bm, nk), lambda i: (i, 0)),
            pl.BlockSpec((bm, n - nk), lambda i: (i, 0)),
        ],
        out_shape=[
            jax.ShapeDtypeStruct((t, nk), BF16),
            jax.ShapeDtypeStruct((t, n - nk), F32),
        ],
        scratch_shapes=[pltpu.VMEM((1, n - nk), F32)],
        compiler_params=_params("arbitrary"),
        name="proj_kc",
    )(h, w, bias)


LOG2E = float(np.log2(np.e))
BIAS_TERMS = 3
AUG_ROWS = BF16_SUBLANES


def _split_bf16(x):
    hi = x.astype(BF16).astype(F32)
    r = x - hi
    mid = r.astype(BF16).astype(F32)
    lo = (r - mid).astype(BF16).astype(F32)
    return hi, mid, lo


def _bias_rows(c, idx, sign):
    hi, mid, lo = _split_bf16(c)
    first = 0 if sign > 0 else BIAS_TERMS
    ones_first = BIAS_TERMS if sign > 0 else 0
    out = jnp.where((idx >= ones_first) & (idx < ones_first + BIAS_TERMS), 1.0, 0.0)
    out = jnp.where(idx == first, sign * hi, out)
    out = jnp.where(idx == first + 1, sign * mid, out)
    out = jnp.where(idx == first + 2, sign * lo, out)
    return out.astype(BF16)


def _fox_kernel(qt_ref, k_ref, vt_ref, c_ref, g_ref, o_ref, kaug_sc, vaug_sc, qaug_sc, s_sc, m_sc, acc_sc,
                *, bq, bk, hps):
    qi = pl.program_id(2)
    s_len = k_ref.shape[0]
    d = HEAD_DIM

    @pl.when(qi == 0)
    def _():
        lane = lax.broadcasted_iota(jnp.int32, (d, d), 1)
        for hh in range(hps):
            kaug_sc[hh, :, 0:d] = k_ref[:, hh * d:(hh + 1) * d]
            for i in range(s_len // d):
                c_row = c_ref[hh, :, i * d:(i + 1) * d] * LOG2E
                c_col = jnp.transpose(jnp.broadcast_to(c_row, (d, d)))
                kaug_sc[hh, i * d:(i + 1) * d, d:2 * d] = _bias_rows(c_col, lane, -1.0)
            vaug_sc[hh, 0:d, :] = vt_ref[hh * d:(hh + 1) * d, :]
            ones_row = lax.broadcasted_iota(jnp.int32, (AUG_ROWS, s_len), 0) == 0
            vaug_sc[hh, d:, :] = jnp.where(ones_row, 1.0, 0.0).astype(BF16)

    q0 = pl.multiple_of(qi * bq, bq)
    sub = lax.broadcasted_iota(jnp.int32, (AUG_ROWS, bq), 0)
    for hh in range(hps):
        cq = jnp.broadcast_to(c_ref[hh, :, pl.ds(q0, bq)] * LOG2E, (AUG_ROWS, bq))
        qaug_sc[hh, 0:d, :] = qt_ref[hh * d:(hh + 1) * d, :]
        qaug_sc[hh, d:d + AUG_ROWS, :] = _bias_rows(cq, sub, 1.0)
        qaug_sc[hh, d + AUG_ROWS:, :] = jnp.zeros((d - AUG_ROWS, bq), BF16)
    m_sc[...] = jnp.full_like(m_sc, -jnp.inf)
    acc_sc[...] = jnp.zeros_like(acc_sc)

    def scores(ki, slot, lo, width):
        k0 = pl.multiple_of(ki * bk, bk)
        for hh in range(hps):
            s_sc[slot, hh, :, lo:lo + width] = _dot(kaug_sc[hh, pl.ds(k0, bk), :],
                                                    qaug_sc[hh, :, lo:lo + width])

    def absorb(ki, slot, lo, width, masked):
        cols = slice(lo, lo + width)
        k0 = pl.multiple_of(ki * bk, bk)
        for hh in range(hps):
            s = s_sc[slot, hh, :, cols]
            if masked:
                r = lax.broadcasted_iota(jnp.int32, (bk, width), 0)
                c = lax.broadcasted_iota(jnp.int32, (bk, width), 1)
                s = jnp.where(r - c <= q0 + lo - k0, s, -jnp.inf)
            m_prev = m_sc[hh, :, cols]
            m_new = jnp.maximum(m_prev, jnp.max(s, axis=0, keepdims=True))
            alpha = jnp.exp2(m_prev - m_new)
            p = jnp.exp2(s - m_new)
            pv = _dot(vaug_sc[hh, :, pl.ds(k0, bk)], p.astype(BF16))
            acc_sc[hh, :, cols] = alpha * acc_sc[hh, :, cols] + pv
            m_sc[hh, :, cols] = m_new

    scores(0, 0, 0, bq)

    def body(pair, carry):
        ki = 2 * pair
        scores(ki + 1, 1, 0, bq)
        absorb(ki, 0, 0, bq, False)
        scores(ki + 2, 0, 0, bq)
        absorb(ki + 1, 1, 0, bq, False)
        return carry

    lax.fori_loop(0, qi, body, 0)
    scores(2 * qi + 1, 1, bk, bk)
    absorb(2 * qi, 0, 0, bq, True)
    absorb(2 * qi + 1, 1, bk, bk, True)
    for hh in range(hps):
        y_t = acc_sc[hh, 0:d, :] * (1.0 / acc_sc[hh, d:d + 1, :])
        y_t = y_t * lax.rsqrt(jnp.mean(y_t * y_t, axis=0, keepdims=True) + EPS)
        o_ref[:, hh * d:(hh + 1) * d] = (
            jnp.transpose(y_t) * g_ref[:, hh * d:(hh + 1) * d]).astype(o_ref.dtype)


def _fox(qvt, k, c_row, gain, heads):
    b, s, _ = k.shape
    blk = min(FOX_Q_ROWS, s)
    nq = s // blk
    hps = FOX_HEADS_PER_STEP
    w = hps * HEAD_DIM
    assert s % blk == 0 and blk % 2 == 0 and heads % hps == 0
    return pl.pallas_call(
        functools.partial(_fox_kernel, bq=blk, bk=blk // 2, hps=hps),
        grid=(b, heads // hps, nq),
        in_specs=[
            pl.BlockSpec((w, blk), lambda bi, h, qi: (h, bi * nq + qi)),
            pl.BlockSpec((None, s, w), lambda bi, h, qi: (bi, 0, h)),
            pl.BlockSpec((w, s), lambda bi, h, qi: (heads // hps + h, bi)),
            pl.BlockSpec((None, hps, 1, s), lambda bi, h, qi: (bi, h, 0, 0)),
            pl.BlockSpec((1, w), lambda bi, h, qi: (0, h)),
        ],
        out_specs=pl.BlockSpec((None, blk, w), lambda bi, h, qi: (bi, qi, h)),
        out_shape=jax.ShapeDtypeStruct((b, s, heads * HEAD_DIM), BF16),
        scratch_shapes=[
            pltpu.VMEM((hps, s, 2 * HEAD_DIM), BF16),
            pltpu.VMEM((hps, HEAD_DIM + AUG_ROWS, s), BF16),
            pltpu.VMEM((hps, 2 * HEAD_DIM, blk), BF16),
            pltpu.VMEM((2, hps, blk // 2, blk), F32),
            pltpu.VMEM((hps, 1, blk), F32),
            pltpu.VMEM((hps, HEAD_DIM + AUG_ROWS, blk), F32),
        ],
        compiler_params=_params("parallel", "parallel", "arbitrary"),
        name="fox",
    )(qvt, k, qvt, c_row, gain)


def _ret_kernel(h_ref, w_ref, cos_ref, sin_ref, gain_ref, o_ref, p_sc, state_sc,
                *, rows, chunk, heads):
    n = pl.program_id(1)
    d = HEAD_DIM
    w = heads * d

    @pl.when(n == 0)
    def _():
        state_sc[...] = jnp.zeros_like(state_sc)

    p_sc[...] = _dot(h_ref[...], w_ref[...])
    t = lax.broadcasted_iota(jnp.int32, (chunk, 1), 0).astype(F32)
    r = lax.broadcasted_iota(jnp.int32, (chunk, chunk), 0)
    c = lax.broadcasted_iota(jnp.int32, (chunk, chunk), 1)
    diff = (r - c).astype(F32)
    for hh in range(heads):
        lg = float(np.log1p(-np.exp2(-(RET_GAMMA_BASE + hh))))
        lo, hi = hh * d, (hh + 1) * d
        xi = jnp.exp((t + 1.0) * lg)
        zeta = jnp.exp((chunk - 1.0 - t) * lg)
        decay = jnp.where(diff >= 0.0, jnp.exp(jnp.maximum(diff, 0.0) * lg), 0.0)
        for ci in range(rows // chunk):
            rs = slice(ci * chunk, (ci + 1) * chunk)
            cos = cos_ref[rs, :]
            sin = sin_ref[rs, :]

            def rope(x):
                return x * cos + pltpu.roll(x, d // 2, 1) * sin

            q = rope(p_sc[rs, lo:hi])
            k = rope(p_sc[rs, w + lo:w + hi]) * (d ** -0.5)
            v = p_sc[rs, 2 * w + lo:2 * w + hi].astype(BF16)
            gate = p_sc[rs, 3 * w + lo:3 * w + hi]
            qb = q.astype(BF16)
            scores = _dot_nt(qb, k.astype(BF16)) * decay
            state = state_sc[hh]
            o = _dot(scores.astype(BF16), v) + _dot(qb, state.astype(BF16)) * xi
            kz_t = jnp.transpose(k * zeta).astype(BF16)
            state_sc[hh] = state * float(np.exp(chunk * lg)) + _dot(kz_t, v)
            y = _rms(o, gain_ref[:, lo:hi]) * (gate * jax.nn.sigmoid(gate))
            o_ref[rs, lo:hi] = y.astype(o_ref.dtype)


def _retention(h, w_ret, layer, cos2, sin2, gain, heads, b, s):
    t, dm = h.shape
    rows = min(RET_ROWS, s)
    chunk = min(RET_CHUNK, rows)
    nc = s // rows
    w = heads * HEAD_DIM
    assert s % rows == 0 and rows % chunk == 0 and t == b * s
    return pl.pallas_call(
        functools.partial(_ret_kernel, rows=rows, chunk=chunk, heads=heads),
        grid=(b, nc),
        in_specs=[
            pl.BlockSpec((rows, dm), lambda bi, n: (bi * nc + n, 0)),
            pl.BlockSpec((None, dm, 4 * w), lambda bi, n: (layer, 0, 0)),
            pl.BlockSpec((rows, HEAD_DIM), lambda bi, n: (n, 0)),
            pl.BlockSpec((rows, HEAD_DIM), lambda bi, n: (n, 0)),
            pl.BlockSpec((1, w), lambda bi, n: (0, 0)),
        ],
        out_specs=pl.BlockSpec((rows, w), lambda bi, n: (bi * nc + n, 0)),
        out_shape=jax.ShapeDtypeStruct((t, w), BF16),
        scratch_shapes=[
            pltpu.VMEM((rows, 4 * w), F32),
            pltpu.VMEM((heads, HEAD_DIM, HEAD_DIM), F32),
        ],
        compiler_params=_params("parallel", "arbitrary"),
        name="retention",
    )(h, w_ret, cos2, sin2, gain)


def _gelu(x):
    return 0.5 * x * (1.0 + jnp.tanh(np.sqrt(2.0 / np.pi).astype(np.float32) * (x + 0.044715 * (x * x * x))))


def _gmlp_kernel(h_ref, w_ref, lng_ref, lnb_ref, ws_ref, bs_ref, gain_ref, o_ref, uv_ref,
                 *, groups, rows):
    blk = CHUNK_GMLP
    pair = 2 * HEAD_DIM
    for g in range(groups):
        uv_ref[:, g * pair:(g + 1) * pair] = _dot(h_ref[...], w_ref[:, g * pair:(g + 1) * pair])
    r = lax.broadcasted_iota(jnp.int32, (blk, blk), 0)
    c = lax.broadcasted_iota(jnp.int32, (blk, blk), 1)
    for g in range(groups):
        lo, hi = g * HEAD_DIM, (g + 1) * HEAD_DIM
        u_lo, v_lo = g * pair, g * pair + HEAD_DIM
        wm = jnp.where(r >= c, ws_ref[g], 0.0).astype(BF16)
        bias = bs_ref[g]
        v = _gelu(uv_ref[:, v_lo:v_lo + HEAD_DIM])
        mu = jnp.mean(v, axis=-1, keepdims=True)
        var = jnp.mean(jnp.square(v - mu), axis=-1, keepdims=True)
        v = ((v - mu) * lax.rsqrt(var + EPS) * lng_ref[:, lo:hi] + lnb_ref[:, lo:hi]).astype(BF16)
        v_wide = jnp.concatenate([v[ci * blk:(ci + 1) * blk] for ci in range(rows // blk)], axis=1)
        mixed = _dot(wm, v_wide) + bias
        for ci in range(rows // blk):
            rs = slice(ci * blk, (ci + 1) * blk)
            y = _gelu(uv_ref[rs, u_lo:u_lo + HEAD_DIM]) * mixed[:, ci * HEAD_DIM:(ci + 1) * HEAD_DIM]
            o_ref[rs, lo:hi] = _rms(y, gain_ref[:, lo:hi]).astype(o_ref.dtype)


def _gmlp(h, w_gm, layer, ln_g, ln_b, w_s, b_s, gain, groups):
    t, dm = h.shape
    w = groups * HEAD_DIM
    rows = min(GMLP_ROWS, t)
    assert t % rows == 0 and rows % CHUNK_GMLP == 0
    return pl.pallas_call(
        functools.partial(_gmlp_kernel, groups=groups, rows=rows),
        grid=(t // rows,),
        in_specs=[
            pl.BlockSpec((rows, dm), lambda i: (i, 0)),
            pl.BlockSpec((None, dm, 2 * w), lambda i: (layer, 0, 0)),
            pl.BlockSpec((1, w), lambda i: (0, 0)),
            pl.BlockSpec((1, w), lambda i: (0, 0)),
            pl.BlockSpec((groups, CHUNK_GMLP, CHUNK_GMLP), lambda i: (0, 0, 0)),
            pl.BlockSpec((groups, CHUNK_GMLP, 1), lambda i: (0, 0, 0)),
            pl.BlockSpec((1, w), lambda i: (0, 0)),
        ],
        out_specs=pl.BlockSpec((rows, w), lambda i: (i, 0)),
        out_shape=jax.ShapeDtypeStruct((t, w), BF16),
        scratch_shapes=[pltpu.VMEM((rows, 2 * w), F32)],
        compiler_params=_params("parallel"),
        name="gmlp",
    )(h, w_gm, ln_g, ln_b, w_s, b_s, gain)


def _outproj_kernel(x_ref, ya_ref, yb_ref, yc_ref, w_ref, o_ref):
    wa, wb = ya_ref.shape[1], yb_ref.shape[1]
    acc = _dot(ya_ref[...], w_ref[0:wa, :])
    acc += _dot(yb_ref[...], w_ref[wa:wa + wb, :])
    acc += _dot(yc_ref[...], w_ref[wa + wb:, :])
    o_ref[...] = x_ref[...] + acc


def _outproj(x, ya, yb, yc, w, layer):
    t, d = x.shape
    k = w.shape[-2]
    bm, bn = min(OUT_ROWS, t), min(OUT_COLS, d)
    assert t % bm == 0 and d % bn == 0
    return pl.pallas_call(
        _outproj_kernel,
        grid=(t // bm, d // bn),
        in_specs=[
            pl.BlockSpec((bm, bn), lambda i, j: (i, j)),
            pl.BlockSpec((bm, ya.shape[1]), lambda i, j: (i, 0)),
            pl.BlockSpec((bm, yb.shape[1]), lambda i, j: (i, 0)),
            pl.BlockSpec((bm, yc.shape[1]), lambda i, j: (i, 0)),
            pl.BlockSpec((None, k, bn), lambda i, j: (layer, 0, j)),
        ],
        out_specs=pl.BlockSpec((bm, bn), lambda i, j: (i, j)),
        out_shape=jax.ShapeDtypeStruct((t, d), F32),
        compiler_params=_params("parallel", "arbitrary"),
        name="outproj",
    )(x, ya, yb, yc, w)


def kernel(x, ffn1_norm, ffn1_w_gate, ffn1_w_up, ffn1_w_down, mix_norm, w_in, fox_b_f,
           gmlp_ln_g, gmlp_ln_b, gmlp_w_s, gmlp_b_s, out_norm, w_out, ffn2_norm,
           ffn2_w_gate, ffn2_w_up, ffn2_w_down, final_norm):
    b, s, d = x.shape
    depth = w_in.shape[0]
    t = b * s
    n_heads = d // HEAD_DIM
    fox_h, ret_h = n_heads // 2, n_heads // 4
    gm_g = n_heads - fox_h - ret_h
    fox_w, ret_w, gm_w = fox_h * HEAD_DIM, ret_h * HEAD_DIM, gm_g * HEAD_DIM

    o_fz = 3 * fox_w
    o_ret = o_fz + fox_h
    o_gm = o_ret + 4 * ret_w
    w_qv = jnp.concatenate([w_in[:, :, :fox_w], w_in[:, :, 2 * fox_w:o_fz]], axis=2).astype(BF16)
    w_kz = jnp.concatenate(
        [w_in[:, :, fox_w:2 * fox_w],
         jnp.pad(w_in[:, :, o_fz:o_ret], ((0, 0), (0, 0), (0, HEAD_DIM - fox_h)))], axis=2
    ).astype(BF16)
    w_ret = w_in[:, :, o_ret:o_gm].astype(BF16)
    w_gm = jnp.concatenate(
        [w_in[:, :, o_gm + side * gm_w + g * HEAD_DIM:o_gm + side * gm_w + (g + 1) * HEAD_DIM]
         for g in range(gm_g) for side in (0, 1)], axis=2).astype(BF16)
    w_o = w_out.astype(BF16)
    ffn_f32 = (ffn1_w_gate, ffn1_w_up, ffn1_w_down, ffn2_w_gate, ffn2_w_up, ffn2_w_down)
    ffn_w = [w[0].astype(BF16) for w in ffn_f32]
    fz_bias = jnp.pad(fox_b_f, ((0, 0), (0, HEAD_DIM - fox_h)))

    half = HEAD_DIM // 2
    pos = jnp.arange(s, dtype=F32)
    inv_freq = ROPE_BASE ** (-jnp.arange(half, dtype=F32) / half)
    ang = pos[:, None] * inv_freq[None, :]
    cos2 = jnp.concatenate([jnp.cos(ang), jnp.cos(ang)], axis=-1)
    sin2 = jnp.concatenate([-jnp.sin(ang), jnp.sin(ang)], axis=-1)

    xf = x.reshape(t, d)
    for l in range(depth):
        x1, h = _ffn(xf, ffn1_norm[l][None], *ffn_w[:3], mix_norm[l][None], emit_h=True)
        qvt = _proj_t(h, w_qv, l, BF16, scale=HEAD_DIM ** -0.5 * LOG2E, scaled_rows=fox_w)
        kf, c = _proj_kc(h, w_kz, fz_bias[l][None], l, fox_w, s)
        c = jnp.transpose(c.reshape(b, s, HEAD_DIM)[:, :, :fox_h], (0, 2, 1))
        gains = out_norm[l][None]
        ya = _fox(qvt, kf.reshape(b, s, fox_w), c[:, :, None, :], gains[:, :fox_w], fox_h)
        yb = _retention(h, w_ret, l, cos2, sin2, gains[:, fox_w:fox_w + ret_w], ret_h, b, s)
        yc = _gmlp(h, w_gm, l, gmlp_ln_g[l][None], gmlp_ln_b[l][None], gmlp_w_s[l],
                   gmlp_b_s[l][..., None], gains[:, fox_w + ret_w:], gm_g)
        x2 = _outproj(x1, ya.reshape(t, fox_w), yb, yc, w_o, l)

        last = l == depth - 1
        xf, *next_w = _ffn(x2, ffn2_norm[l][None], *ffn_w[3:], final_norm[None], norm_out=last,
                           cast=() if last else ffn_f32, cast_layer=l + 1)
        ffn_w = next_w or ffn_w
    return xf.reshape(b, s, d)
```

```python
import functools

import jax
import jax.numpy as jnp
import numpy as np
from jax import lax
from jax.experimental import pallas as pl
from jax.experimental.pallas import tpu as pltpu

F32 = jnp.float32
BF16 = jnp.bfloat16

HEAD_DIM = 128
CHUNK_GMLP = 128
ROPE_BASE = 10000.0
RET_GAMMA_BASE = 5.0
EPS = 1e-6

V7X_VMEM_BYTES = 64 * 1024 * 1024
VMEM_LIMIT = V7X_VMEM_BYTES - 1 * 1024 * 1024
LANES = 128
BF16_SUBLANES = 16
CUMSUM_ROWS = 128

FFN_ROWS = 1024
FFN_PROLOGUE_ROWS = 256
FFN_EPILOGUE_ROWS = 128
FFN_COLS = 512
MM_ROWS = 1024
MM_COLS = 1024
OUT_ROWS = 2048
OUT_COLS = 512
FOX_Q_ROWS = 1024
FOX_HEADS_PER_STEP = 4
RET_ROWS = 1024
RET_CHUNK = 256
GMLP_ROWS = 1024


def _params(*sem):
    return pltpu.CompilerParams(dimension_semantics=sem, vmem_limit_bytes=VMEM_LIMIT)


def _rms(x, gain):
    return x * lax.rsqrt(jnp.mean(x * x, axis=-1, keepdims=True) + EPS) * gain


def _dot(a, b):
    return jnp.dot(a, b, preferred_element_type=F32)


def _dot_nt(a, b):
    return lax.dot_general(a, b, (((1,), (1,)), ((), ())), preferred_element_type=F32)


def _ffn_kernel(*refs, emit_h, norm_out, n_cast):
    x_ref, g_ref, wg_ref, wu_ref, wd_ref, g2_ref = refs[:6]
    cast_in = refs[6:6 + n_cast]
    outs, xn_sc = refs[6 + n_cast:-1], refs[-1]
    acc_ref = outs[0]
    cast_out = outs[1 + int(emit_h):]
    j = pl.program_id(1)

    def strips(rows):
        return x_ref.shape[0] // rows, lambda r: pl.ds(pl.multiple_of(r * rows, rows), rows)

    @pl.when(j == 0)
    def _():
        n, strip = strips(FFN_PROLOGUE_ROWS)

        def body(r, carry):
            xn_sc[strip(r), :] = _rms(x_ref[strip(r), :], g_ref[...]).astype(BF16)
            acc_ref[strip(r), :] = jnp.zeros((FFN_PROLOGUE_ROWS, acc_ref.shape[1]), F32)
            return carry

        lax.fori_loop(0, n, body, 0)

    xn = xn_sc[...]
    a = _dot(xn, wg_ref[...])
    b = _dot(xn, wu_ref[...])
    hmid = (a * jax.nn.sigmoid(a) * b).astype(BF16)
    acc_ref[...] += _dot(hmid, wd_ref[...])
    for src, dst in zip(cast_in, cast_out):
        dst[...] = src[...].astype(BF16)

    @pl.when(j == pl.num_programs(1) - 1)
    def _():
        n, strip = strips(FFN_EPILOGUE_ROWS)

        def body(r, carry):
            y = x_ref[strip(r), :] + 0.5 * acc_ref[strip(r), :]
            acc_ref[strip(r), :] = _rms(y, g2_ref[...]) if norm_out else y
            if emit_h:
                outs[1][strip(r), :] = _rms(y, g2_ref[...]).astype(BF16)
            return carry

        lax.fori_loop(0, n, body, 0)


def _ffn(x, gain, wg, wu, wd, gain2, *, emit_h=False, norm_out=False, cast=(), cast_layer=0):
    t, d = x.shape
    f = wg.shape[-1]
    bm, tf = min(FFN_ROWS, t), FFN_COLS
    assert t % bm == 0 and f % tf == 0
    ni, nj = t // bm, f // tf
    in_specs = [
        pl.BlockSpec((bm, d), lambda i, j: (i, 0)),
        pl.BlockSpec((1, d), lambda i, j: (0, 0)),
        pl.BlockSpec((d, tf), lambda i, j: (0, j)),
        pl.BlockSpec((d, tf), lambda i, j: (0, j)),
        pl.BlockSpec((tf, d), lambda i, j: (j, 0)),
        pl.BlockSpec((1, d), lambda i, j: (0, 0)),
    ]
    out_shape = [jax.ShapeDtypeStruct((t, d), F32)]
    if emit_h:
        out_shape.append(jax.ShapeDtypeStruct((t, d), BF16))
    out_specs = [pl.BlockSpec((bm, d), lambda i, j: (i, 0)) for _ in out_shape]
    for w in cast:
        _, r, c = w.shape
        if r == d:
            tile, idx = (r // ni, c // nj), lambda i, j: (i, j)
        else:
            tile, idx = (r // nj, c // ni), lambda i, j: (j, i)
        assert tile[0] % BF16_SUBLANES == 0 and tile[1] % LANES == 0
        in_specs.append(
            pl.BlockSpec((None,) + tile, lambda i, j, idx=idx: (cast_layer,) + idx(i, j)))
        out_specs.append(pl.BlockSpec(tile, idx))
        out_shape.append(jax.ShapeDtypeStruct((r, c), BF16))
    return pl.pallas_call(
        functools.partial(_ffn_kernel, emit_h=emit_h, norm_out=norm_out, n_cast=len(cast)),
        grid=(ni, nj),
        in_specs=in_specs,
        out_specs=out_specs,
        out_shape=out_shape,
        scratch_shapes=[pltpu.VMEM((bm, d), BF16)],
        compiler_params=_params("parallel", "arbitrary"),
        name="ffn",
    )(x, gain, wg, wu, wd, gain2, *cast)


def _proj_t_kernel(h_ref, w_ref, o_ref, *, scale, scaled_rows, bn):
    y = _dot(h_ref[...], w_ref[...])
    if scaled_rows:
        j = pl.program_id(1)
        y = y * jnp.where(j * bn < scaled_rows, scale, 1.0).astype(F32)
    o_ref[...] = jnp.transpose(y).astype(o_ref.dtype)


def _proj_t(h, w, layer, out_dtype, *, scale=1.0, scaled_rows=0):
    t, d = h.shape
    n = w.shape[-1]
    bm, bn = min(MM_ROWS, t), min(MM_COLS, n)
    assert t % bm == 0 and n % bn == 0 and scaled_rows % bn == 0
    return pl.pallas_call(
        functools.partial(_proj_t_kernel, scale=scale, scaled_rows=scaled_rows, bn=bn),
        grid=(t // bm, n // bn),
        in_specs=[
            pl.BlockSpec((bm, d), lambda i, j: (i, 0)),
            pl.BlockSpec((None, d, bn), lambda i, j: (layer, 0, j)),
        ],
        out_specs=pl.BlockSpec((bn, bm), lambda i, j: (j, i)),
        out_shape=jax.ShapeDtypeStruct((n, t), out_dtype),
        compiler_params=_params("parallel", "arbitrary"),
        name="proj_t",
    )(h, w)


def _proj_kc_kernel(h_ref, w_ref, b_ref, k_ref, c_ref, carry_sc, *, blocks_per_seq):
    y = _dot(h_ref[...], w_ref[...])
    nk = k_ref.shape[1]
    k_ref[...] = y[:, :nk].astype(k_ref.dtype)

    @pl.when(pl.program_id(0) % blocks_per_seq == 0)
    def _():
        carry_sc[...] = jnp.zeros_like(carry_sc)

    blk = CUMSUM_ROWS
    r = lax.broadcasted_iota(jnp.int32, (blk, blk), 0)
    c = lax.broadcasted_iota(jnp.int32, (blk, blk), 1)
    tril = jnp.where(r >= c, 1.0, 0.0).astype(F32)
    carry = carry_sc[...]
    for i in range(h_ref.shape[0] // blk):
        z = y[i * blk:(i + 1) * blk, nk:] + b_ref[...]
        log_f = jnp.minimum(z, 0.0) - jnp.log1p(jnp.exp(-jnp.abs(z)))
        cs = jnp.dot(tril, log_f, precision=lax.Precision.HIGHEST,
                     preferred_element_type=F32) + carry
        c_ref[i * blk:(i + 1) * blk, :] = cs
        carry = cs[blk - 1:blk, :]
    carry_sc[...] = carry


def _proj_kc(h, w, bias, layer, nk, s):
    t, d = h.shape
    n = w.shape[-1]
    bm = min(MM_ROWS, s)
    assert s % bm == 0 and bm % CUMSUM_ROWS == 0
    return pl.pallas_call(
        functools.partial(_proj_kc_kernel, blocks_per_seq=s // bm),
        grid=(t // bm,),
        in_specs=[
            pl.BlockSpec((bm, d), lambda i: (i, 0)),
            pl.BlockSpec((None, d, n), lambda i: (layer, 0, 0)),
            pl.BlockSpec((1, n - nk), lambda i: (0, 0)),
        ],
        out_specs=[
            pl.BlockSpec((bm, nk), lambda i: (i, 0)),
            pl.BlockSpec((bm, n - nk), lambda i: (i, 0)),
        ],
        out_shape=[
            jax.ShapeDtypeStruct((t, nk), BF16),
            jax.ShapeDtypeStruct((t, n - nk), F32),
        ],
        scratch_shapes=[pltpu.VMEM((1, n - nk), F32)],
        compiler_params=_params("arbitrary"),
        name="proj_kc",
    )(h, w, bias)


LOG2E = float(np.log2(np.e))
BIAS_TERMS = 3
AUG_ROWS = BF16_SUBLANES


def _split_bf16(x):
    hi = x.astype(BF16).astype(F32)
    r = x - hi
    mid = r.astype(BF16).astype(F32)
    lo = (r - mid).astype(BF16).astype(F32)
    return hi, mid, lo


def _bias_rows(c, idx, sign):
    hi, mid, lo = _split_bf16(c)
    first = 0 if sign > 0 else BIAS_TERMS
    ones_first = BIAS_TERMS if sign > 0 else 0
    out = jnp.where((idx >= ones_first) & (idx < ones_first + BIAS_TERMS), 1.0, 0.0)
    out = jnp.where(idx == first, sign * hi, out)
    out = jnp.where(idx == first + 1, sign * mid, out)
    out = jnp.where(idx == first + 2, sign * lo, out)
    return out.astype(BF16)


def _fox_kernel(qt_ref, k_ref, vt_ref, c_ref, g_ref, o_ref, kaug_sc, vaug_sc, qaug_sc, s_sc, m_sc, acc_sc,
                *, bq, bk, hps):
    qi = pl.program_id(2)
    s_len = k_ref.shape[0]
    d = HEAD_DIM

    @pl.when(qi == 0)
    def _():
        lane = lax.broadcasted_iota(jnp.int32, (d, d), 1)
        for hh in range(hps):
            kaug_sc[hh, :, 0:d] = k_ref[:, hh * d:(hh + 1) * d]
            for i in range(s_len // d):
                c_row = c_ref[hh, :, i * d:(i + 1) * d] * LOG2E
                c_col = jnp.transpose(jnp.broadcast_to(c_row, (d, d)))
                kaug_sc[hh, i * d:(i + 1) * d, d:2 * d] = _bias_rows(c_col, lane, -1.0)
            vaug_sc[hh, 0:d, :] = vt_ref[hh * d:(hh + 1) * d, :]
            ones_row = lax.broadcasted_iota(jnp.int32, (AUG_ROWS, s_len), 0) == 0
            vaug_sc[hh, d:, :] = jnp.where(ones_row, 1.0, 0.0).astype(BF16)

    q0 = pl.multiple_of(qi * bq, bq)
    sub = lax.broadcasted_iota(jnp.int32, (AUG_ROWS, bq), 0)
    for hh in range(hps):
        cq = jnp.broadcast_to(c_ref[hh, :, pl.ds(q0, bq)] * LOG2E, (AUG_ROWS, bq))
        qaug_sc[hh, 0:d, :] = qt_ref[hh * d:(hh + 1) * d, :]
        qaug_sc[hh, d:d + AUG_ROWS, :] = _bias_rows(cq, sub, 1.0)
        qaug_sc[hh, d + AUG_ROWS:, :] = jnp.zeros((d - AUG_ROWS, bq), BF16)
    m_sc[...] = jnp.full_like(m_sc, -jnp.inf)
    acc_sc[...] = jnp.zeros_like(acc_sc)

    def scores(ki, slot, lo, width):
        k0 = pl.multiple_of(ki * bk, bk)
        for hh in range(hps):
            s_sc[slot, hh, :, lo:lo + width] = _dot(kaug_sc[hh, pl.ds(k0, bk), :],
                                                    qaug_sc[hh, :, lo:lo + width])

    def absorb(ki, slot, lo, width, masked):
        cols = slice(lo, lo + width)
        k0 = pl.multiple_of(ki * bk, bk)
        for hh in range(hps):
            s = s_sc[slot, hh, :, cols]
            if masked:
                r = lax.broadcasted_iota(jnp.int32, (bk, width), 0)
                c = lax.broadcasted_iota(jnp.int32, (bk, width), 1)
                s = jnp.where(r - c <= q0 + lo - k0, s, -jnp.inf)
            m_prev = m_sc[hh, :, cols]
            m_new = jnp.maximum(m_prev, jnp.max(s, axis=0, keepdims=True))
            alpha = jnp.exp2(m_prev - m_new)
            p = jnp.exp2(s - m_new)
            pv = _dot(vaug_sc[hh, :, pl.ds(k0, bk)], p.astype(BF16))
            acc_sc[hh, :, cols] = alpha * acc_sc[hh, :, cols] + pv
            m_sc[hh, :, cols] = m_new

    scores(0, 0, 0, bq)

    def body(pair, carry):
        ki = 2 * pair
        scores(ki + 1, 1, 0, bq)
        absorb(ki, 0, 0, bq, False)
        scores(ki + 2, 0, 0, bq)
        absorb(ki + 1, 1, 0, bq, False)
        return carry

    lax.fori_loop(0, qi, body, 0)
    scores(2 * qi + 1, 1, bk, bk)
    absorb(2 * qi, 0, 0, bq, True)
    absorb(2 * qi + 1, 1, bk, bk, True)
    for hh in range(hps):
        y_t = acc_sc[hh, 0:d, :] * (1.0 / acc_sc[hh, d:d + 1, :])
        y_t = y_t * lax.rsqrt(jnp.mean(y_t * y_t, axis=0, keepdims=True) + EPS)
        o_ref[:, hh * d:(hh + 1) * d] = (
            jnp.transpose(y_t) * g_ref[:, hh * d:(hh + 1) * d]).astype(o_ref.dtype)


def _fox(qvt, k, c_row, gain, heads):
    b, s, _ = k.shape
    blk = min(FOX_Q_ROWS, s)
    nq = s // blk
    hps = FOX_HEADS_PER_STEP
    w = hps * HEAD_DIM
    assert s % blk == 0 and blk % 2 == 0 and heads % hps == 0
    return pl.pallas_call(
        functools.partial(_fox_kernel, bq=blk, bk=blk // 2, hps=hps),
        grid=(b, heads // hps, nq),
        in_specs=[
            pl.BlockSpec((w, blk), lambda bi, h, qi: (h, bi * nq + qi)),
            pl.BlockSpec((None, s, w), lambda bi, h, qi: (bi, 0, h)),
            pl.BlockSpec((w, s), lambda bi, h, qi: (heads // hps + h, bi)),
            pl.BlockSpec((None, hps, 1, s), lambda bi, h, qi: (bi, h, 0, 0)),
            pl.BlockSpec((1, w), lambda bi, h, qi: (0, h)),
        ],
        out_specs=pl.BlockSpec((None, blk, w), lambda bi, h, qi: (bi, qi, h)),
        out_shape=jax.ShapeDtypeStruct((b, s, heads * HEAD_DIM), BF16),
        scratch_shapes=[
            pltpu.VMEM((hps, s, 2 * HEAD_DIM), BF16),
            pltpu.VMEM((hps, HEAD_DIM + AUG_ROWS, s), BF16),
            pltpu.VMEM((hps, 2 * HEAD_DIM, blk), BF16),
            pltpu.VMEM((2, hps, blk // 2, blk), F32),
            pltpu.VMEM((hps, 1, blk), F32),
            pltpu.VMEM((hps, HEAD_DIM + AUG_ROWS, blk), F32),
        ],
        compiler_params=_params("parallel", "parallel", "arbitrary"),
        name="fox",
    )(qvt, k, qvt, c_row, gain)


def _ret_kernel(h_ref, w_ref, cos_ref, sin_ref, gain_ref, o_ref, p_sc, state_sc,
                *, rows, chunk, heads):
    n = pl.program_id(1)
    d = HEAD_DIM
    w = heads * d

    @pl.when(n == 0)
    def _():
        state_sc[...] = jnp.zeros_like(state_sc)

    p_sc[...] = _dot(h_ref[...], w_ref[...])
    t = lax.broadcasted_iota(jnp.int32, (chunk, 1), 0).astype(F32)
    r = lax.broadcasted_iota(jnp.int32, (chunk, chunk), 0)
    c = lax.broadcasted_iota(jnp.int32, (chunk, chunk), 1)
    diff = (r - c).astype(F32)
    for hh in range(heads):
        lg = float(np.log1p(-np.exp2(-(RET_GAMMA_BASE + hh))))
        lo, hi = hh * d, (hh + 1) * d
        xi = jnp.exp((t + 1.0) * lg)
        zeta = jnp.exp((chunk - 1.0 - t) * lg)
        decay = jnp.where(diff >= 0.0, jnp.exp(jnp.maximum(diff, 0.0) * lg), 0.0)
        for ci in range(rows // chunk):
            rs = slice(ci * chunk, (ci + 1) * chunk)
            cos = cos_ref[rs, :]
            sin = sin_ref[rs, :]

            def rope(x):
                return x * cos + pltpu.roll(x, d // 2, 1) * sin

            q = rope(p_sc[rs, lo:hi])
            k = rope(p_sc[rs, w + lo:w + hi]) * (d ** -0.5)
            v = p_sc[rs, 2 * w + lo:2 * w + hi].astype(BF16)
            gate = p_sc[rs, 3 * w + lo:3 * w + hi]
            qb = q.astype(BF16)
            scores = _dot_nt(qb, k.astype(BF16)) * decay
            state = state_sc[hh]
            o = _dot(scores.astype(BF16), v) + _dot(qb, state.astype(BF16)) * xi
            kz_t = jnp.transpose(k * zeta).astype(BF16)
            state_sc[hh] = state * float(np.exp(chunk * lg)) + _dot(kz_t, v)
            y = _rms(o, gain_ref[:, lo:hi]) * (gate * jax.nn.sigmoid(gate))
            o_ref[rs, lo:hi] = y.astype(o_ref.dtype)


def _retention(h, w_ret, layer, cos2, sin2, gain, heads, b, s):
    t, dm = h.shape
    rows = min(RET_ROWS, s)
    chunk = min(RET_CHUNK, rows)
    nc = s // rows
    w = heads * HEAD_DIM
    assert s % rows == 0 and rows % chunk == 0 and t == b * s
    return pl.pallas_call(
        functools.partial(_ret_kernel, rows=rows, chunk=chunk, heads=heads),
        grid=(b, nc),
        in_specs=[
            pl.BlockSpec((rows, dm), lambda bi, n: (bi * nc + n, 0)),
            pl.BlockSpec((None, dm, 4 * w), lambda bi, n: (layer, 0, 0)),
            pl.BlockSpec((rows, HEAD_DIM), lambda bi, n: (n, 0)),
            pl.BlockSpec((rows, HEAD_DIM), lambda bi, n: (n, 0)),
            pl.BlockSpec((1, w), lambda bi, n: (0, 0)),
        ],
        out_specs=pl.BlockSpec((rows, w), lambda bi, n: (bi * nc + n, 0)),
        out_shape=jax.ShapeDtypeStruct((t, w), BF16),
        scratch_shapes=[
            pltpu.VMEM((rows, 4 * w), F32),
            pltpu.VMEM((heads, HEAD_DIM, HEAD_DIM), F32),
        ],
        compiler_params=_params("parallel", "arbitrary"),
        name="retention",
    )(h, w_ret, cos2, sin2, gain)


def _gelu(x):
    return 0.5 * x * (1.0 + jnp.tanh(np.sqrt(2.0 / np.pi).astype(np.float32) * (x + 0.044715 * (x * x * x))))


def _gmlp_kernel(h_ref, w_ref, lng_ref, lnb_ref, ws_ref, bs_ref, gain_ref, o_ref, uv_ref,
                 *, groups, rows):
    blk = CHUNK_GMLP
    pair = 2 * HEAD_DIM
    for g in range(groups):
        uv_ref[:, g * pair:(g + 1) * pair] = _dot(h_ref[...], w_ref[:, g * pair:(g + 1) * pair])
    r = lax.broadcasted_iota(jnp.int32, (blk, blk), 0)
    c = lax.broadcasted_iota(jnp.int32, (blk, blk), 1)
    for g in range(groups):
        lo, hi = g * HEAD_DIM, (g + 1) * HEAD_DIM
        u_lo, v_lo = g * pair, g * pair + HEAD_DIM
        wm = jnp.where(r >= c, ws_ref[g], 0.0).astype(BF16)
        bias = bs_ref[g]
        v = _gelu(uv_ref[:, v_lo:v_lo + HEAD_DIM])
        mu = jnp.mean(v, axis=-1, keepdims=True)
        var = jnp.mean(jnp.square(v - mu), axis=-1, keepdims=True)
        v = ((v - mu) * lax.rsqrt(var + EPS) * lng_ref[:, lo:hi] + lnb_ref[:, lo:hi]).astype(BF16)
        v_wide = jnp.concatenate([v[ci * blk:(ci + 1) * blk] for ci in range(rows // blk)], axis=1)
        mixed = _dot(wm, v_wide) + bias
        for ci in range(rows // blk):
            rs = slice(ci * blk, (ci + 1) * blk)
            y = _gelu(uv_ref[rs, u_lo:u_lo + HEAD_DIM]) * mixed[:, ci * HEAD_DIM:(ci + 1) * HEAD_DIM]
            o_ref[rs, lo:hi] = _rms(y, gain_ref[:, lo:hi]).astype(o_ref.dtype)


def _gmlp(h, w_gm, layer, ln_g, ln_b, w_s, b_s, gain, groups):
    t, dm = h.shape
    w = groups * HEAD_DIM
    rows = min(GMLP_ROWS, t)
    assert t % rows == 0 and rows % CHUNK_GMLP == 0
    return pl.pallas_call(
        functools.partial(_gmlp_kernel, groups=groups, rows=rows),
        grid=(t // rows,),
        in_specs=[
            pl.BlockSpec((rows, dm), lambda i: (i, 0)),
            pl.BlockSpec((None, dm, 2 * w), lambda i: (layer, 0, 0)),
            pl.BlockSpec((1, w), lambda i: (0, 0)),
            pl.BlockSpec((1, w), lambda i: (0, 0)),
            pl.BlockSpec((groups, CHUNK_GMLP, CHUNK_GMLP), lambda i: (0, 0, 0)),
            pl.BlockSpec((groups, CHUNK_GMLP, 1), lambda i: (0, 0, 0)),
            pl.BlockSpec((1, w), lambda i: (0, 0)),
        ],
        out_specs=pl.BlockSpec((rows, w), lambda i: (i, 0)),
        out_shape=jax.ShapeDtypeStruct((t, w), BF16),
        scratch_shapes=[pltpu.VMEM((rows, 2 * w), F32)],
        compiler_params=_params("parallel"),
        name="gmlp",
    )(h, w_gm, ln_g, ln_b, w_s, b_s, gain)


def _ret_gmlp_kernel(h_ref, wr_ref, cos_ref, sin_ref, rgain_ref, wg_ref, lng_ref, lnb_ref,
                     ws_ref, bs_ref, ggain_ref, or_ref, og_ref, p_sc, state_sc, uv_sc,
                     *, rows, chunk, heads, groups):
    _ret_kernel(h_ref, wr_ref, cos_ref, sin_ref, rgain_ref, or_ref, p_sc, state_sc,
                rows=rows, chunk=chunk, heads=heads)
    _gmlp_kernel(h_ref, wg_ref, lng_ref, lnb_ref, ws_ref, bs_ref, ggain_ref, og_ref, uv_sc,
                 groups=groups, rows=rows)


def _ret_gmlp(h, w_ret, w_gm, layer, cos2, sin2, ret_gain, heads, ln_g, ln_b, w_s, b_s, gm_gain,
              groups, b, s):
    t, dm = h.shape
    rows = min(RET_ROWS, s)
    chunk = min(RET_CHUNK, rows)
    nc = s // rows
    wr, wg = heads * HEAD_DIM, groups * HEAD_DIM
    assert s % rows == 0 and rows % chunk == 0 and rows % CHUNK_GMLP == 0 and t == b * s
    row_block = lambda width: pl.BlockSpec((rows, width), lambda bi, n: (bi * nc + n, 0))
    const = lambda shape: pl.BlockSpec(shape, lambda bi, n: (0,) * len(shape))
    return pl.pallas_call(
        functools.partial(_ret_gmlp_kernel, rows=rows, chunk=chunk, heads=heads, groups=groups),
        grid=(b, nc),
        in_specs=[
            row_block(dm),
            pl.BlockSpec((None, dm, 4 * wr), lambda bi, n: (layer, 0, 0)),
            pl.BlockSpec((rows, HEAD_DIM), lambda bi, n: (n, 0)),
            pl.BlockSpec((rows, HEAD_DIM), lambda bi, n: (n, 0)),
            const((1, wr)),
            pl.BlockSpec((None, dm, 2 * wg), lambda bi, n: (layer, 0, 0)),
            const((1, wg)),
            const((1, wg)),
            const((groups, CHUNK_GMLP, CHUNK_GMLP)),
            const((groups, CHUNK_GMLP, 1)),
            const((1, wg)),
        ],
        out_specs=[row_block(wr), row_block(wg)],
        out_shape=[jax.ShapeDtypeStruct((t, wr), BF16), jax.ShapeDtypeStruct((t, wg), BF16)],
        scratch_shapes=[
            pltpu.VMEM((rows, 4 * wr), F32),
            pltpu.VMEM((heads, HEAD_DIM, HEAD_DIM), F32),
            pltpu.VMEM((rows, 2 * wg), F32),
        ],
        compiler_params=_params("parallel", "arbitrary"),
        name="ret_gmlp",
    )(h, w_ret, cos2, sin2, ret_gain, w_gm, ln_g, ln_b, w_s, b_s, gm_gain)


def _outproj_kernel(x_ref, ya_ref, yb_ref, yc_ref, w_ref, o_ref):
    wa, wb = ya_ref.shape[1], yb_ref.shape[1]
    acc = _dot(ya_ref[...], w_ref[0:wa, :])
    acc += _dot(yb_ref[...], w_ref[wa:wa + wb, :])
    acc += _dot(yc_ref[...], w_ref[wa + wb:, :])
    o_ref[...] = x_ref[...] + acc


def _outproj(x, ya, yb, yc, w, layer):
    t, d = x.shape
    k = w.shape[-2]
    bm, bn = min(OUT_ROWS, t), min(OUT_COLS, d)
    assert t % bm == 0 and d % bn == 0
    return pl.pallas_call(
        _outproj_kernel,
        grid=(t // bm, d // bn),
        in_specs=[
            pl.BlockSpec((bm, bn), lambda i, j: (i, j)),
            pl.BlockSpec((bm, ya.shape[1]), lambda i, j: (i, 0)),
            pl.BlockSpec((bm, yb.shape[1]), lambda i, j: (i, 0)),
            pl.BlockSpec((bm, yc.shape[1]), lambda i, j: (i, 0)),
            pl.BlockSpec((None, k, bn), lambda i, j: (layer, 0, j)),
        ],
        out_specs=pl.BlockSpec((bm, bn), lambda i, j: (i, j)),
        out_shape=jax.ShapeDtypeStruct((t, d), F32),
        compiler_params=_params("parallel", "arbitrary"),
        name="outproj",
    )(x, ya, yb, yc, w)


def kernel(x, ffn1_norm, ffn1_w_gate, ffn1_w_up, ffn1_w_down, mix_norm, w_in, fox_b_f,
           gmlp_ln_g, gmlp_ln_b, gmlp_w_s, gmlp_b_s, out_norm, w_out, ffn2_norm,
           ffn2_w_gate, ffn2_w_up, ffn2_w_down, final_norm):
    b, s, d = x.shape
    depth = w_in.shape[0]
    t = b * s
    n_heads = d // HEAD_DIM
    fox_h, ret_h = n_heads // 2, n_heads // 4
    gm_g = n_heads - fox_h - ret_h
    fox_w, ret_w, gm_w = fox_h * HEAD_DIM, ret_h * HEAD_DIM, gm_g * HEAD_DIM

    o_fz = 3 * fox_w
    o_ret = o_fz + fox_h
    o_gm = o_ret + 4 * ret_w
    w_qv = jnp.concatenate([w_in[:, :, :fox_w], w_in[:, :, 2 * fox_w:o_fz]], axis=2).astype(BF16)
    w_kz = jnp.concatenate(
        [w_in[:, :, fox_w:2 * fox_w],
         jnp.pad(w_in[:, :, o_fz:o_ret], ((0, 0), (0, 0), (0, HEAD_DIM - fox_h)))], axis=2
    ).astype(BF16)
    w_ret = w_in[:, :, o_ret:o_gm].astype(BF16)
    w_gm = jnp.concatenate(
        [w_in[:, :, o_gm + side * gm_w + g * HEAD_DIM:o_gm + side * gm_w + (g + 1) * HEAD_DIM]
         for g in range(gm_g) for side in (0, 1)], axis=2).astype(BF16)
    w_o = w_out.astype(BF16)
    ffn_f32 = (ffn1_w_gate, ffn1_w_up, ffn1_w_down, ffn2_w_gate, ffn2_w_up, ffn2_w_down)
    ffn_w = [w[0].astype(BF16) for w in ffn_f32]
    fz_bias = jnp.pad(fox_b_f, ((0, 0), (0, HEAD_DIM - fox_h)))

    half = HEAD_DIM // 2
    pos = jnp.arange(s, dtype=F32)
    inv_freq = ROPE_BASE ** (-jnp.arange(half, dtype=F32) / half)
    ang = pos[:, None] * inv_freq[None, :]
    cos2 = jnp.concatenate([jnp.cos(ang), jnp.cos(ang)], axis=-1)
    sin2 = jnp.concatenate([-jnp.sin(ang), jnp.sin(ang)], axis=-1)

    xf = x.reshape(t, d)
    for l in range(depth):
        x1, h = _ffn(xf, ffn1_norm[l][None], *ffn_w[:3], mix_norm[l][None], emit_h=True)
        qvt = _proj_t(h, w_qv, l, BF16, scale=HEAD_DIM ** -0.5 * LOG2E, scaled_rows=fox_w)
        kf, c = _proj_kc(h, w_kz, fz_bias[l][None], l, fox_w, s)
        c = jnp.transpose(c.reshape(b, s, HEAD_DIM)[:, :, :fox_h], (0, 2, 1))
        gains = out_norm[l][None]
        ya = _fox(qvt, kf.reshape(b, s, fox_w), c[:, :, None, :], gains[:, :fox_w], fox_h)
        yb, yc = _ret_gmlp(h, w_ret, w_gm, l, cos2, sin2, gains[:, fox_w:fox_w + ret_w], ret_h,
                           gmlp_ln_g[l][None], gmlp_ln_b[l][None], gmlp_w_s[l],
                           gmlp_b_s[l][..., None], gains[:, fox_w + ret_w:], gm_g, b, s)
        x2 = _outproj(x1, ya.reshape(t, fox_w), yb, yc, w_o, l)

        last = l == depth - 1
        xf, *next_w = _ffn(x2, ffn2_norm[l][None], *ffn_w[3:], final_norm[None], norm_out=last,
                           cast=() if last else ffn_f32, cast_layer=l + 1)
        ffn_w = next_w or ffn_w
    return xf.reshape(b, s, d)
```
